```python
import jax, jax.numpy as jnp
from jax import lax
import numpy as np

D_MODEL = 2048
BATCH = 4
SEQ = 4096
DEPTH = 1

GRID_W = 64
CTX_LEN = 256
EPS = 1e-6
N_MOD = 6
MLA_HEADS = D_MODEL // 256
Q_LORA = D_MODEL // 4
KV_LORA = D_MODEL // 4
QK_NOPE = 128
QK_ROPE = 64
V_HEAD = 128
ROPE_THETA = 10000.0
Q_BLOCK = 128
ATTN_SCALE = (QK_NOPE + QK_ROPE) ** -0.5
SGU_HEADS = D_MODEL // 256
SGU_HEAD_DIM = 128
CHUNK = 128
MLA_WIDTH = MLA_HEADS * V_HEAD
SGU_WIDTH = SGU_HEADS * SGU_HEAD_DIM
MIX_WIDTH = MLA_WIDTH + SGU_WIDTH
OFF_KV = Q_LORA
OFF_KPE = OFF_KV + KV_LORA
OFF_U = OFF_KPE + QK_ROPE
OFF_V = OFF_U + SGU_WIDTH
IN_COLS = OFF_V + SGU_WIDTH
N_EXPERTS = 16
CAP_FACTOR = 2
EXPERT_FF = D_MODEL

kernel_name = "hybrid_mla_sgu_ecmoe_diffusion_block"


def rmsnorm(x, w):
    xf = x.astype(jnp.float32)
    y = xf * lax.rsqrt(jnp.mean(xf * xf, axis=-1, keepdims=True) + EPS)
    return (y * w.astype(jnp.float32)).astype(x.dtype)


def modulate(h, shift, scale):
    return h * (1 + scale) + shift


def axial_angles(length):
    rows = length // GRID_W
    row = jnp.repeat(jnp.arange(rows, dtype=jnp.int32), GRID_W)
    col = jnp.tile(jnp.arange(GRID_W, dtype=jnp.int32), rows)
    half = QK_ROPE // 2
    inv = 1.0 / (ROPE_THETA ** (jnp.arange(0, half, 2, dtype=jnp.float32) / half))
    return row.astype(jnp.float32)[:, None] * inv, col.astype(jnp.float32)[:, None] * inv


def rope_1d(x, ang):
    xf = x.astype(jnp.float32)
    x1, x2 = jnp.split(xf, 2, axis=-1)
    cos, sin = jnp.cos(ang), jnp.sin(ang)
    return jnp.concatenate([x1 * cos - x2 * sin, x2 * cos + x1 * sin], axis=-1).astype(x.dtype)


def axial_rope(x, ang_r, ang_c):
    xr, xc = jnp.split(x, 2, axis=-1)
    return jnp.concatenate([rope_1d(xr, ang_r), rope_1d(xc, ang_c)], axis=-1)


def mla_q(c_q, q_norm_w, w_uq, angles):
    B, L, _ = c_q.shape
    q = (rmsnorm(c_q, q_norm_w) @ w_uq).reshape(B, L, MLA_HEADS, QK_NOPE + QK_ROPE)
    q_nope, q_pe = jnp.split(q, [QK_NOPE], axis=-1)
    if angles is not None:
        q_pe = axial_rope(q_pe, angles[0][:, None], angles[1][:, None])
    return jnp.concatenate([q_nope, q_pe], axis=-1)


def mla_kv(c_kv, k_pe, kv_norm_w, w_ukv, angles):
    B, L, _ = c_kv.shape
    kv = (rmsnorm(c_kv, kv_norm_w) @ w_ukv).reshape(B, L, MLA_HEADS, QK_NOPE + V_HEAD)
    k_nope, v = jnp.split(kv, [QK_NOPE], axis=-1)
    if angles is not None:
        k_pe = axial_rope(k_pe, angles[0], angles[1])
    k_pe = jnp.broadcast_to(k_pe[:, :, None, :], (B, L, MLA_HEADS, QK_ROPE))
    return jnp.concatenate([k_nope, k_pe], axis=-1), v


def attend(q, k, v):
    s = jnp.einsum('bqhd,bkhd->bhqk', q, k, preferred_element_type=jnp.float32) * ATTN_SCALE
    p = jax.nn.softmax(s, axis=-1).astype(v.dtype)
    return jnp.einsum('bhqk,bkhd->bqhd', p, v)


def blocked_attention(q, k, v):
    B, L, H, dq = q.shape
    nb = L // Q_BLOCK
    qb = q.reshape(B, nb, Q_BLOCK, H, dq).transpose(1, 0, 2, 3, 4)
    out = lax.map(lambda qi: attend(qi, k, v), qb)
    return out.transpose(1, 0, 2, 3, 4).reshape(B, L, H, v.shape[-1])


def spatial_gating(u, v, sgu_norm_w, sgu_w, sgu_b):
    B, L, _ = u.shape
    n = L // CHUNK
    u = jax.nn.gelu(u).reshape(B, n, CHUNK, SGU_HEADS, SGU_HEAD_DIM)
    vf = jax.nn.gelu(v).reshape(B, n, CHUNK, SGU_HEADS, SGU_HEAD_DIM).astype(jnp.float32)
    mu = jnp.mean(vf, axis=-1, keepdims=True)
    var = jnp.mean(jnp.square(vf - mu), axis=-1, keepdims=True)
    vn = ((vf - mu) * lax.rsqrt(var + EPS)
          * sgu_norm_w.astype(jnp.float32).reshape(SGU_HEADS, SGU_HEAD_DIM)).astype(u.dtype)
    mixed = jnp.einsum('gpq,bnqgc->bnpgc', sgu_w, vn) + sgu_b.T[:, :, None]
    return (u * mixed).reshape(B, L, SGU_WIDTH)


def ec_moe(h, w_router, b_router, w_gate, w_up, w_down):
    B, L, D = h.shape
    cap = CAP_FACTOR * L // N_EXPERTS
    logits = jnp.einsum('bld,de->ble', h, w_router).astype(jnp.float32) + b_router.astype(jnp.float32)
    aff = jax.nn.softmax(logits, axis=-1)
    gate, idx = lax.top_k(jnp.swapaxes(aff, 1, 2), cap)
    xs = jax.vmap(lambda hb, ib: hb[ib])(h, idx)
    a = jnp.einsum('becd,edf->becf', xs, w_gate)
    g = jnp.einsum('becd,edf->becf', xs, w_up)
    ye = jnp.einsum('becf,efd->becd', jax.nn.silu(a) * g, w_down) * gate[..., None].astype(h.dtype)
    scatter = lambda ib, yb: jnp.zeros((L, D), yb.dtype).at[ib.reshape(-1)].add(yb.reshape(-1, D))
    return jax.vmap(scatter)(idx, ye)


def setup_inputs(seed: int = 0) -> dict:
    key = jax.random.key(seed)
    ks = iter(jax.random.split(key, 32))
    nrm = lambda shape, scale: jax.random.normal(next(ks), shape, jnp.float32) * scale
    gain = lambda shape: 1.0 + nrm(shape, 0.05)
    return {
        "x": nrm((BATCH, SEQ, D_MODEL), 1.0),
        "c": nrm((BATCH, D_MODEL), 1.0),
        "ctx": nrm((BATCH, CTX_LEN, D_MODEL), 1.0),
        "c_ctx": nrm((D_MODEL,), 1.0),
        "w_ada": nrm((DEPTH, D_MODEL, N_MOD * D_MODEL), 0.5 * D_MODEL ** -0.5),
        "b_ada": nrm((DEPTH, N_MOD * D_MODEL), 0.02),
        "pre_norm1": gain((DEPTH, D_MODEL)),
        "w_in": nrm((DEPTH, D_MODEL, IN_COLS), D_MODEL ** -0.5),
        "q_norm_w": gain((DEPTH, Q_LORA)),
        "w_uq": nrm((DEPTH, Q_LORA, MLA_HEADS * (QK_NOPE + QK_ROPE)), Q_LORA ** -0.5),
        "kv_norm_w": gain((DEPTH, KV_LORA)),
        "w_ukv": nrm((DEPTH, KV_LORA, MLA_HEADS * (QK_NOPE + V_HEAD)), KV_LORA ** -0.5),
        "sgu_norm_w": gain((DEPTH, SGU_WIDTH)),
        "sgu_w": nrm((DEPTH, SGU_HEADS, CHUNK, CHUNK), CHUNK ** -0.5),
        "sgu_b": 1.0 + nrm((DEPTH, SGU_HEADS, CHUNK), 0.1),
        "w_out": nrm((DEPTH, MIX_WIDTH, D_MODEL), MIX_WIDTH ** -0.5),
        "post_norm1": gain((DEPTH, D_MODEL)),
        "pre_norm2": gain((DEPTH, D_MODEL)),
        "w_router": nrm((DEPTH, D_MODEL, N_EXPERTS), D_MODEL ** -0.5),
        "b_router": nrm((DEPTH, N_EXPERTS), 0.01),
        "w_e_gate": nrm((DEPTH, N_EXPERTS, D_MODEL, EXPERT_FF), D_MODEL ** -0.5),
        "w_e_up": nrm((DEPTH, N_EXPERTS, D_MODEL, EXPERT_FF), D_MODEL ** -0.5),
        "w_e_down": nrm((DEPTH, N_EXPERTS, EXPERT_FF, D_MODEL), EXPERT_FF ** -0.5),
        "post_norm2": gain((DEPTH, D_MODEL)),
    }


def reference(x, c, ctx, c_ctx, w_ada, b_ada, pre_norm1, w_in, q_norm_w, w_uq, kv_norm_w,
              w_ukv, sgu_norm_w, sgu_w, sgu_b, w_out, post_norm1, pre_norm2, w_router,
              b_router, w_e_gate, w_e_up, w_e_down, post_norm2):
    B, L, _ = x.shape
    angles = axial_angles(L)
    for l in range(DEPTH):
        last = l == DEPTH - 1
        mod = jax.nn.silu(c) @ w_ada[l] + b_ada[l]
        mod_c = jax.nn.silu(c_ctx) @ w_ada[l] + b_ada[l]
        sh1, sc1, g1, sh2, sc2, g2 = jnp.split(mod[:, None, :], N_MOD, axis=-1)
        sh1c, sc1c, g1c, sh2c, sc2c, g2c = jnp.split(mod_c, N_MOD, axis=-1)

        h = modulate(rmsnorm(x, pre_norm1[l]), sh1, sc1)
        hc = modulate(rmsnorm(ctx, pre_norm1[l]), sh1c, sc1c)
        p = h @ w_in[l]
        c_q, c_kv, k_pe, u, v_s = jnp.split(p, [OFF_KV, OFF_KPE, OFF_U, OFF_V], axis=-1)
        pc_kv = hc @ w_in[l][:, OFF_KV:OFF_U]
        c_kv_c, k_pe_c = jnp.split(pc_kv, [KV_LORA], axis=-1)

        q = mla_q(c_q, q_norm_w[l], w_uq[l], angles)
        k, v = mla_kv(c_kv, k_pe, kv_norm_w[l], w_ukv[l], angles)
        k_c, v_c = mla_kv(c_kv_c, k_pe_c, kv_norm_w[l], w_ukv[l], None)
        attn = blocked_attention(q, jnp.concatenate([k, k_c], axis=1),
                                 jnp.concatenate([v, v_c], axis=1))
        sgu = spatial_gating(u, v_s, sgu_norm_w[l], sgu_w[l], sgu_b[l])
        y = jnp.concatenate([attn.reshape(B, L, MLA_WIDTH), sgu], axis=-1) @ w_out[l]
        x_new = x + g1 * rmsnorm(y, post_norm1[l])

        if not last:
            pc = hc @ w_in[l]
            c_q_c, _, _, u_c, v_sc = jnp.split(pc, [OFF_KV, OFF_KPE, OFF_U, OFF_V], axis=-1)
            q_c = mla_q(c_q_c, q_norm_w[l], w_uq[l], None)
            attn_c = attend(q_c, k_c, v_c)
            sgu_c = spatial_gating(u_c, v_sc, sgu_norm_w[l], sgu_w[l], sgu_b[l])
            y_c = jnp.concatenate([attn_c.reshape(B, CTX_LEN, MLA_WIDTH), sgu_c], axis=-1) @ w_out[l]
            ctx = ctx + g1c * rmsnorm(y_c, post_norm1[l])
            h2c = modulate(rmsnorm(ctx, pre_norm2[l]), sh2c, sc2c)
            y2c = ec_moe(h2c, w_router[l], b_router[l], w_e_gate[l], w_e_up[l], w_e_down[l])
            ctx = ctx + g2c * rmsnorm(y2c, post_norm2[l])

        x = x_new
        h2 = modulate(rmsnorm(x, pre_norm2[l]), sh2, sc2)
        y2 = ec_moe(h2, w_router[l], b_router[l], w_e_gate[l], w_e_up[l], w_e_down[l])
        x = x + g2 * rmsnorm(y2, post_norm2[l])
    return x
```

```python
import functools

import numpy as np
import jax
import jax.numpy as jnp
from jax import lax
from jax.experimental import pallas as pl
from jax.experimental.pallas import tpu as pltpu

F32 = jnp.float32
BF16 = jnp.bfloat16
I32 = jnp.int32

D = 2048
GRID_W = 64
EPS = 1e-6
N_MOD = 6
HEADS = 8
Q_LORA = 512
KV_LORA = 512
QK_NOPE = 128
QK_ROPE = 64
V_HEAD = 128
ROPE_THETA = 10000.0
ATTN_SCALE = (QK_NOPE + QK_ROPE) ** -0.5
SGU_HEADS = 8
SGU_DIM = 128
CHUNK = 128
MLA_WIDTH = HEADS * V_HEAD
SGU_WIDTH = SGU_HEADS * SGU_DIM
N_EXPERTS = 16
CAP_FACTOR = 2
EXPERT_FF = D

LANES = 128
SUBLANES = 8
ROW_TILES = D // LANES
QK_PAD = 256
VMEM_LIMIT = 56 * 1024 * 1024

C_Q = 0
C_KV = C_Q + Q_LORA
C_KPE = C_KV + KV_LORA
C_KPP = C_KPE + LANES
C_U = C_KPP + LANES
C_V = C_U + SGU_WIDTH
IN_EXT = C_V + SGU_WIDTH


def _rms(x, w):
    return x * lax.rsqrt(jnp.mean(x * x, axis=-1, keepdims=True) + EPS) * w


def _gelu_tanh(x):
    return 0.5 * x * (1.0 + jnp.tanh(np.sqrt(2.0 / np.pi).astype(np.float32) * (x + 0.044715 * (x * x * x))))


def _silu(x):
    return x * (1.0 / (1.0 + jnp.exp(-x)))


def _dot(a, b):
    return jnp.dot(a, b, preferred_element_type=F32)


def _ada_kernel(c_ref, w_ref, b_ref, o_ref):
    s = _silu(c_ref[...]).astype(BF16)
    o_ref[...] = _dot(s, w_ref[...].astype(BF16)) + b_ref[...]


def _ada(cc, w_ada, b_ada):
    n = w_ada.shape[1]
    tn = 1024
    return pl.pallas_call(
        _ada_kernel,
        grid=(n // tn,),
        in_specs=[pl.BlockSpec((SUBLANES, D), lambda j: (0, 0)),
                  pl.BlockSpec((D, tn), lambda j: (0, j)),
                  pl.BlockSpec((1, tn), lambda j: (0, j))],
        out_specs=pl.BlockSpec((SUBLANES, tn), lambda j: (0, j)),
        out_shape=jax.ShapeDtypeStruct((SUBLANES, n), F32),
        compiler_params=pltpu.CompilerParams(dimension_semantics=("arbitrary",), vmem_limit_bytes=VMEM_LIMIT),
        name="ada",
    )(cc, w_ada, b_ada.reshape(1, n))


def _kv_path(hb, win_ref, kvn_ref, wukv_ref, cos, sin, k_ref, v_ref):
    ckv = _dot(hb, win_ref[:, C_KV:C_KV + KV_LORA])
    ckvn = _rms(ckv, kvn_ref[...]).astype(BF16)
    kn = _dot(ckvn, wukv_ref[:, :HEADS * QK_NOPE])
    v_ref[...] = _dot(ckvn, wukv_ref[:, HEADS * QK_NOPE:]).astype(BF16)
    kpe = (_dot(hb, win_ref[:, C_KPE:C_KPE + LANES]) * cos
           + _dot(hb, win_ref[:, C_KPP:C_KPP + LANES]) * sin).astype(BF16)
    for h in range(HEADS):
        k_ref[:, h * QK_PAD:h * QK_PAD + QK_NOPE] = kn[:, h * QK_NOPE:(h + 1) * QK_NOPE].astype(BF16)
        k_ref[:, h * QK_PAD + QK_NOPE:(h + 1) * QK_PAD] = kpe


def _inproj_kernel(x_ref, mod_ref, pn_ref, win_ref, qn_ref, wuq_ref, kvn_ref, wukv_ref,
                   sgn_ref, sgw_ref, sgb_ref, cs_ref, q_ref, k_ref, v_ref, sgu_ref):
    tm = x_ref.shape[0]
    mod = mod_ref[0]
    h = _rms(x_ref[...], pn_ref[...]) * (1.0 + mod[1:2]) + mod[0:1]
    hb = h.astype(BF16)
    cos = cs_ref[:, :LANES]
    sin = cs_ref[:, LANES:]

    _kv_path(hb, win_ref, kvn_ref, wukv_ref, cos, sin, k_ref, v_ref)

    cq = _dot(hb, win_ref[:, C_Q:C_Q + Q_LORA])
    cqn = _rms(cq, qn_ref[...]).astype(BF16)
    for h_ in range(HEADS):
        qm = _dot(cqn, wuq_ref[:, h_ * QK_PAD:(h_ + 1) * QK_PAD])
        qp = _dot(cqn, wuq_ref[:, HEADS * QK_PAD + h_ * LANES:HEADS * QK_PAD + (h_ + 1) * LANES])
        q_ref[:, h_ * QK_PAD:h_ * QK_PAD + QK_NOPE] = (qm[:, :QK_NOPE] * ATTN_SCALE).astype(BF16)
        q_ref[:, h_ * QK_PAD + QK_NOPE:(h_ + 1) * QK_PAD] = (
            (qm[:, QK_NOPE:] * cos + qp * sin) * ATTN_SCALE).astype(BF16)

    for g in range(SGU_HEADS):
        gs = slice(g * SGU_DIM, (g + 1) * SGU_DIM)
        u = _gelu_tanh(_dot(hb, win_ref[:, C_U + g * SGU_DIM:C_U + (g + 1) * SGU_DIM]))
        vs = _gelu_tanh(_dot(hb, win_ref[:, C_V + g * SGU_DIM:C_V + (g + 1) * SGU_DIM]))
        mu = jnp.mean(vs, axis=-1, keepdims=True)
        vc = vs - mu
        var = jnp.mean(vc * vc, axis=-1, keepdims=True)
        vn = (vc * lax.rsqrt(var + EPS) * sgn_ref[:, gs]).astype(BF16)
        wg = sgw_ref[g]
        bias = sgb_ref[:, g:g + 1]
        for n in range(tm // CHUNK):
            rs = slice(n * CHUNK, (n + 1) * CHUNK)
            mixed = _dot(wg, vn[rs, :]) + bias
            sgu_ref[rs, gs] = (u[rs, :] * mixed).astype(BF16)


def _ctx_kv_kernel(x_ref, mod_ref, pn_ref, win_ref, kvn_ref, wukv_ref, cs_ref, k_ref, v_ref):
    mod = mod_ref[0]
    h = _rms(x_ref[...], pn_ref[...]) * (1.0 + mod[1:2]) + mod[0:1]
    _kv_path(h.astype(BF16), win_ref, kvn_ref, wukv_ref, cs_ref[:, :LANES], cs_ref[:, LANES:], k_ref, v_ref)


def _const_spec(shape):
    nd = len(shape)
    return pl.BlockSpec(shape, lambda i: (0,) * nd, pipeline_mode=pl.Buffered(1))


def _inproj(x2, mod3, pn, win, qn, wuq, kvn, wukv, sgn, sgw, sgb, cs, rows_per_batch, tm):
    n = x2.shape[0]
    tpb = rows_per_batch // tm
    row = lambda w: pl.BlockSpec((tm, w), lambda i: (i, 0))
    return pl.pallas_call(
        _inproj_kernel,
        grid=(n // tm,),
        in_specs=[row(D),
                  pl.BlockSpec((1, N_MOD, D), lambda i: (i // tpb, 0, 0)),
                  _const_spec((1, D)), _const_spec(win.shape), _const_spec((1, Q_LORA)),
                  _const_spec(wuq.shape), _const_spec((1, KV_LORA)), _const_spec(wukv.shape),
                  _const_spec((1, SGU_WIDTH)), _const_spec(sgw.shape), _const_spec(sgb.shape),
                  pl.BlockSpec((tm, 2 * LANES), lambda i: (i % tpb, 0))],
        out_specs=[row(HEADS * QK_PAD), row(HEADS * QK_PAD), row(MLA_WIDTH), row(SGU_WIDTH)],
        out_shape=[jax.ShapeDtypeStruct((n, HEADS * QK_PAD), BF16),
                   jax.ShapeDtypeStruct((n, HEADS * QK_PAD), BF16),
                   jax.ShapeDtypeStruct((n, MLA_WIDTH), BF16),
                   jax.ShapeDtypeStruct((n, SGU_WIDTH), BF16)],
        compiler_params=pltpu.CompilerParams(dimension_semantics=("arbitrary",), vmem_limit_bytes=VMEM_LIMIT),
        name="inproj",
    )(x2, mod3, pn, win, qn, wuq, kvn, wukv, sgn, sgw, sgb, cs)


def _ctx_kv(c2, mod3, ctx_row, pn, win, kvn, wukv, cs, tm):
    n = c2.shape[0]
    row = lambda w: pl.BlockSpec((tm, w), lambda i: (i, 0))
    return pl.pallas_call(
        _ctx_kv_kernel,
        grid=(n // tm,),
        in_specs=[row(D),
                  pl.BlockSpec((1, N_MOD, D), lambda i: (ctx_row, 0, 0)),
                  _const_spec((1, D)), _const_spec(win.shape), _const_spec((1, KV_LORA)),
                  _const_spec(wukv.shape),
                  pl.BlockSpec((tm, 2 * LANES), lambda i: (0, 0))],
        out_specs=[row(HEADS * QK_PAD), row(MLA_WIDTH)],
        out_shape=[jax.ShapeDtypeStruct((n, HEADS * QK_PAD), BF16),
                   jax.ShapeDtypeStruct((n, MLA_WIDTH), BF16)],
        compiler_params=pltpu.CompilerParams(dimension_semantics=("arbitrary",), vmem_limit_bytes=VMEM_LIMIT),
        name="ctx_kv",
    )(c2, mod3, pn, win, kvn, wukv, cs)


def _attn_kernel(q_ref, k_ref, v_ref, kc_ref, vc_ref, o_ref, *, tk):
    q = q_ref[0]
    tq = q.shape[0]
    nk = k_ref.shape[1] // tk

    def step(kb, vb, carry):
        m, l, acc = carry
        s = lax.dot_general(q, kb, (((1,), (1,)), ((), ())), preferred_element_type=F32)
        m_new = jnp.maximum(m, jnp.max(s, axis=-1, keepdims=True))
        alpha = jnp.exp(m - m_new)
        p = jnp.exp(s - m_new)
        l = alpha * l + jnp.sum(p, axis=-1, keepdims=True)
        acc = alpha * acc + _dot(p.astype(BF16), vb)
        return m_new, l, acc

    def body(j, carry):
        off = pl.multiple_of(j * tk, tk)
        return step(k_ref[0, pl.ds(off, tk), :], v_ref[0, pl.ds(off, tk), :], carry)

    init = (jnp.full((tq, 1), -jnp.inf, F32), jnp.zeros((tq, 1), F32), jnp.zeros((tq, V_HEAD), F32))
    carry = lax.fori_loop(0, nk, body, init)
    m, l, acc = step(kc_ref[0], vc_ref[0], carry)
    o_ref[0] = (acc / l).astype(BF16)


def _attention(q, k, v, kc, vc, tq, tk):
    b, l, _ = q.shape
    lc = kc.shape[1]
    return pl.pallas_call(
        functools.partial(_attn_kernel, tk=tk),
        grid=(b, HEADS, l // tq),
        in_specs=[pl.BlockSpec((1, tq, QK_PAD), lambda b_, h, i: (b_, i, h)),
                  pl.BlockSpec((1, l, QK_PAD), lambda b_, h, i: (b_, 0, h)),
                  pl.BlockSpec((1, l, V_HEAD), lambda b_, h, i: (b_, 0, h)),
                  pl.BlockSpec((1, lc, QK_PAD), lambda b_, h, i: (b_, 0, h)),
                  pl.BlockSpec((1, lc, V_HEAD), lambda b_, h, i: (b_, 0, h))],
        out_specs=pl.BlockSpec((1, tq, V_HEAD), lambda b_, h, i: (b_, i, h)),
        out_shape=jax.ShapeDtypeStruct((b, l, MLA_WIDTH), BF16),
        compiler_params=pltpu.CompilerParams(dimension_semantics=("arbitrary",) * 3),
        name="attn",
    )(q, k, v, kc, vc)


def _outproj_kernel(at_ref, sg_ref, x_ref, mod_ref, wo_ref, pn1_ref, pn2_ref, wr_ref, br_ref,
                    xn_ref, h2_ref, aff_ref):
    tm = x_ref.shape[0]
    mod = mod_ref[0]
    y = _dot(at_ref[...], wo_ref[:MLA_WIDTH, :]) + _dot(sg_ref[...], wo_ref[MLA_WIDTH:, :])
    xn = x_ref[...] + mod[2:3] * _rms(y, pn1_ref[...])
    xn_ref[...] = xn
    h2 = _rms(xn, pn2_ref[...]) * (1.0 + mod[4:5]) + mod[3:4]
    for j in range(ROW_TILES):
        h2_ref[pl.ds(j, tm, stride=ROW_TILES), :] = h2[:, j * LANES:(j + 1) * LANES]
    logits = _dot(h2.astype(BF16), wr_ref[...]) + br_ref[...]
    lane = lax.broadcasted_iota(I32, logits.shape, 1)
    logits = jnp.where(lane < N_EXPERTS, logits, -jnp.inf)
    e = jnp.exp(logits - jnp.max(logits, axis=-1, keepdims=True))
    aff_ref[...] = e / jnp.sum(e, axis=-1, keepdims=True)


def _outproj(attn2, sgu2, x2, mod3, wo, pn1, pn2, wr, br, rows_per_batch, tm):
    n = x2.shape[0]
    tpb = rows_per_batch // tm
    row = lambda w: pl.BlockSpec((tm, w), lambda i: (i, 0))
    return pl.pallas_call(
        _outproj_kernel,
        grid=(n // tm,),
        in_specs=[row(MLA_WIDTH), row(SGU_WIDTH), row(D),
                  pl.BlockSpec((1, N_MOD, D), lambda i: (i // tpb, 0, 0)),
                  _const_spec(wo.shape), _const_spec((1, D)), _const_spec((1, D)),
                  _const_spec(wr.shape), _const_spec((1, LANES))],
        out_specs=[row(D), pl.BlockSpec((tm * ROW_TILES, LANES), lambda i: (i, 0)), row(LANES)],
        out_shape=[jax.ShapeDtypeStruct((n, D), F32),
                   jax.ShapeDtypeStruct((n * ROW_TILES, LANES), F32),
                   jax.ShapeDtypeStruct((n, LANES), F32)],
        compiler_params=pltpu.CompilerParams(dimension_semantics=("arbitrary",), vmem_limit_bytes=VMEM_LIMIT),
        name="outproj",
    )(attn2, sgu2, x2, mod3, wo, pn1, pn2, wr, br)


def _prefix_count(x01):
    r, t = x01.shape
    nb = t // LANES
    stacked = jnp.concatenate([x01[:, k * LANES:(k + 1) * LANES] for k in range(nb)], axis=0).astype(BF16)
    ii = lax.broadcasted_iota(I32, (LANES, LANES), 0)
    jj = lax.broadcasted_iota(I32, (LANES, LANES), 1)
    tri = jnp.where(ii <= jj, 1.0, 0.0).astype(BF16)
    within = _dot(stacked, tri)
    off = jnp.zeros((r, 1), F32)
    blocks = []
    for k in range(nb):
        w = within[k * r:(k + 1) * r, :]
        blocks.append(w + off)
        off = off + w[:, LANES - 1:LANES]
    return jnp.concatenate(blocks, axis=1)


ROUTE_SLOTS = 128
ROUTE_SEARCH_STEPS = 192


def _route_kernel(aff_ref, idx_ref, gate_ref, incl_s, isel_s, aff_s, *, cap):
    aff = aff_ref[...]
    nr, lt = aff.shape
    capf = float(cap)

    def search(i, c):
        lo, hi = c
        mid = 0.5 * (lo + hi)
        ge = jnp.sum(jnp.where(aff >= mid, 1.0, 0.0), axis=1, keepdims=True) >= capf
        return jnp.where(ge, mid, lo), jnp.where(ge, hi, mid)

    lo, _ = lax.fori_loop(0, ROUTE_SEARCH_STEPS, search,
                          (jnp.zeros((nr, 1), F32), jnp.full((nr, 1), 2.0, F32)))
    thr = jnp.min(jnp.where(aff >= lo, aff, jnp.inf), axis=1, keepdims=True)
    gt = aff > thr
    eq = aff == thr
    n_gt = jnp.sum(jnp.where(gt, 1.0, 0.0), axis=1, keepdims=True)
    eq_rank = _prefix_count(jnp.where(eq, 1.0, 0.0))
    sel = jnp.where(gt, 1.0, jnp.where(eq, jnp.where(eq_rank <= capf - n_gt, 1.0, 0.0), 0.0))
    incl = _prefix_count(sel)
    isel = incl * sel
    for r in range(nr):
        incl_s[r] = incl[r:r + 1, :]
        isel_s[r] = isel[r:r + 1, :]
        aff_s[r] = aff[r:r + 1, :]

    ns = min(cap, ROUTE_SLOTS)

    def compact(r, _):
        for c0 in range(0, cap, ns):
            slot = (lax.broadcasted_iota(I32, (ns, LANES), 0) + c0).astype(F32)
            cnt = jnp.zeros((ns, LANES), F32)
            gat = jnp.zeros((ns, LANES), F32)
            for k in range(lt // LANES):
                ks = slice(k * LANES, (k + 1) * LANES)
                cnt = cnt + jnp.where(incl_s[r, :, ks] <= slot, 1.0, 0.0)
                gat = gat + jnp.where(isel_s[r, :, ks] == slot + 1.0, aff_s[r, :, ks], 0.0)
            idx_ref[r, c0:c0 + ns, :] = jnp.sum(cnt, axis=1, keepdims=True).astype(I32)
            gate_ref[r, c0:c0 + ns, :] = jnp.sum(gat, axis=1, keepdims=True)
        return 0

    lax.fori_loop(0, nr, compact, 0)


def _route(aff_rows, cap):
    nr, lt = aff_rows.shape
    return pl.pallas_call(
        functools.partial(_route_kernel, cap=cap),
        out_shape=[jax.ShapeDtypeStruct((nr, cap, 1), I32),
                   jax.ShapeDtypeStruct((nr, cap, 1), F32)],
        scratch_shapes=[pltpu.VMEM((nr, 1, lt), F32)] * 3,
        compiler_params=pltpu.CompilerParams(vmem_limit_bytes=VMEM_LIMIT),
        name="route",
    )(aff_rows)


def _ffn_kernel(idx_ref, h2_ref, gate_ref, wg_ref, wu_ref, wd_ref, o_ref, land, lhs, acc, wgb, wub, wdb, sem,
                *, cap, seq):
    e = pl.program_id(0)
    f = pl.program_id(1)
    b = pl.program_id(2)
    nf = pl.num_programs(1)
    pair = b * pl.num_programs(0) + e

    @pl.when(b == 0)
    def _cast_weights():
        wgb[...] = wg_ref[0].astype(BF16)
        wub[...] = wu_ref[0].astype(BF16)
        wdb[...] = wd_ref[0].astype(BF16)

    @pl.when(f == 0)
    def _gather():
        def issue(c, _):
            tok = b * seq + idx_ref[pair * cap + c]
            src = h2_ref.at[pl.ds(pl.multiple_of(tok * ROW_TILES, ROW_TILES), ROW_TILES)]
            dst = land.at[pl.ds(pl.multiple_of(c * ROW_TILES, ROW_TILES), ROW_TILES)]
            pltpu.make_async_copy(src, dst, sem).start()
            return 0
        lax.fori_loop(0, cap, issue, 0)
        pltpu.make_async_copy(h2_ref.at[pl.ds(0, cap * ROW_TILES)], land, sem).wait()
        for j in range(ROW_TILES):
            lhs[b, :, j * LANES:(j + 1) * LANES] = land[pl.ds(j, cap, stride=ROW_TILES), :].astype(BF16)

    x = lhs[b]
    a = _dot(x, wgb[...])
    g = _dot(x, wub[...])
    hm = (_silu(a) * g).astype(BF16)
    part = _dot(hm, wdb[...])

    @pl.when(f == 0)
    def _init():
        acc[b] = part

    @pl.when(f > 0)
    def _accum():
        acc[b] += part

    @pl.when(f == nf - 1)
    def _emit():
        gate = gate_ref[0]
        for j in range(ROW_TILES):
            o_ref[pl.ds(j, cap, stride=ROW_TILES), :] = acc[b, :, j * LANES:(j + 1) * LANES] * gate


def _ffn(idx_flat, h2_rows, gate_col, w_gate, w_up, w_down, nb, seq, cap, tf):
    ne, _, ff = w_gate.shape
    nf = ff // tf
    pair_block = lambda e, f, b, idx: (jnp.where(f == nf - 1, b, 0) * ne + e, 0)
    grid_spec = pltpu.PrefetchScalarGridSpec(
        num_scalar_prefetch=1,
        grid=(ne, nf, nb),
        in_specs=[pl.BlockSpec(memory_space=pl.ANY),
                  pl.BlockSpec((1, cap, 1), lambda e, f, b, idx: pair_block(e, f, b, idx) + (0,)),
                  pl.BlockSpec((1, D, tf), lambda e, f, b, idx: (e, 0, f)),
                  pl.BlockSpec((1, D, tf), lambda e, f, b, idx: (e, 0, f)),
                  pl.BlockSpec((1, tf, D), lambda e, f, b, idx: (e, f, 0))],
        out_specs=pl.BlockSpec((cap * ROW_TILES, LANES), pair_block),
        scratch_shapes=[pltpu.VMEM((cap * ROW_TILES, LANES), F32),
                        pltpu.VMEM((nb, cap, D), BF16),
                        pltpu.VMEM((nb, cap, D), F32),
                        pltpu.VMEM((D, tf), BF16),
                        pltpu.VMEM((D, tf), BF16),
                        pltpu.VMEM((tf, D), BF16),
                        pltpu.SemaphoreType.DMA(())],
    )
    return pl.pallas_call(
        functools.partial(_ffn_kernel, cap=cap, seq=seq),
        grid_spec=grid_spec,
        out_shape=jax.ShapeDtypeStruct((nb * ne * cap * ROW_TILES, LANES), F32),
        compiler_params=pltpu.CompilerParams(dimension_semantics=("arbitrary",) * 3, vmem_limit_bytes=VMEM_LIMIT),
        name="ffn",
    )(idx_flat, h2_rows, gate_col, w_gate, w_up, w_down)


COMBINE_UNROLL = 8


def _combine_kernel(idx_ref, ye_ref, y_ref, *, cap):
    b = pl.program_id(0)
    e = pl.program_id(2)
    pair = b * pl.num_programs(2) + e

    @pl.when(e == 0)
    def _zero():
        y_ref[...] = jnp.zeros(y_ref.shape, F32)

    def group(gi, _):
        base = gi * COMBINE_UNROLL
        toks = [idx_ref[pair * cap + base + u] for u in range(COMBINE_UNROLL)]
        rows = [y_ref[toks[u]] + ye_ref[base + u] for u in range(COMBINE_UNROLL)]
        for u in range(COMBINE_UNROLL):
            y_ref[toks[u]] = rows[u]
        return 0

    lax.fori_loop(0, cap // COMBINE_UNROLL, group, 0)


def _combine(idx_flat, ye, nb, seq, cap):
    ne = ye.shape[0] // (nb * cap * ROW_TILES)
    halves = ROW_TILES // SUBLANES
    ye5 = ye.reshape(nb * ne, cap, halves, SUBLANES, LANES)
    grid_spec = pltpu.PrefetchScalarGridSpec(
        num_scalar_prefetch=1,
        grid=(nb, halves, ne),
        in_specs=[pl.BlockSpec((None, cap, None, SUBLANES, LANES), lambda b, h, e, idx: (b * ne + e, 0, h, 0, 0))],
        out_specs=pl.BlockSpec((None, seq, None, SUBLANES, LANES), lambda b, h, e, idx: (b, 0, h, 0, 0)),
    )
    return pl.pallas_call(
        functools.partial(_combine_kernel, cap=cap),
        grid_spec=grid_spec,
        out_shape=jax.ShapeDtypeStruct((nb, seq, halves, SUBLANES, LANES), F32),
        compiler_params=pltpu.CompilerParams(dimension_semantics=("arbitrary",) * 3, vmem_limit_bytes=VMEM_LIMIT),
        name="combine",
    )(idx_flat, ye5)


def _final_kernel(y_ref, xn_ref, mod_ref, pn_ref, o_ref):
    tm = xn_ref.shape[0]
    g2 = mod_ref[0][5:6]
    ss = jnp.zeros((tm, LANES), F32)
    for j in range(ROW_TILES):
        v = y_ref[pl.ds(j, tm, stride=ROW_TILES), :]
        ss = ss + v * v
    rs = lax.rsqrt(jnp.sum(ss, axis=-1, keepdims=True) * (1.0 / D) + EPS)
    for j in range(ROW_TILES):
        cs = slice(j * LANES, (j + 1) * LANES)
        v = y_ref[pl.ds(j, tm, stride=ROW_TILES), :]
        o_ref[:, cs] = xn_ref[:, cs] + g2[:, cs] * (v * rs * pn_ref[:, cs])


def _final(y2_rows, xn, mod3, pn, rows_per_batch, tm):
    n = xn.shape[0]
    tpb = rows_per_batch // tm
    return pl.pallas_call(
        _final_kernel,
        grid=(n // tm,),
        in_specs=[pl.BlockSpec((tm * ROW_TILES, LANES), lambda i: (i, 0)),
                  pl.BlockSpec((tm, D), lambda i: (i, 0)),
                  pl.BlockSpec((1, N_MOD, D), lambda i: (i // tpb, 0, 0)),
                  _const_spec((1, D))],
        out_specs=pl.BlockSpec((tm, D), lambda i: (i, 0)),
        out_shape=jax.ShapeDtypeStruct((n, D), F32),
        compiler_params=pltpu.CompilerParams(dimension_semantics=("arbitrary",)),
        name="final",
    )(y2_rows, xn, mod3, pn)


def _rope_partner():
    q = QK_ROPE // 4
    return np.concatenate([np.arange(q, 2 * q), np.arange(0, q), np.arange(3 * q, 4 * q), np.arange(2 * q, 3 * q)])


def _rope_table(length):
    pos = np.arange(length)
    half = QK_ROPE // 2
    inv = (1.0 / (ROPE_THETA ** (np.arange(0, half, 2, dtype=np.float32) / half))).astype(np.float32)
    ar = (pos // GRID_W).astype(np.float32)[:, None] * inv
    ac = (pos % GRID_W).astype(np.float32)[:, None] * inv
    cos = np.concatenate([np.cos(ar), np.cos(ar), np.cos(ac), np.cos(ac)], axis=1)
    sin = np.concatenate([-np.sin(ar), np.sin(ar), -np.sin(ac), np.sin(ac)], axis=1)
    z = np.zeros((length, LANES - QK_ROPE), np.float32)
    return jnp.asarray(np.concatenate([cos, z, sin, z], axis=1).astype(np.float32))


def _identity_rope_table(length):
    t = np.zeros((length, 2 * LANES), np.float32)
    t[:, :QK_ROPE] = 1.0
    return jnp.asarray(t)


def _prep_w_in(w_in):
    perm = _rope_partner()
    kpe = w_in[:, 2 * Q_LORA:2 * Q_LORA + QK_ROPE]
    z = jnp.zeros((D, LANES - QK_ROPE), w_in.dtype)
    rest = w_in[:, 2 * Q_LORA + QK_ROPE:]
    return jnp.concatenate([w_in[:, :2 * Q_LORA], kpe, z, kpe[:, perm], z, rest], axis=1).astype(BF16)


def _prep_w_uq(w_uq):
    perm = _rope_partner()
    w = w_uq.reshape(Q_LORA, HEADS, QK_NOPE + QK_ROPE)
    z = jnp.zeros((Q_LORA, HEADS, LANES - QK_ROPE), w_uq.dtype)
    main = jnp.concatenate([w, z], axis=2).reshape(Q_LORA, HEADS * QK_PAD)
    partner = jnp.concatenate([w[:, :, QK_NOPE:][:, :, perm], z], axis=2).reshape(Q_LORA, HEADS * LANES)
    return jnp.concatenate([main, partner], axis=1).astype(BF16)


def _prep_w_ukv(w_ukv):
    w = w_ukv.reshape(KV_LORA, HEADS, 2, QK_NOPE)
    return w.transpose(0, 2, 1, 3).reshape(KV_LORA, 2 * HEADS * QK_NOPE).astype(BF16)


def kernel(x, c, ctx, c_ctx, w_ada, b_ada, pre_norm1, w_in, q_norm_w, w_uq, kv_norm_w, w_ukv, sgu_norm_w, sgu_w,
           sgu_b, w_out, post_norm1, pre_norm2, w_router, b_router, w_e_gate, w_e_up, w_e_down, post_norm2):
    nb, seq, _ = x.shape
    lc = ctx.shape[1]
    depth = w_ada.shape[0]
    assert depth == 1 and seq % 512 == 0 and lc % 128 == 0 and nb < SUBLANES
    cap = CAP_FACTOR * seq // N_EXPERTS
    n = nb * seq
    tm = 256

    cc = jnp.zeros((SUBLANES, D), F32).at[:nb].set(c).at[nb].set(c_ctx)
    mod3 = _ada(cc, w_ada[0], b_ada[0]).reshape(SUBLANES, N_MOD, D)

    row = lambda w: w.reshape(1, -1)
    win = _prep_w_in(w_in[0])
    wuq = _prep_w_uq(w_uq[0])
    wukv = _prep_w_ukv(w_ukv[0])
    x2 = x.reshape(n, D)
    q, k, v, sgu = _inproj(x2, mod3, row(pre_norm1[0]), win, row(q_norm_w[0]), wuq, row(kv_norm_w[0]), wukv,
                           row(sgu_norm_w[0]), sgu_w[0].astype(BF16), sgu_b[0].T, _rope_table(seq), seq, tm)
    kc, vc = _ctx_kv(ctx.reshape(nb * lc, D), mod3, nb, row(pre_norm1[0]), win, row(kv_norm_w[0]), wukv,
                     _identity_rope_table(lc), lc)

    attn = _attention(q.reshape(nb, seq, -1), k.reshape(nb, seq, -1), v.reshape(nb, seq, -1),
                      kc.reshape(nb, lc, -1), vc.reshape(nb, lc, -1), tq=512, tk=512)

    wr = jnp.zeros((D, LANES), BF16).at[:, :N_EXPERTS].set(w_router[0].astype(BF16))
    br = jnp.zeros((1, LANES), F32).at[0, :N_EXPERTS].set(b_router[0])
    xn, h2_rows, aff = _outproj(attn.reshape(n, MLA_WIDTH), sgu, x2, mod3, w_out[0].astype(BF16),
                                row(post_norm1[0]), row(pre_norm2[0]), wr, br, seq, tm)

    aff_t = aff[:, :N_EXPERTS].reshape(nb, seq, N_EXPERTS).transpose(0, 2, 1)
    idx_col, gate_col = _route(aff_t.reshape(nb * N_EXPERTS, seq), cap)
    idx_flat = idx_col.reshape(-1)

    ye = _ffn(idx_flat, h2_rows, gate_col, w_e_gate[0], w_e_up[0], w_e_down[0], nb, seq, cap, tf=256)
    y2 = _combine(idx_flat, ye, nb, seq, cap)
    out = _final(y2.reshape(n * ROW_TILES, LANES), xn, mod3, row(post_norm2[0]), seq, tm)
    return out.reshape(nb, seq, D)
```

```python
import functools

import numpy as np
import jax
import jax.numpy as jnp
from jax import lax
from jax.experimental import pallas as pl
from jax.experimental.pallas import tpu as pltpu

F32 = jnp.float32
BF16 = jnp.bfloat16
I32 = jnp.int32

D = 2048
GRID_W = 64
EPS = 1e-6
N_MOD = 6
HEADS = 8
Q_LORA = 512
KV_LORA = 512
QK_NOPE = 128
QK_ROPE = 64
V_HEAD = 128
ROPE_THETA = 10000.0
ATTN_SCALE = (QK_NOPE + QK_ROPE) ** -0.5
Q_SCALE = ATTN_SCALE * float(np.log2(np.e))
SGU_HEADS = 8
SGU_DIM = 128
CHUNK = 128
MLA_WIDTH = HEADS * V_HEAD
SGU_WIDTH = SGU_HEADS * SGU_DIM
N_EXPERTS = 16
CAP_FACTOR = 2
EXPERT_FF = D

LANES = 128
SUBLANES = 8
ROW_TILES = D // LANES
QK_PAD = 256
VMEM_LIMIT = 56 * 1024 * 1024

C_Q = 0
C_KV = C_Q + Q_LORA
C_KPE = C_KV + KV_LORA
C_KPP = C_KPE + LANES
C_U = C_KPP + LANES
C_V = C_U + SGU_WIDTH
IN_EXT = C_V + SGU_WIDTH


def _rms(x, w):
    return x * lax.rsqrt(jnp.mean(x * x, axis=-1, keepdims=True) + EPS) * w


def _gelu_tanh(x):
    return 0.5 * x * (1.0 + jnp.tanh(np.sqrt(2.0 / np.pi).astype(np.float32) * (x + 0.044715 * (x * x * x))))


def _silu(x):
    return x * (1.0 / (1.0 + jnp.exp(-x)))


def _dot(a, b):
    return jnp.dot(a, b, preferred_element_type=F32)


def _ada_kernel(c_ref, w_ref, b_ref, o_ref):
    s = _silu(c_ref[...]).astype(BF16)
    o_ref[...] = _dot(s, w_ref[...].astype(BF16)) + b_ref[...]


def _ada(cc, w_ada, b_ada):
    n = w_ada.shape[1]
    tn = 1024
    return pl.pallas_call(
        _ada_kernel,
        grid=(n // tn,),
        in_specs=[pl.BlockSpec((SUBLANES, D), lambda j: (0, 0)),
                  pl.BlockSpec((D, tn), lambda j: (0, j)),
                  pl.BlockSpec((1, tn), lambda j: (0, j))],
        out_specs=pl.BlockSpec((SUBLANES, tn), lambda j: (0, j)),
        out_shape=jax.ShapeDtypeStruct((SUBLANES, n), F32),
        compiler_params=pltpu.CompilerParams(dimension_semantics=("arbitrary",), vmem_limit_bytes=VMEM_LIMIT),
        name="ada",
    )(cc, w_ada, b_ada.reshape(1, n))


def _kv_path(hb, win_ref, kvn_ref, wukv_ref, cos, sin, k_ref, v_ref):
    ckv = _dot(hb, win_ref[:, C_KV:C_KV + KV_LORA])
    ckvn = _rms(ckv, kvn_ref[...]).astype(BF16)
    kn = _dot(ckvn, wukv_ref[:, :HEADS * QK_NOPE])
    v_ref[...] = _dot(ckvn, wukv_ref[:, HEADS * QK_NOPE:]).astype(BF16)
    kp = _dot(hb, win_ref[:, C_KPE:C_KPE + 2 * LANES])
    kpe = (kp[:, :LANES] * cos + kp[:, LANES:] * sin).astype(BF16)
    for h in range(HEADS):
        k_ref[:, h * QK_PAD:h * QK_PAD + QK_NOPE] = kn[:, h * QK_NOPE:(h + 1) * QK_NOPE].astype(BF16)
        k_ref[:, h * QK_PAD + QK_NOPE:(h + 1) * QK_PAD] = kpe


def _inproj_kernel(x_ref, mod_ref, pn_ref, win_ref, qn_ref, wuq_ref, kvn_ref, wukv_ref,
                   sgn_ref, sgw_ref, sgb_ref, cs_ref, q_ref, k_ref, v_ref, sgu_ref):
    tm = x_ref.shape[0]
    mod = mod_ref[0]
    h = _rms(x_ref[...], pn_ref[...]) * (1.0 + mod[1:2]) + mod[0:1]
    hb = h.astype(BF16)
    cos = cs_ref[:, :LANES]
    sin = cs_ref[:, LANES:]

    _kv_path(hb, win_ref, kvn_ref, wukv_ref, cos, sin, k_ref, v_ref)

    cq = _dot(hb, win_ref[:, C_Q:C_Q + Q_LORA])
    cqn = _rms(cq, qn_ref[...]).astype(BF16)
    for hp in range(HEADS // 2):
        qm2 = _dot(cqn, wuq_ref[:, hp * 2 * QK_PAD:(hp + 1) * 2 * QK_PAD])
        qp2 = _dot(cqn, wuq_ref[:, HEADS * QK_PAD + hp * 2 * LANES:HEADS * QK_PAD + (hp + 1) * 2 * LANES])
        for i in range(2):
            h_ = 2 * hp + i
            qm = qm2[:, i * QK_PAD:(i + 1) * QK_PAD]
            qp = qp2[:, i * LANES:(i + 1) * LANES]
            q_ref[:, h_ * QK_PAD:h_ * QK_PAD + QK_NOPE] = (qm[:, :QK_NOPE] * Q_SCALE).astype(BF16)
            q_ref[:, h_ * QK_PAD + QK_NOPE:(h_ + 1) * QK_PAD] = (
                (qm[:, QK_NOPE:] * cos + qp * sin) * Q_SCALE).astype(BF16)

    for gp in range(SGU_HEADS // 2):
        u2 = _gelu_tanh(_dot(hb, win_ref[:, C_U + gp * 2 * SGU_DIM:C_U + (gp + 1) * 2 * SGU_DIM]))
        vs2 = _gelu_tanh(_dot(hb, win_ref[:, C_V + gp * 2 * SGU_DIM:C_V + (gp + 1) * 2 * SGU_DIM]))
        for i in range(2):
            g = 2 * gp + i
            gs = slice(g * SGU_DIM, (g + 1) * SGU_DIM)
            u = u2[:, i * SGU_DIM:(i + 1) * SGU_DIM]
            vs = vs2[:, i * SGU_DIM:(i + 1) * SGU_DIM]
            mu = jnp.mean(vs, axis=-1, keepdims=True)
            vc = vs - mu
            var = jnp.mean(vc * vc, axis=-1, keepdims=True)
            vn = (vc * lax.rsqrt(var + EPS) * sgn_ref[:, gs]).astype(BF16)
            wg = sgw_ref[g]
            bias = sgb_ref[:, g:g + 1]
            for n in range(tm // CHUNK):
                rs = slice(n * CHUNK, (n + 1) * CHUNK)
                mixed = _dot(wg, vn[rs, :]) + bias
                sgu_ref[rs, gs] = (u[rs, :] * mixed).astype(BF16)


def _ctx_kv_kernel(x_ref, mod_ref, pn_ref, win_ref, kvn_ref, wukv_ref, cs_ref, k_ref, v_ref):
    mod = mod_ref[0]
    h = _rms(x_ref[...], pn_ref[...]) * (1.0 + mod[1:2]) + mod[0:1]
    _kv_path(h.astype(BF16), win_ref, kvn_ref, wukv_ref, cs_ref[:, :LANES], cs_ref[:, LANES:], k_ref, v_ref)


def _const_spec(shape):
    nd = len(shape)
    return pl.BlockSpec(shape, lambda i: (0,) * nd, pipeline_mode=pl.Buffered(1))


def _inproj(x2, mod3, pn, win, qn, wuq, kvn, wukv, sgn, sgw, sgb, cs, rows_per_batch, tm):
    n = x2.shape[0]
    tpb = rows_per_batch // tm
    row = lambda w: pl.BlockSpec((tm, w), lambda i: (i, 0))
    return pl.pallas_call(
        _inproj_kernel,
        grid=(n // tm,),
        in_specs=[row(D),
                  pl.BlockSpec((1, N_MOD, D), lambda i: (i // tpb, 0, 0)),
                  _const_spec((1, D)), _const_spec(win.shape), _const_spec((1, Q_LORA)),
                  _const_spec(wuq.shape), _const_spec((1, KV_LORA)), _const_spec(wukv.shape),
                  _const_spec((1, SGU_WIDTH)), _const_spec(sgw.shape), _const_spec(sgb.shape),
                  pl.BlockSpec((tm, 2 * LANES), lambda i: (i % tpb, 0))],
        out_specs=[row(HEADS * QK_PAD), row(HEADS * QK_PAD), row(MLA_WIDTH), row(SGU_WIDTH)],
        out_shape=[jax.ShapeDtypeStruct((n, HEADS * QK_PAD), BF16),
                   jax.ShapeDtypeStruct((n, HEADS * QK_PAD), BF16),
                   jax.ShapeDtypeStruct((n, MLA_WIDTH), BF16),
                   jax.ShapeDtypeStruct((n, SGU_WIDTH), BF16)],
        compiler_params=pltpu.CompilerParams(dimension_semantics=("arbitrary",), vmem_limit_bytes=VMEM_LIMIT),
        name="inproj",
    )(x2, mod3, pn, win, qn, wuq, kvn, wukv, sgn, sgw, sgb, cs)


def _ctx_kv(c2, mod3, ctx_row, pn, win, kvn, wukv, cs, tm):
    n = c2.shape[0]
    row = lambda w: pl.BlockSpec((tm, w), lambda i: (i, 0))
    return pl.pallas_call(
        _ctx_kv_kernel,
        grid=(n // tm,),
        in_specs=[row(D),
                  pl.BlockSpec((1, N_MOD, D), lambda i: (ctx_row, 0, 0)),
                  _const_spec((1, D)), _const_spec(win.shape), _const_spec((1, KV_LORA)),
                  _const_spec(wukv.shape),
                  pl.BlockSpec((tm, 2 * LANES), lambda i: (0, 0))],
        out_specs=[row(HEADS * QK_PAD), row(MLA_WIDTH)],
        out_shape=[jax.ShapeDtypeStruct((n, HEADS * QK_PAD), BF16),
                   jax.ShapeDtypeStruct((n, MLA_WIDTH), BF16)],
        compiler_params=pltpu.CompilerParams(dimension_semantics=("arbitrary",), vmem_limit_bytes=VMEM_LIMIT),
        name="ctx_kv",
    )(c2, mod3, pn, win, kvn, wukv, cs)


def _attn_kernel(q_ref, k_ref, v_ref, kc_ref, vc_ref, o_ref, *, tk, streams):
    tq = q_ref.shape[1]
    ts = tq // streams
    nk = k_ref.shape[1] // tk
    qs = [q_ref[0, s * ts:(s + 1) * ts, :] for s in range(streams)]

    def step(q, kb, vb, carry):
        m, l, acc = carry
        s = lax.dot_general(q, kb, (((1,), (1,)), ((), ())), preferred_element_type=F32)
        m_new = jnp.maximum(m, jnp.max(s, axis=-1, keepdims=True))
        alpha = jnp.exp2(m - m_new)
        p = jnp.exp2(s - m_new)
        l = alpha * l + jnp.sum(p, axis=-1, keepdims=True)
        acc = alpha * acc + _dot(p.astype(BF16), vb)
        return m_new, l, acc

    def body(j, carries):
        off = pl.multiple_of(j * tk, tk)
        kb = k_ref[0, pl.ds(off, tk), :]
        vb = v_ref[0, pl.ds(off, tk), :]
        return tuple(step(qs[s], kb, vb, carries[s]) for s in range(streams))

    init = (jnp.full((ts, 1), -jnp.inf, F32), jnp.zeros((ts, 1), F32), jnp.zeros((ts, V_HEAD), F32))
    carries = (init,) * streams
    for j in range(nk):
        carries = body(j, carries)
    for s in range(streams):
        m, l, acc = step(qs[s], kc_ref[0], vc_ref[0], carries[s])
        o_ref[0, s * ts:(s + 1) * ts, :] = (acc / l).astype(BF16)


def _attention(q, k, v, kc, vc, tq, tk):
    b, l, _ = q.shape
    lc = kc.shape[1]
    tk = min(tk, l)
    assert l % tk == 0 and l % tq == 0
    return pl.pallas_call(
        functools.partial(_attn_kernel, tk=tk, streams=1),
        grid=(b, HEADS, l // tq),
        in_specs=[pl.BlockSpec((1, tq, QK_PAD), lambda b_, h, i: (b_, i, h)),
                  pl.BlockSpec((1, l, QK_PAD), lambda b_, h, i: (b_, 0, h)),
                  pl.BlockSpec((1, l, V_HEAD), lambda b_, h, i: (b_, 0, h)),
                  pl.BlockSpec((1, lc, QK_PAD), lambda b_, h, i: (b_, 0, h)),
                  pl.BlockSpec((1, lc, V_HEAD), lambda b_, h, i: (b_, 0, h))],
        out_specs=pl.BlockSpec((1, tq, V_HEAD), lambda b_, h, i: (b_, i, h)),
        out_shape=jax.ShapeDtypeStruct((b, l, MLA_WIDTH), BF16),
        compiler_params=pltpu.CompilerParams(dimension_semantics=("arbitrary",) * 3, vmem_limit_bytes=VMEM_LIMIT),
        name="attn",
    )(q, k, v, kc, vc)


def _outproj_kernel(at_ref, sg_ref, x_ref, mod_ref, wo_ref, pn1_ref, pn2_ref, wr_ref, br_ref,
                    xn_ref, h2_ref, aff_ref):
    tm = x_ref.shape[0]
    mod = mod_ref[0]
    y = _dot(at_ref[...], wo_ref[:MLA_WIDTH, :]) + _dot(sg_ref[...], wo_ref[MLA_WIDTH:, :])
    xn = x_ref[...] + mod[2:3] * _rms(y, pn1_ref[...])
    xn_ref[...] = xn
    h2 = _rms(xn, pn2_ref[...]) * (1.0 + mod[4:5]) + mod[3:4]
    h2_ref[...] = h2
    logits = _dot(h2.astype(BF16), wr_ref[...]) + br_ref[...]
    lane = lax.broadcasted_iota(I32, logits.shape, 1)
    logits = jnp.where(lane < N_EXPERTS, logits, -jnp.inf)
    e = jnp.exp(logits - jnp.max(logits, axis=-1, keepdims=True))
    aff_ref[...] = e / jnp.sum(e, axis=-1, keepdims=True)


def _outproj(attn2, sgu2, x2, mod3, wo, pn1, pn2, wr, br, rows_per_batch, tm):
    n = x2.shape[0]
    tpb = rows_per_batch // tm
    row = lambda w: pl.BlockSpec((tm, w), lambda i: (i, 0))
    return pl.pallas_call(
        _outproj_kernel,
        grid=(n // tm,),
        in_specs=[row(MLA_WIDTH), row(SGU_WIDTH), row(D),
                  pl.BlockSpec((1, N_MOD, D), lambda i: (i // tpb, 0, 0)),
                  _const_spec(wo.shape), _const_spec((1, D)), _const_spec((1, D)),
                  _const_spec(wr.shape), _const_spec((1, LANES))],
        out_specs=[row(D), row(D), row(LANES)],
        out_shape=[jax.ShapeDtypeStruct((n, D), F32),
                   jax.ShapeDtypeStruct((n, D), F32),
                   jax.ShapeDtypeStruct((n, LANES), F32)],
        compiler_params=pltpu.CompilerParams(dimension_semantics=("arbitrary",), vmem_limit_bytes=VMEM_LIMIT),
        name="outproj",
    )(attn2, sgu2, x2, mod3, wo, pn1, pn2, wr, br)


def _prefix_count(x01):
    r, t = x01.shape
    nb = t // LANES
    stacked = jnp.concatenate([x01[:, k * LANES:(k + 1) * LANES] for k in range(nb)], axis=0).astype(BF16)
    ii = lax.broadcasted_iota(I32, (LANES, LANES), 0)
    jj = lax.broadcasted_iota(I32, (LANES, LANES), 1)
    tri = jnp.where(ii <= jj, 1.0, 0.0).astype(BF16)
    within = _dot(stacked, tri)
    off = jnp.zeros((r, 1), F32)
    blocks = []
    for k in range(nb):
        w = within[k * r:(k + 1) * r, :]
        blocks.append(w + off)
        off = off + w[:, LANES - 1:LANES]
    return jnp.concatenate(blocks, axis=1)


ROUTE_SLOTS = 128
ROUTE_SEARCH_STEPS = 192


def _route_kernel(aff_ref, idx_ref, gate_ref, incl_s, isel_s, aff_s, *, cap):
    aff = aff_ref[...]
    nr, lt = aff.shape
    capf = float(cap)

    def search(i, c):
        lo, hi = c
        mid = 0.5 * (lo + hi)
        ge = jnp.sum(jnp.where(aff >= mid, 1.0, 0.0), axis=1, keepdims=True) >= capf
        return jnp.where(ge, mid, lo), jnp.where(ge, hi, mid)

    lo, _ = lax.fori_loop(0, ROUTE_SEARCH_STEPS, search,
                          (jnp.zeros((nr, 1), F32), jnp.full((nr, 1), 2.0, F32)))
    thr = jnp.min(jnp.where(aff >= lo, aff, jnp.inf), axis=1, keepdims=True)
    gt = aff > thr
    eq = aff == thr
    n_gt = jnp.sum(jnp.where(gt, 1.0, 0.0), axis=1, keepdims=True)
    eq_rank = _prefix_count(jnp.where(eq, 1.0, 0.0))
    sel = jnp.where(gt, 1.0, jnp.where(eq, jnp.where(eq_rank <= capf - n_gt, 1.0, 0.0), 0.0))
    incl = _prefix_count(sel)
    isel = incl * sel
    for r in range(nr):
        incl_s[r] = incl[r:r + 1, :]
        isel_s[r] = isel[r:r + 1, :]
        aff_s[r] = aff[r:r + 1, :]

    ns = min(cap, ROUTE_SLOTS)

    def compact(r, _):
        for c0 in range(0, cap, ns):
            slot = (lax.broadcasted_iota(I32, (ns, LANES), 0) + c0).astype(F32)
            cnt = jnp.zeros((ns, LANES), F32)
            gat = jnp.zeros((ns, LANES), F32)
            for k in range(lt // LANES):
                ks = slice(k * LANES, (k + 1) * LANES)
                cnt = cnt + jnp.where(incl_s[r, :, ks] <= slot, 1.0, 0.0)
                gat = gat + jnp.where(isel_s[r, :, ks] == slot + 1.0, aff_s[r, :, ks], 0.0)
            idx_ref[r, c0:c0 + ns, :] = jnp.sum(cnt, axis=1, keepdims=True).astype(I32)
            gate_ref[r, c0:c0 + ns, :] = jnp.sum(gat, axis=1, keepdims=True)
        return 0

    lax.fori_loop(0, nr, compact, 0)


def _route(aff_rows, cap):
    nr, lt = aff_rows.shape
    return pl.pallas_call(
        functools.partial(_route_kernel, cap=cap),
        out_shape=[jax.ShapeDtypeStruct((nr, cap, 1), I32),
                   jax.ShapeDtypeStruct((nr, cap, 1), F32)],
        scratch_shapes=[pltpu.VMEM((nr, 1, lt), F32)] * 3,
        compiler_params=pltpu.CompilerParams(vmem_limit_bytes=VMEM_LIMIT),
        name="route",
    )(aff_rows)


GATHER_UNROLL = 8


def _ffn_kernel(idx_ref, h2_ref, gate_ref, wg_ref, wu_ref, wd_ref, o_ref, land, lhs, acc, wgb, wub, wdb, sem,
                *, cap, seq):
    e = pl.program_id(0)
    f = pl.program_id(1)
    b = pl.program_id(2)
    nf = pl.num_programs(1)
    pair = b * pl.num_programs(0) + e

    @pl.when(b == 0)
    def _cast_weights():
        wgb[...] = wg_ref[0].astype(BF16)
        wub[...] = wu_ref[0].astype(BF16)
        wdb[...] = wd_ref[0].astype(BF16)

    @pl.when(f == 0)
    def _gather():
        def issue(c, _):
            tok = b * seq + idx_ref[pair * cap + c]
            pltpu.make_async_copy(h2_ref.at[pl.ds(tok, 1)], land.at[pl.ds(c, 1)], sem).start()
            return 0
        lax.fori_loop(0, cap, issue, 0, unroll=GATHER_UNROLL)
        pltpu.make_async_copy(h2_ref.at[pl.ds(0, cap)], land, sem).wait()
        lhs[b] = land[...].astype(BF16)
        acc[b] = jnp.zeros((cap, D), F32)

    x = lhs[b]
    a = _dot(x, wgb[...])
    g = _dot(x, wub[...])
    hm = (_silu(a) * g).astype(BF16)
    acc[b] += _dot(hm, wdb[...])

    @pl.when(f == nf - 1)
    def _emit():
        gate = gate_ref[0]
        for j in range(ROW_TILES):
            o_ref[pl.ds(j, cap, stride=ROW_TILES), :] = acc[b, :, j * LANES:(j + 1) * LANES] * gate


def _ffn(idx_flat, h2_rows, gate_col, w_gate, w_up, w_down, nb, seq, cap, tf):
    ne, _, ff = w_gate.shape
    nf = ff // tf
    pair_block = lambda e, f, b, idx: (jnp.where(f == nf - 1, b, 0) * ne + e, 0)
    grid_spec = pltpu.PrefetchScalarGridSpec(
        num_scalar_prefetch=1,
        grid=(ne, nf, nb),
        in_specs=[pl.BlockSpec(memory_space=pl.ANY),
                  pl.BlockSpec((1, cap, 1), lambda e, f, b, idx: pair_block(e, f, b, idx) + (0,)),
                  pl.BlockSpec((1, D, tf), lambda e, f, b, idx: (e, 0, f)),
                  pl.BlockSpec((1, D, tf), lambda e, f, b, idx: (e, 0, f)),
                  pl.BlockSpec((1, tf, D), lambda e, f, b, idx: (e, f, 0))],
        out_specs=pl.BlockSpec((cap * ROW_TILES, LANES), pair_block),
        scratch_shapes=[pltpu.VMEM((cap, D), F32),
                        pltpu.VMEM((nb, cap, D), BF16),
                        pltpu.VMEM((nb, cap, D), F32),
                        pltpu.VMEM((D, tf), BF16),
                        pltpu.VMEM((D, tf), BF16),
                        pltpu.VMEM((tf, D), BF16),
                        pltpu.SemaphoreType.DMA(())],
    )
    return pl.pallas_call(
        functools.partial(_ffn_kernel, cap=cap, seq=seq),
        grid_spec=grid_spec,
        out_shape=jax.ShapeDtypeStruct((nb * ne * cap * ROW_TILES, LANES), F32),
        compiler_params=pltpu.CompilerParams(dimension_semantics=("arbitrary",) * 3, vmem_limit_bytes=VMEM_LIMIT),
        name="ffn",
    )(idx_flat, h2_rows, gate_col, w_gate, w_up, w_down)


COMBINE_UNROLL = 8


def _combine_kernel(idx_ref, ye_ref, y_ref, *, cap):
    b = pl.program_id(0)
    e = pl.program_id(2)
    pair = b * pl.num_programs(2) + e

    @pl.when(e == 0)
    def _zero():
        y_ref[...] = jnp.zeros(y_ref.shape, F32)

    def group(gi, _):
        base = gi * COMBINE_UNROLL
        toks = [idx_ref[pair * cap + base + u] for u in range(COMBINE_UNROLL)]
        rows = [y_ref[toks[u]] + ye_ref[base + u] for u in range(COMBINE_UNROLL)]
        for u in range(COMBINE_UNROLL):
            y_ref[toks[u]] = rows[u]
        return 0

    lax.fori_loop(0, cap // COMBINE_UNROLL, group, 0)


def _combine(idx_flat, ye, nb, seq, cap):
    ne = ye.shape[0] // (nb * cap * ROW_TILES)
    halves = ROW_TILES // SUBLANES
    ye5 = ye.reshape(nb * ne, cap, halves, SUBLANES, LANES)
    grid_spec = pltpu.PrefetchScalarGridSpec(
        num_scalar_prefetch=1,
        grid=(nb, halves, ne),
        in_specs=[pl.BlockSpec((None, cap, None, SUBLANES, LANES), lambda b, h, e, idx: (b * ne + e, 0, h, 0, 0))],
        out_specs=pl.BlockSpec((None, seq, None, SUBLANES, LANES), lambda b, h, e, idx: (b, 0, h, 0, 0)),
    )
    return pl.pallas_call(
        functools.partial(_combine_kernel, cap=cap),
        grid_spec=grid_spec,
        out_shape=jax.ShapeDtypeStruct((nb, seq, halves, SUBLANES, LANES), F32),
        compiler_params=pltpu.CompilerParams(dimension_semantics=("arbitrary",) * 3, vmem_limit_bytes=VMEM_LIMIT),
        name="combine",
    )(idx_flat, ye5)


def _final_kernel(y_ref, xn_ref, mod_ref, pn_ref, o_ref):
    tm = xn_ref.shape[0]
    g2 = mod_ref[0][5:6]
    ss = jnp.zeros((tm, LANES), F32)
    for j in range(ROW_TILES):
        v = y_ref[pl.ds(j, tm, stride=ROW_TILES), :]
        ss = ss + v * v
    rs = lax.rsqrt(jnp.sum(ss, axis=-1, keepdims=True) * (1.0 / D) + EPS)
    for j in range(ROW_TILES):
        cs = slice(j * LANES, (j + 1) * LANES)
        v = y_ref[pl.ds(j, tm, stride=ROW_TILES), :]
        o_ref[:, cs] = xn_ref[:, cs] + g2[:, cs] * (v * rs * pn_ref[:, cs])


def _final(y2_rows, xn, mod3, pn, rows_per_batch, tm):
    n = xn.shape[0]
    tpb = rows_per_batch // tm
    return pl.pallas_call(
        _final_kernel,
        grid=(n // tm,),
        in_specs=[pl.BlockSpec((tm * ROW_TILES, LANES), lambda i: (i, 0)),
                  pl.BlockSpec((tm, D), lambda i: (i, 0)),
                  pl.BlockSpec((1, N_MOD, D), lambda i: (i // tpb, 0, 0)),
                  _const_spec((1, D))],
        out_specs=pl.BlockSpec((tm, D), lambda i: (i, 0)),
        out_shape=jax.ShapeDtypeStruct((n, D), F32),
        compiler_params=pltpu.CompilerParams(dimension_semantics=("arbitrary",)),
        name="final",
    )(y2_rows, xn, mod3, pn)


def _rope_partner():
    q = QK_ROPE // 4
    return np.concatenate([np.arange(q, 2 * q), np.arange(0, q), np.arange(3 * q, 4 * q), np.arange(2 * q, 3 * q)])


def _rope_table(length):
    pos = np.arange(length)
    half = QK_ROPE // 2
    inv = (1.0 / (ROPE_THETA ** (np.arange(0, half, 2, dtype=np.float32) / half))).astype(np.float32)
    ar = (pos // GRID_W).astype(np.float32)[:, None] * inv
    ac = (pos % GRID_W).astype(np.float32)[:, None] * inv
    cos = np.concatenate([np.cos(ar), np.cos(ar), np.cos(ac), np.cos(ac)], axis=1)
    sin = np.concatenate([-np.sin(ar), np.sin(ar), -np.sin(ac), np.sin(ac)], axis=1)
    z = np.zeros((length, LANES - QK_ROPE), np.float32)
    return jnp.asarray(np.concatenate([cos, z, sin, z], axis=1).astype(np.float32))


def _identity_rope_table(length):
    t = np.zeros((length, 2 * LANES), np.float32)
    t[:, :QK_ROPE] = 1.0
    return jnp.asarray(t)


def _prep_w_in(w_in):
    perm = _rope_partner()
    kpe = w_in[:, 2 * Q_LORA:2 * Q_LORA + QK_ROPE]
    z = jnp.zeros((D, LANES - QK_ROPE), w_in.dtype)
    rest = w_in[:, 2 * Q_LORA + QK_ROPE:]
    return jnp.concatenate([w_in[:, :2 * Q_LORA], kpe, z, kpe[:, perm], z, rest], axis=1).astype(BF16)


def _prep_w_uq(w_uq):
    perm = _rope_partner()
    w = w_uq.reshape(Q_LORA, HEADS, QK_NOPE + QK_ROPE)
    z = jnp.zeros((Q_LORA, HEADS, LANES - QK_ROPE), w_uq.dtype)
    main = jnp.concatenate([w, z], axis=2).reshape(Q_LORA, HEADS * QK_PAD)
    partner = jnp.concatenate([w[:, :, QK_NOPE:][:, :, perm], z], axis=2).reshape(Q_LORA, HEADS * LANES)
    return jnp.concatenate([main, partner], axis=1).astype(BF16)


def _prep_w_ukv(w_ukv):
    w = w_ukv.reshape(KV_LORA, HEADS, 2, QK_NOPE)
    return w.transpose(0, 2, 1, 3).reshape(KV_LORA, 2 * HEADS * QK_NOPE).astype(BF16)


def kernel(x, c, ctx, c_ctx, w_ada, b_ada, pre_norm1, w_in, q_norm_w, w_uq, kv_norm_w, w_ukv, sgu_norm_w, sgu_w,
           sgu_b, w_out, post_norm1, pre_norm2, w_router, b_router, w_e_gate, w_e_up, w_e_down, post_norm2):
    nb, seq, _ = x.shape
    lc = ctx.shape[1]
    depth = w_ada.shape[0]
    assert depth == 1 and seq % 512 == 0 and lc % 128 == 0 and nb < SUBLANES
    cap = CAP_FACTOR * seq // N_EXPERTS
    n = nb * seq
    tm = 256

    cc = jnp.zeros((SUBLANES, D), F32).at[:nb].set(c).at[nb].set(c_ctx)
    mod3 = _ada(cc, w_ada[0], b_ada[0]).reshape(SUBLANES, N_MOD, D)

    row = lambda w: w.reshape(1, -1)
    win = _prep_w_in(w_in[0])
    wuq = _prep_w_uq(w_uq[0])
    wukv = _prep_w_ukv(w_ukv[0])
    x2 = x.reshape(n, D)
    q, k, v, sgu = _inproj(x2, mod3, row(pre_norm1[0]), win, row(q_norm_w[0]), wuq, row(kv_norm_w[0]), wukv,
                           row(sgu_norm_w[0]), sgu_w[0].astype(BF16), sgu_b[0].T, _rope_table(seq), seq, 2 * tm)
    kc, vc = _ctx_kv(ctx.reshape(nb * lc, D), mod3, nb, row(pre_norm1[0]), win, row(kv_norm_w[0]), wukv,
                     _identity_rope_table(lc), lc)

    attn = _attention(q.reshape(nb, seq, -1), k.reshape(nb, seq, -1), v.reshape(nb, seq, -1),
                      kc.reshape(nb, lc, -1), vc.reshape(nb, lc, -1), tq=512, tk=4096)

    wr = jnp.zeros((D, LANES), BF16).at[:, :N_EXPERTS].set(w_router[0].astype(BF16))
    br = jnp.zeros((1, LANES), F32).at[0, :N_EXPERTS].set(b_router[0])
    xn, h2_rows, aff = _outproj(attn.reshape(n, MLA_WIDTH), sgu, x2, mod3, w_out[0].astype(BF16),
                                row(post_norm1[0]), row(pre_norm2[0]), wr, br, seq, tm)

    aff_t = aff[:, :N_EXPERTS].reshape(nb, seq, N_EXPERTS).transpose(0, 2, 1)
    idx_col, gate_col = _route(aff_t.reshape(nb * N_EXPERTS, seq), cap)
    idx_flat = idx_col.reshape(-1)

    ye = _ffn(idx_flat, h2_rows, gate_col, w_e_gate[0], w_e_up[0], w_e_down[0], nb, seq, cap, tf=256)
    y2 = _combine(idx_flat, ye, nb, seq, cap)
    out = _final(y2.reshape(n * ROW_TILES, LANES), xn, mod3, row(post_norm2[0]), seq, tm)
    return out.reshape(nb, seq, D)
```

```python
import functools

import numpy as np
import jax
import jax.numpy as jnp
from jax import lax
from jax.experimental import pallas as pl
from jax.experimental.pallas import tpu as pltpu

F32 = jnp.float32
BF16 = jnp.bfloat16
I32 = jnp.int32

D = 2048
GRID_W = 64
EPS = 1e-6
N_MOD = 6
HEADS = 8
Q_LORA = 512
KV_LORA = 512
QK_NOPE = 128
QK_ROPE = 64
V_HEAD = 128
ROPE_THETA = 10000.0
ATTN_SCALE = (QK_NOPE + QK_ROPE) ** -0.5
Q_SCALE = ATTN_SCALE * float(np.log2(np.e))
SGU_HEADS = 8
SGU_DIM = 128
CHUNK = 128
MLA_WIDTH = HEADS * V_HEAD
SGU_WIDTH = SGU_HEADS * SGU_DIM
N_EXPERTS = 16
CAP_FACTOR = 2
EXPERT_FF = D

LANES = 128
SUBLANES = 8
ROW_TILES = D // LANES
QK_PAD = 256
VMEM_LIMIT = 56 * 1024 * 1024

C_Q = 0
C_KV = C_Q + Q_LORA
C_KPE = C_KV + KV_LORA
C_KPP = C_KPE + LANES
C_U = C_KPP + LANES
C_V = C_U + SGU_WIDTH
IN_EXT = C_V + SGU_WIDTH


def _rms(x, w):
    return x * lax.rsqrt(jnp.mean(x * x, axis=-1, keepdims=True) + EPS) * w


def _gelu_tanh(x):
    return 0.5 * x * (1.0 + jnp.tanh(np.sqrt(2.0 / np.pi).astype(np.float32) * (x + 0.044715 * (x * x * x))))


def _silu(x):
    return x * (1.0 / (1.0 + jnp.exp(-x)))


def _dot(a, b):
    return jnp.dot(a, b, preferred_element_type=F32)


def _ada_kernel(c_ref, w_ref, b_ref, o_ref):
    s = _silu(c_ref[...]).astype(BF16)
    o_ref[...] = _dot(s, w_ref[...].astype(BF16)) + b_ref[...]


def _ada(cc, w_ada, b_ada):
    n = w_ada.shape[1]
    tn = 1024
    return pl.pallas_call(
        _ada_kernel,
        grid=(n // tn,),
        in_specs=[pl.BlockSpec((SUBLANES, D), lambda j: (0, 0)),
                  pl.BlockSpec((D, tn), lambda j: (0, j)),
                  pl.BlockSpec((1, tn), lambda j: (0, j))],
        out_specs=pl.BlockSpec((SUBLANES, tn), lambda j: (0, j)),
        out_shape=jax.ShapeDtypeStruct((SUBLANES, n), F32),
        compiler_params=pltpu.CompilerParams(dimension_semantics=("arbitrary",), vmem_limit_bytes=VMEM_LIMIT),
        name="ada",
    )(cc, w_ada, b_ada.reshape(1, n))


def _kv_path(hb, win_ref, kvn_ref, wukv_ref, cos, sin, k_ref, v_ref):
    ckv = _dot(hb, win_ref[:, C_KV:C_KV + KV_LORA])
    ckvn = _rms(ckv, kvn_ref[...]).astype(BF16)
    kn = _dot(ckvn, wukv_ref[:, :HEADS * QK_NOPE])
    v_ref[...] = _dot(ckvn, wukv_ref[:, HEADS * QK_NOPE:]).astype(BF16)
    kp = _dot(hb, win_ref[:, C_KPE:C_KPE + 2 * LANES])
    kpe = (kp[:, :LANES] * cos + kp[:, LANES:] * sin).astype(BF16)
    for h in range(HEADS):
        k_ref[:, h * QK_PAD:h * QK_PAD + QK_NOPE] = kn[:, h * QK_NOPE:(h + 1) * QK_NOPE].astype(BF16)
        k_ref[:, h * QK_PAD + QK_NOPE:(h + 1) * QK_PAD] = kpe


def _inproj_kernel(x_ref, mod_ref, pn_ref, win_ref, qn_ref, wuq_ref, kvn_ref, wukv_ref,
                   sgn_ref, sgw_ref, sgb_ref, cs_ref, q_ref, k_ref, v_ref, sgu_ref):
    tm = x_ref.shape[0]
    mod = mod_ref[0]
    h = _rms(x_ref[...], pn_ref[...]) * (1.0 + mod[1:2]) + mod[0:1]
    hb = h.astype(BF16)
    cos = cs_ref[:, :LANES]
    sin = cs_ref[:, LANES:]

    _kv_path(hb, win_ref, kvn_ref, wukv_ref, cos, sin, k_ref, v_ref)

    cq = _dot(hb, win_ref[:, C_Q:C_Q + Q_LORA])
    cqn = _rms(cq, qn_ref[...]).astype(BF16)
    for hp in range(HEADS // 2):
        qm2 = _dot(cqn, wuq_ref[:, hp * 2 * QK_PAD:(hp + 1) * 2 * QK_PAD])
        qp2 = _dot(cqn, wuq_ref[:, HEADS * QK_PAD + hp * 2 * LANES:HEADS * QK_PAD + (hp + 1) * 2 * LANES])
        for i in range(2):
            h_ = 2 * hp + i
            qm = qm2[:, i * QK_PAD:(i + 1) * QK_PAD]
            qp = qp2[:, i * LANES:(i + 1) * LANES]
            q_ref[:, h_ * QK_PAD:h_ * QK_PAD + QK_NOPE] = (qm[:, :QK_NOPE] * Q_SCALE).astype(BF16)
            q_ref[:, h_ * QK_PAD + QK_NOPE:(h_ + 1) * QK_PAD] = (
                (qm[:, QK_NOPE:] * cos + qp * sin) * Q_SCALE).astype(BF16)

    for gp in range(SGU_HEADS // 2):
        u2 = _gelu_tanh(_dot(hb, win_ref[:, C_U + gp * 2 * SGU_DIM:C_U + (gp + 1) * 2 * SGU_DIM]))
        vs2 = _gelu_tanh(_dot(hb, win_ref[:, C_V + gp * 2 * SGU_DIM:C_V + (gp + 1) * 2 * SGU_DIM]))
        for i in range(2):
            g = 2 * gp + i
            gs = slice(g * SGU_DIM, (g + 1) * SGU_DIM)
            u = u2[:, i * SGU_DIM:(i + 1) * SGU_DIM]
            vs = vs2[:, i * SGU_DIM:(i + 1) * SGU_DIM]
            mu = jnp.mean(vs, axis=-1, keepdims=True)
            vc = vs - mu
            var = jnp.mean(vc * vc, axis=-1, keepdims=True)
            vn = (vc * lax.rsqrt(var + EPS) * sgn_ref[:, gs]).astype(BF16)
            wg = sgw_ref[g]
            bias = sgb_ref[:, g:g + 1]
            for n in range(tm // CHUNK):
                rs = slice(n * CHUNK, (n + 1) * CHUNK)
                mixed = _dot(wg, vn[rs, :]) + bias
                sgu_ref[rs, gs] = (u[rs, :] * mixed).astype(BF16)


def _ctx_kv_kernel(x_ref, mod_ref, pn_ref, win_ref, kvn_ref, wukv_ref, cs_ref, k_ref, v_ref):
    mod = mod_ref[0]
    h = _rms(x_ref[...], pn_ref[...]) * (1.0 + mod[1:2]) + mod[0:1]
    _kv_path(h.astype(BF16), win_ref, kvn_ref, wukv_ref, cs_ref[:, :LANES], cs_ref[:, LANES:], k_ref, v_ref)


def _const_spec(shape):
    nd = len(shape)
    return pl.BlockSpec(shape, lambda i: (0,) * nd, pipeline_mode=pl.Buffered(1))


def _inproj(x2, mod3, pn, win, qn, wuq, kvn, wukv, sgn, sgw, sgb, cs, rows_per_batch, tm):
    n = x2.shape[0]
    tpb = rows_per_batch // tm
    row = lambda w: pl.BlockSpec((tm, w), lambda i: (i, 0))
    return pl.pallas_call(
        _inproj_kernel,
        grid=(n // tm,),
        in_specs=[row(D),
                  pl.BlockSpec((1, N_MOD, D), lambda i: (i // tpb, 0, 0)),
                  _const_spec((1, D)), _const_spec(win.shape), _const_spec((1, Q_LORA)),
                  _const_spec(wuq.shape), _const_spec((1, KV_LORA)), _const_spec(wukv.shape),
                  _const_spec((1, SGU_WIDTH)), _const_spec(sgw.shape), _const_spec(sgb.shape),
                  pl.BlockSpec((tm, 2 * LANES), lambda i: (i % tpb, 0))],
        out_specs=[row(HEADS * QK_PAD), row(HEADS * QK_PAD), row(MLA_WIDTH), row(SGU_WIDTH)],
        out_shape=[jax.ShapeDtypeStruct((n, HEADS * QK_PAD), BF16),
                   jax.ShapeDtypeStruct((n, HEADS * QK_PAD), BF16),
                   jax.ShapeDtypeStruct((n, MLA_WIDTH), BF16),
                   jax.ShapeDtypeStruct((n, SGU_WIDTH), BF16)],
        compiler_params=pltpu.CompilerParams(dimension_semantics=("arbitrary",), vmem_limit_bytes=VMEM_LIMIT),
        name="inproj",
    )(x2, mod3, pn, win, qn, wuq, kvn, wukv, sgn, sgw, sgb, cs)


def _ctx_kv(c2, mod3, ctx_row, pn, win, kvn, wukv, cs, tm):
    n = c2.shape[0]
    row = lambda w: pl.BlockSpec((tm, w), lambda i: (i, 0))
    return pl.pallas_call(
        _ctx_kv_kernel,
        grid=(n // tm,),
        in_specs=[row(D),
                  pl.BlockSpec((1, N_MOD, D), lambda i: (ctx_row, 0, 0)),
                  _const_spec((1, D)), _const_spec(win.shape), _const_spec((1, KV_LORA)),
                  _const_spec(wukv.shape),
                  pl.BlockSpec((tm, 2 * LANES), lambda i: (0, 0))],
        out_specs=[row(HEADS * QK_PAD), row(MLA_WIDTH)],
        out_shape=[jax.ShapeDtypeStruct((n, HEADS * QK_PAD), BF16),
                   jax.ShapeDtypeStruct((n, MLA_WIDTH), BF16)],
        compiler_params=pltpu.CompilerParams(dimension_semantics=("arbitrary",), vmem_limit_bytes=VMEM_LIMIT),
        name="ctx_kv",
    )(c2, mod3, pn, win, kvn, wukv, cs)


def _attn_kernel(q_ref, k_ref, v_ref, kc_ref, vc_ref, o_ref, *, tk, streams):
    tq = q_ref.shape[1]
    ts = tq // streams
    nk = k_ref.shape[1] // tk
    qs = [q_ref[0, s * ts:(s + 1) * ts, :] for s in range(streams)]

    def step(q, kb, vb, carry):
        m, l, acc = carry
        s = lax.dot_general(q, kb, (((1,), (1,)), ((), ())), preferred_element_type=F32)
        m_new = jnp.maximum(m, jnp.max(s, axis=-1, keepdims=True))
        alpha = jnp.exp2(m - m_new)
        p = jnp.exp2(s - m_new)
        l = alpha * l + jnp.sum(p, axis=-1, keepdims=True)
        acc = alpha * acc + _dot(p.astype(BF16), vb)
        return m_new, l, acc

    def body(j, carries):
        off = pl.multiple_of(j * tk, tk)
        kb = k_ref[0, pl.ds(off, tk), :]
        vb = v_ref[0, pl.ds(off, tk), :]
        return tuple(step(qs[s], kb, vb, carries[s]) for s in range(streams))

    init = (jnp.full((ts, 1), -jnp.inf, F32), jnp.zeros((ts, 1), F32), jnp.zeros((ts, V_HEAD), F32))
    carries = (init,) * streams
    for j in range(nk):
        carries = body(j, carries)
    for s in range(streams):
        m, l, acc = step(qs[s], kc_ref[0], vc_ref[0], carries[s])
        o_ref[0, s * ts:(s + 1) * ts, :] = (acc / l).astype(BF16)


def _attention(q, k, v, kc, vc, tq, tk):
    b, l, _ = q.shape
    lc = kc.shape[1]
    tk = min(tk, l)
    assert l % tk == 0 and l % tq == 0
    return pl.pallas_call(
        functools.partial(_attn_kernel, tk=tk, streams=1),
        grid=(b, HEADS, l // tq),
        in_specs=[pl.BlockSpec((1, tq, QK_PAD), lambda b_, h, i: (b_, i, h)),
                  pl.BlockSpec((1, l, QK_PAD), lambda b_, h, i: (b_, 0, h)),
                  pl.BlockSpec((1, l, V_HEAD), lambda b_, h, i: (b_, 0, h)),
                  pl.BlockSpec((1, lc, QK_PAD), lambda b_, h, i: (b_, 0, h)),
                  pl.BlockSpec((1, lc, V_HEAD), lambda b_, h, i: (b_, 0, h))],
        out_specs=pl.BlockSpec((1, tq, V_HEAD), lambda b_, h, i: (b_, i, h)),
        out_shape=jax.ShapeDtypeStruct((b, l, MLA_WIDTH), BF16),
        compiler_params=pltpu.CompilerParams(dimension_semantics=("arbitrary",) * 3, vmem_limit_bytes=VMEM_LIMIT),
        name="attn",
    )(q, k, v, kc, vc)


def _outproj_kernel(at_ref, sg_ref, x_ref, mod_ref, wo_ref, pn1_ref, pn2_ref, wr_ref, br_ref,
                    xn_ref, h2_ref, aff_ref):
    tm = x_ref.shape[0]
    mod = mod_ref[0]
    y = _dot(at_ref[...], wo_ref[:MLA_WIDTH, :]) + _dot(sg_ref[...], wo_ref[MLA_WIDTH:, :])
    xn = x_ref[...] + mod[2:3] * _rms(y, pn1_ref[...])
    xn_ref[...] = xn
    h2 = _rms(xn, pn2_ref[...]) * (1.0 + mod[4:5]) + mod[3:4]
    h2_ref[...] = h2
    logits = _dot(h2.astype(BF16), wr_ref[...]) + br_ref[...]
    lane = lax.broadcasted_iota(I32, logits.shape, 1)
    logits = jnp.where(lane < N_EXPERTS, logits, -jnp.inf)
    e = jnp.exp(logits - jnp.max(logits, axis=-1, keepdims=True))
    aff_ref[...] = e / jnp.sum(e, axis=-1, keepdims=True)


def _outproj(attn2, sgu2, x2, mod3, wo, pn1, pn2, wr, br, rows_per_batch, tm):
    n = x2.shape[0]
    tpb = rows_per_batch // tm
    row = lambda w: pl.BlockSpec((tm, w), lambda i: (i, 0))
    return pl.pallas_call(
        _outproj_kernel,
        grid=(n // tm,),
        in_specs=[row(MLA_WIDTH), row(SGU_WIDTH), row(D),
                  pl.BlockSpec((1, N_MOD, D), lambda i: (i // tpb, 0, 0)),
                  _const_spec(wo.shape), _const_spec((1, D)), _const_spec((1, D)),
                  _const_spec(wr.shape), _const_spec((1, LANES))],
        out_specs=[row(D), row(D), row(LANES)],
        out_shape=[jax.ShapeDtypeStruct((n, D), F32),
                   jax.ShapeDtypeStruct((n, D), F32),
                   jax.ShapeDtypeStruct((n, LANES), F32)],
        compiler_params=pltpu.CompilerParams(dimension_semantics=("arbitrary",), vmem_limit_bytes=VMEM_LIMIT),
        name="outproj",
    )(attn2, sgu2, x2, mod3, wo, pn1, pn2, wr, br)


def _prefix_count(x01):
    r, t = x01.shape
    nb = t // LANES
    stacked = jnp.concatenate([x01[:, k * LANES:(k + 1) * LANES] for k in range(nb)], axis=0).astype(BF16)
    ii = lax.broadcasted_iota(I32, (LANES, LANES), 0)
    jj = lax.broadcasted_iota(I32, (LANES, LANES), 1)
    tri = jnp.where(ii <= jj, 1.0, 0.0).astype(BF16)
    within = _dot(stacked, tri)
    off = jnp.zeros((r, 1), F32)
    blocks = []
    for k in range(nb):
        w = within[k * r:(k + 1) * r, :]
        blocks.append(w + off)
        off = off + w[:, LANES - 1:LANES]
    return jnp.concatenate(blocks, axis=1)


ROUTE_SLOTS = 128
ROUTE_SEARCH_STEPS = 192


def _route_kernel(aff_ref, idx_ref, gate_ref, incl_s, isel_s, aff_s, *, cap):
    aff = aff_ref[...]
    nr, lt = aff.shape
    capf = float(cap)

    def search(i, c):
        lo, hi = c
        mid = 0.5 * (lo + hi)
        ge = jnp.sum(jnp.where(aff >= mid, 1.0, 0.0), axis=1, keepdims=True) >= capf
        return jnp.where(ge, mid, lo), jnp.where(ge, hi, mid)

    lo, _ = lax.fori_loop(0, ROUTE_SEARCH_STEPS, search,
                          (jnp.zeros((nr, 1), F32), jnp.full((nr, 1), 2.0, F32)))
    thr = jnp.min(jnp.where(aff >= lo, aff, jnp.inf), axis=1, keepdims=True)
    gt = aff > thr
    eq = aff == thr
    n_gt = jnp.sum(jnp.where(gt, 1.0, 0.0), axis=1, keepdims=True)
    eq_rank = _prefix_count(jnp.where(eq, 1.0, 0.0))
    sel = jnp.where(gt, 1.0, jnp.where(eq, jnp.where(eq_rank <= capf - n_gt, 1.0, 0.0), 0.0))
    incl = _prefix_count(sel)
    isel = incl * sel
    for r in range(nr):
        incl_s[r] = incl[r:r + 1, :]
        isel_s[r] = isel[r:r + 1, :]
        aff_s[r] = aff[r:r + 1, :]

    ns = min(cap, ROUTE_SLOTS)

    def compact(r, _):
        for c0 in range(0, cap, ns):
            slot = (lax.broadcasted_iota(I32, (ns, LANES), 0) + c0).astype(F32)
            cnt = jnp.zeros((ns, LANES), F32)
            gat = jnp.zeros((ns, LANES), F32)
            for k in range(lt // LANES):
                ks = slice(k * LANES, (k + 1) * LANES)
                cnt = cnt + jnp.where(incl_s[r, :, ks] <= slot, 1.0, 0.0)
                gat = gat + jnp.where(isel_s[r, :, ks] == slot + 1.0, aff_s[r, :, ks], 0.0)
            idx_ref[r, c0:c0 + ns, :] = jnp.sum(cnt, axis=1, keepdims=True).astype(I32)
            gate_ref[r, c0:c0 + ns, :] = jnp.sum(gat, axis=1, keepdims=True)
        return 0

    lax.fori_loop(0, nr, compact, 0)


def _route(aff_rows, cap):
    nr, lt = aff_rows.shape
    return pl.pallas_call(
        functools.partial(_route_kernel, cap=cap),
        out_shape=[jax.ShapeDtypeStruct((nr, cap, 1), I32),
                   jax.ShapeDtypeStruct((nr, cap, 1), F32)],
        scratch_shapes=[pltpu.VMEM((nr, 1, lt), F32)] * 3,
        compiler_params=pltpu.CompilerParams(vmem_limit_bytes=VMEM_LIMIT),
        name="route",
    )(aff_rows)


GATHER_UNROLL = 8


def _gather_kernel(idx_ref, h2_ref, xs_ref, sem, *, cap, seq, ne):
    p = pl.program_id(0)
    b = p // ne

    def issue(c, _):
        tok = b * seq + idx_ref[p * cap + c]
        pltpu.make_async_copy(h2_ref.at[pl.ds(tok, 1)], xs_ref.at[pl.ds(p * cap + c, 1)], sem).start()
        return 0

    lax.fori_loop(0, cap, issue, 0, unroll=GATHER_UNROLL)

    def drain_one_pair():
        pltpu.make_async_copy(h2_ref.at[pl.ds(0, cap)], xs_ref.at[pl.ds(0, cap)], sem).wait()

    @pl.when(p > 0)
    def _lagged():
        drain_one_pair()

    @pl.when(p == pl.num_programs(0) - 1)
    def _tail():
        drain_one_pair()


def _gather(idx_flat, h2, nb, ne, seq, cap):
    grid_spec = pltpu.PrefetchScalarGridSpec(
        num_scalar_prefetch=1,
        grid=(nb * ne,),
        in_specs=[pl.BlockSpec(memory_space=pl.ANY)],
        out_specs=pl.BlockSpec(memory_space=pl.ANY),
        scratch_shapes=[pltpu.SemaphoreType.DMA(())],
    )
    return pl.pallas_call(
        functools.partial(_gather_kernel, cap=cap, seq=seq, ne=ne),
        grid_spec=grid_spec,
        out_shape=jax.ShapeDtypeStruct((nb * ne * cap, D), F32),
        compiler_params=pltpu.CompilerParams(dimension_semantics=("arbitrary",)),
        name="gather",
    )(idx_flat, h2)


def _ffn_kernel(xs_ref, gate_ref, wg_ref, wu_ref, wd_ref, o_ref, land, lhs, acc, wgb, wub, wdb, sem, *, cap):
    e = pl.program_id(0)
    f = pl.program_id(1)
    b = pl.program_id(2)
    ne = pl.num_programs(0)
    nf = pl.num_programs(1)
    nb = pl.num_programs(2)
    pair = b * ne + e

    @pl.when(b == 0)
    def _cast_weights():
        wgb[...] = wg_ref[0].astype(BF16)
        wub[...] = wu_ref[0].astype(BF16)
        wdb[...] = wd_ref[0].astype(BF16)

    def rows_copy(p):
        return pltpu.make_async_copy(xs_ref.at[pl.ds(pl.multiple_of(p * cap, cap), cap)], land, sem)

    @pl.when(f == 0)
    def _take_rows():
        @pl.when(jnp.logical_and(e == 0, b == 0))
        def _first():
            rows_copy(pair).start()

        rows_copy(pair).wait()
        lhs[b] = land[...].astype(BF16)
        acc[b] = jnp.zeros((cap, D), F32)

        @pl.when(jnp.logical_not(jnp.logical_and(e == ne - 1, b == nb - 1)))
        def _prefetch():
            rows_copy(jnp.where(b == nb - 1, e + 1, pair + ne)).start()

    x = lhs[b]
    a = _dot(x, wgb[...])
    g = _dot(x, wub[...])
    hm = (_silu(a) * g).astype(BF16)
    acc[b] += _dot(hm, wdb[...])

    @pl.when(f == nf - 1)
    def _emit():
        gate = gate_ref[0]
        for j in range(ROW_TILES):
            o_ref[pl.ds(j, cap, stride=ROW_TILES), :] = acc[b, :, j * LANES:(j + 1) * LANES] * gate


def _ffn(xs, gate_col, w_gate, w_up, w_down, nb, cap, tf):
    ne, _, ff = w_gate.shape
    nf = ff // tf
    pair_block = lambda e, f, b: (jnp.where(f == nf - 1, b, 0) * ne + e, 0)
    return pl.pallas_call(
        functools.partial(_ffn_kernel, cap=cap),
        grid=(ne, nf, nb),
        in_specs=[pl.BlockSpec(memory_space=pl.ANY),
                  pl.BlockSpec((1, cap, 1), lambda e, f, b: pair_block(e, f, b) + (0,)),
                  pl.BlockSpec((1, D, tf), lambda e, f, b: (e, 0, f)),
                  pl.BlockSpec((1, D, tf), lambda e, f, b: (e, 0, f)),
                  pl.BlockSpec((1, tf, D), lambda e, f, b: (e, f, 0))],
        out_specs=pl.BlockSpec((cap * ROW_TILES, LANES), pair_block),
        scratch_shapes=[pltpu.VMEM((cap, D), F32),
                        pltpu.VMEM((nb, cap, D), BF16),
                        pltpu.VMEM((nb, cap, D), F32),
                        pltpu.VMEM((D, tf), BF16),
                        pltpu.VMEM((D, tf), BF16),
                        pltpu.VMEM((tf, D), BF16),
                        pltpu.SemaphoreType.DMA(())],
        out_shape=jax.ShapeDtypeStruct((nb * ne * cap * ROW_TILES, LANES), F32),
        compiler_params=pltpu.CompilerParams(dimension_semantics=("arbitrary",) * 3, vmem_limit_bytes=VMEM_LIMIT),
        name="ffn",
    )(xs, gate_col, w_gate, w_up, w_down)


COMBINE_UNROLL = 8


def _combine_kernel(idx_ref, ye_ref, y_ref, *, cap):
    b = pl.program_id(0)
    e = pl.program_id(1)
    pair = b * pl.num_programs(1) + e

    @pl.when(e == 0)
    def _zero():
        y_ref[...] = jnp.zeros(y_ref.shape, F32)

    def group(gi, _):
        base = gi * COMBINE_UNROLL
        toks = [idx_ref[pair * cap + base + u] for u in range(COMBINE_UNROLL)]
        rows = [y_ref[toks[u]] + ye_ref[base + u] for u in range(COMBINE_UNROLL)]
        for u in range(COMBINE_UNROLL):
            y_ref[toks[u]] = rows[u]
        return 0

    lax.fori_loop(0, cap // COMBINE_UNROLL, group, 0)


def _combine(idx_flat, ye, nb, seq, cap):
    ne = ye.shape[0] // (nb * cap * ROW_TILES)
    ye4 = ye.reshape(nb * ne, cap, ROW_TILES, LANES)
    grid_spec = pltpu.PrefetchScalarGridSpec(
        num_scalar_prefetch=1,
        grid=(nb, ne),
        in_specs=[pl.BlockSpec((None, cap, ROW_TILES, LANES), lambda b, e, idx: (b * ne + e, 0, 0, 0))],
        out_specs=pl.BlockSpec((None, seq, ROW_TILES, LANES), lambda b, e, idx: (b, 0, 0, 0),
                               pipeline_mode=pl.Buffered(1)),
    )
    return pl.pallas_call(
        functools.partial(_combine_kernel, cap=cap),
        grid_spec=grid_spec,
        out_shape=jax.ShapeDtypeStruct((nb, seq, ROW_TILES, LANES), F32),
        compiler_params=pltpu.CompilerParams(dimension_semantics=("arbitrary",) * 2, vmem_limit_bytes=VMEM_LIMIT),
        name="combine",
    )(idx_flat, ye4)


def _final_kernel(y_ref, xn_ref, mod_ref, pn_ref, o_ref):
    tm = xn_ref.shape[0]
    g2 = mod_ref[0][5:6]
    ss = jnp.zeros((tm, LANES), F32)
    for j in range(ROW_TILES):
        v = y_ref[pl.ds(j, tm, stride=ROW_TILES), :]
        ss = ss + v * v
    rs = lax.rsqrt(jnp.sum(ss, axis=-1, keepdims=True) * (1.0 / D) + EPS)
    for j in range(ROW_TILES):
        cs = slice(j * LANES, (j + 1) * LANES)
        v = y_ref[pl.ds(j, tm, stride=ROW_TILES), :]
        o_ref[:, cs] = xn_ref[:, cs] + g2[:, cs] * (v * rs * pn_ref[:, cs])


def _final(y2_rows, xn, mod3, pn, rows_per_batch, tm):
    n = xn.shape[0]
    tpb = rows_per_batch // tm
    return pl.pallas_call(
        _final_kernel,
        grid=(n // tm,),
        in_specs=[pl.BlockSpec((tm * ROW_TILES, LANES), lambda i: (i, 0)),
                  pl.BlockSpec((tm, D), lambda i: (i, 0)),
                  pl.BlockSpec((1, N_MOD, D), lambda i: (i // tpb, 0, 0)),
                  _const_spec((1, D))],
        out_specs=pl.BlockSpec((tm, D), lambda i: (i, 0)),
        out_shape=jax.ShapeDtypeStruct((n, D), F32),
        compiler_params=pltpu.CompilerParams(dimension_semantics=("arbitrary",)),
        name="final",
    )(y2_rows, xn, mod3, pn)


def _rope_partner():
    q = QK_ROPE // 4
    return np.concatenate([np.arange(q, 2 * q), np.arange(0, q), np.arange(3 * q, 4 * q), np.arange(2 * q, 3 * q)])


def _rope_table(length):
    pos = np.arange(length)
    half = QK_ROPE // 2
    inv = (1.0 / (ROPE_THETA ** (np.arange(0, half, 2, dtype=np.float32) / half))).astype(np.float32)
    ar = (pos // GRID_W).astype(np.float32)[:, None] * inv
    ac = (pos % GRID_W).astype(np.float32)[:, None] * inv
    cos = np.concatenate([np.cos(ar), np.cos(ar), np.cos(ac), np.cos(ac)], axis=1)
    sin = np.concatenate([-np.sin(ar), np.sin(ar), -np.sin(ac), np.sin(ac)], axis=1)
    z = np.zeros((length, LANES - QK_ROPE), np.float32)
    return jnp.asarray(np.concatenate([cos, z, sin, z], axis=1).astype(np.float32))


def _identity_rope_table(length):
    t = np.zeros((length, 2 * LANES), np.float32)
    t[:, :QK_ROPE] = 1.0
    return jnp.asarray(t)


def _prep_w_in(w_in):
    perm = _rope_partner()
    kpe = w_in[:, 2 * Q_LORA:2 * Q_LORA + QK_ROPE]
    z = jnp.zeros((D, LANES - QK_ROPE), w_in.dtype)
    rest = w_in[:, 2 * Q_LORA + QK_ROPE:]
    return jnp.concatenate([w_in[:, :2 * Q_LORA], kpe, z, kpe[:, perm], z, rest], axis=1).astype(BF16)


def _prep_w_uq(w_uq):
    perm = _rope_partner()
    w = w_uq.reshape(Q_LORA, HEADS, QK_NOPE + QK_ROPE)
    z = jnp.zeros((Q_LORA, HEADS, LANES - QK_ROPE), w_uq.dtype)
    main = jnp.concatenate([w, z], axis=2).reshape(Q_LORA, HEADS * QK_PAD)
    partner = jnp.concatenate([w[:, :, QK_NOPE:][:, :, perm], z], axis=2).reshape(Q_LORA, HEADS * LANES)
    return jnp.concatenate([main, partner], axis=1).astype(BF16)


def _prep_w_ukv(w_ukv):
    w = w_ukv.reshape(KV_LORA, HEADS, 2, QK_NOPE)
    return w.transpose(0, 2, 1, 3).reshape(KV_LORA, 2 * HEADS * QK_NOPE).astype(BF16)


def kernel(x, c, ctx, c_ctx, w_ada, b_ada, pre_norm1, w_in, q_norm_w, w_uq, kv_norm_w, w_ukv, sgu_norm_w, sgu_w,
           sgu_b, w_out, post_norm1, pre_norm2, w_router, b_router, w_e_gate, w_e_up, w_e_down, post_norm2):
    nb, seq, _ = x.shape
    lc = ctx.shape[1]
    depth = w_ada.shape[0]
    assert depth == 1 and seq % 512 == 0 and lc % 128 == 0 and nb < SUBLANES
    cap = CAP_FACTOR * seq // N_EXPERTS
    n = nb * seq
    tm = 256

    cc = jnp.zeros((SUBLANES, D), F32).at[:nb].set(c).at[nb].set(c_ctx)
    mod3 = _ada(cc, w_ada[0], b_ada[0]).reshape(SUBLANES, N_MOD, D)

    row = lambda w: w.reshape(1, -1)
    win = _prep_w_in(w_in[0])
    wuq = _prep_w_uq(w_uq[0])
    wukv = _prep_w_ukv(w_ukv[0])
    x2 = x.reshape(n, D)
    q, k, v, sgu = _inproj(x2, mod3, row(pre_norm1[0]), win, row(q_norm_w[0]), wuq, row(kv_norm_w[0]), wukv,
                           row(sgu_norm_w[0]), sgu_w[0].astype(BF16), sgu_b[0].T, _rope_table(seq), seq, 2 * tm)
    kc, vc = _ctx_kv(ctx.reshape(nb * lc, D), mod3, nb, row(pre_norm1[0]), win, row(kv_norm_w[0]), wukv,
                     _identity_rope_table(lc), lc)

    attn = _attention(q.reshape(nb, seq, -1), k.reshape(nb, seq, -1), v.reshape(nb, seq, -1),
                      kc.reshape(nb, lc, -1), vc.reshape(nb, lc, -1), tq=512, tk=4096)

    wr = jnp.zeros((D, LANES), BF16).at[:, :N_EXPERTS].set(w_router[0].astype(BF16))
    br = jnp.zeros((1, LANES), F32).at[0, :N_EXPERTS].set(b_router[0])
    xn, h2_rows, aff = _outproj(attn.reshape(n, MLA_WIDTH), sgu, x2, mod3, w_out[0].astype(BF16),
                                row(post_norm1[0]), row(pre_norm2[0]), wr, br, seq, tm)

    aff_t = aff[:, :N_EXPERTS].reshape(nb, seq, N_EXPERTS).transpose(0, 2, 1)
    idx_col, gate_col = _route(aff_t.reshape(nb * N_EXPERTS, seq), cap)
    idx_flat = idx_col.reshape(-1)

    xs = _gather(idx_flat, h2_rows, nb, N_EXPERTS, seq, cap)
    ye = _ffn(xs, gate_col, w_e_gate[0], w_e_up[0], w_e_down[0], nb, cap, tf=256)
    y2 = _combine(idx_flat, ye, nb, seq, cap)
    out = _final(y2.reshape(n * ROW_TILES, LANES), xn, mod3, row(post_norm2[0]), seq, tm)
    return out.reshape(nb, seq, D)
```

```python
import functools

import numpy as np
import jax
import jax.numpy as jnp
from jax import lax
from jax.experimental import pallas as pl
from jax.experimental.pallas import tpu as pltpu

F32 = jnp.float32
BF16 = jnp.bfloat16
I32 = jnp.int32

D = 2048
GRID_W = 64
EPS = 1e-6
N_MOD = 6
HEADS = 8
Q_LORA = 512
KV_LORA = 512
QK_NOPE = 128
QK_ROPE = 64
V_HEAD = 128
ROPE_THETA = 10000.0
ATTN_SCALE = (QK_NOPE + QK_ROPE) ** -0.5
Q_SCALE = ATTN_SCALE * float(np.log2(np.e))
SGU_HEADS = 8
SGU_DIM = 128
CHUNK = 128
MLA_WIDTH = HEADS * V_HEAD
SGU_WIDTH = SGU_HEADS * SGU_DIM
N_EXPERTS = 16
CAP_FACTOR = 2
EXPERT_FF = D

LANES = 128
SUBLANES = 8
ROW_TILES = D // LANES
QK_PAD = 256
VMEM_LIMIT = 56 * 1024 * 1024

C_Q = 0
C_KV = C_Q + Q_LORA
C_KPE = C_KV + KV_LORA
C_KPP = C_KPE + LANES
C_U = C_KPP + LANES
C_V = C_U + SGU_WIDTH
IN_EXT = C_V + SGU_WIDTH


def _rms(x, w):
    return x * lax.rsqrt(jnp.mean(x * x, axis=-1, keepdims=True) + EPS) * w


def _gelu_tanh(x):
    return 0.5 * x * (1.0 + jnp.tanh(np.sqrt(2.0 / np.pi).astype(np.float32) * (x + 0.044715 * (x * x * x))))


def _silu(x):
    return x * (1.0 / (1.0 + jnp.exp(-x)))


def _dot(a, b):
    return jnp.dot(a, b, preferred_element_type=F32)


def _ada_kernel(c_ref, w_ref, b_ref, o_ref):
    s = _silu(c_ref[...]).astype(BF16)
    o_ref[...] = _dot(s, w_ref[...].astype(BF16)) + b_ref[...]


def _ada(cc, w_ada, b_ada):
    n = w_ada.shape[1]
    tn = 1024
    return pl.pallas_call(
        _ada_kernel,
        grid=(n // tn,),
        in_specs=[pl.BlockSpec((SUBLANES, D), lambda j: (0, 0)),
                  pl.BlockSpec((D, tn), lambda j: (0, j)),
                  pl.BlockSpec((1, tn), lambda j: (0, j))],
        out_specs=pl.BlockSpec((SUBLANES, tn), lambda j: (0, j)),
        out_shape=jax.ShapeDtypeStruct((SUBLANES, n), F32),
        compiler_params=pltpu.CompilerParams(dimension_semantics=("arbitrary",), vmem_limit_bytes=VMEM_LIMIT),
        name="ada",
    )(cc, w_ada, b_ada.reshape(1, n))


def _kv_path(hb, win_ref, kvn_ref, wukv_ref, cos, sin, k_ref, v_ref):
    ckv = _dot(hb, win_ref[:, C_KV:C_KV + KV_LORA])
    ckvn = _rms(ckv, kvn_ref[...]).astype(BF16)
    kn = _dot(ckvn, wukv_ref[:, :HEADS * QK_NOPE])
    v_ref[...] = _dot(ckvn, wukv_ref[:, HEADS * QK_NOPE:]).astype(BF16)
    kp = _dot(hb, win_ref[:, C_KPE:C_KPE + 2 * LANES])
    kpe = (kp[:, :LANES] * cos + kp[:, LANES:] * sin).astype(BF16)
    for h in range(HEADS):
        k_ref[:, h * QK_PAD:h * QK_PAD + QK_NOPE] = kn[:, h * QK_NOPE:(h + 1) * QK_NOPE].astype(BF16)
        k_ref[:, h * QK_PAD + QK_NOPE:(h + 1) * QK_PAD] = kpe


def _inproj_kernel(x_ref, mod_ref, pn_ref, win_ref, qn_ref, wuq_ref, kvn_ref, wukv_ref,
                   sgn_ref, sgw_ref, sgb_ref, cs_ref, q_ref, k_ref, v_ref, sgu_ref):
    tm = x_ref.shape[0]
    mod = mod_ref[0]
    h = _rms(x_ref[...], pn_ref[...]) * (1.0 + mod[1:2]) + mod[0:1]
    hb = h.astype(BF16)
    cos = cs_ref[:, :LANES]
    sin = cs_ref[:, LANES:]

    _kv_path(hb, win_ref, kvn_ref, wukv_ref, cos, sin, k_ref, v_ref)

    cq = _dot(hb, win_ref[:, C_Q:C_Q + Q_LORA])
    cqn = _rms(cq, qn_ref[...]).astype(BF16)
    for hp in range(HEADS // 2):
        qm2 = _dot(cqn, wuq_ref[:, hp * 2 * QK_PAD:(hp + 1) * 2 * QK_PAD])
        qp2 = _dot(cqn, wuq_ref[:, HEADS * QK_PAD + hp * 2 * LANES:HEADS * QK_PAD + (hp + 1) * 2 * LANES])
        for i in range(2):
            h_ = 2 * hp + i
            qm = qm2[:, i * QK_PAD:(i + 1) * QK_PAD]
            qp = qp2[:, i * LANES:(i + 1) * LANES]
            q_ref[:, h_ * QK_PAD:h_ * QK_PAD + QK_NOPE] = (qm[:, :QK_NOPE] * Q_SCALE).astype(BF16)
            q_ref[:, h_ * QK_PAD + QK_NOPE:(h_ + 1) * QK_PAD] = (
                (qm[:, QK_NOPE:] * cos + qp * sin) * Q_SCALE).astype(BF16)

    for gp in range(SGU_HEADS // 2):
        u2 = _gelu_tanh(_dot(hb, win_ref[:, C_U + gp * 2 * SGU_DIM:C_U + (gp + 1) * 2 * SGU_DIM]))
        vs2 = _gelu_tanh(_dot(hb, win_ref[:, C_V + gp * 2 * SGU_DIM:C_V + (gp + 1) * 2 * SGU_DIM]))
        for i in range(2):
            g = 2 * gp + i
            gs = slice(g * SGU_DIM, (g + 1) * SGU_DIM)
            u = u2[:, i * SGU_DIM:(i + 1) * SGU_DIM]
            vs = vs2[:, i * SGU_DIM:(i + 1) * SGU_DIM]
            mu = jnp.mean(vs, axis=-1, keepdims=True)
            vc = vs - mu
            var = jnp.mean(vc * vc, axis=-1, keepdims=True)
            vn = (vc * lax.rsqrt(var + EPS) * sgn_ref[:, gs]).astype(BF16)
            wg = sgw_ref[g]
            bias = sgb_ref[:, g:g + 1]
            for n in range(tm // CHUNK):
                rs = slice(n * CHUNK, (n + 1) * CHUNK)
                mixed = _dot(wg, vn[rs, :]) + bias
                sgu_ref[rs, gs] = (u[rs, :] * mixed).astype(BF16)


def _ctx_kv_kernel(x_ref, mod_ref, pn_ref, win_ref, kvn_ref, wukv_ref, cs_ref, k_ref, v_ref):
    mod = mod_ref[0]
    h = _rms(x_ref[...], pn_ref[...]) * (1.0 + mod[1:2]) + mod[0:1]
    _kv_path(h.astype(BF16), win_ref, kvn_ref, wukv_ref, cs_ref[:, :LANES], cs_ref[:, LANES:], k_ref, v_ref)


def _const_spec(shape):
    nd = len(shape)
    return pl.BlockSpec(shape, lambda i: (0,) * nd, pipeline_mode=pl.Buffered(1))


def _inproj(x2, mod3, pn, win, qn, wuq, kvn, wukv, sgn, sgw, sgb, cs, rows_per_batch, tm):
    n = x2.shape[0]
    tpb = rows_per_batch // tm
    row = lambda w: pl.BlockSpec((tm, w), lambda i: (i, 0))
    return pl.pallas_call(
        _inproj_kernel,
        grid=(n // tm,),
        in_specs=[row(D),
                  pl.BlockSpec((1, N_MOD, D), lambda i: (i // tpb, 0, 0)),
                  _const_spec((1, D)), _const_spec(win.shape), _const_spec((1, Q_LORA)),
                  _const_spec(wuq.shape), _const_spec((1, KV_LORA)), _const_spec(wukv.shape),
                  _const_spec((1, SGU_WIDTH)), _const_spec(sgw.shape), _const_spec(sgb.shape),
                  pl.BlockSpec((tm, 2 * LANES), lambda i: (i % tpb, 0))],
        out_specs=[row(HEADS * QK_PAD), row(HEADS * QK_PAD), row(MLA_WIDTH), row(SGU_WIDTH)],
        out_shape=[jax.ShapeDtypeStruct((n, HEADS * QK_PAD), BF16),
                   jax.ShapeDtypeStruct((n, HEADS * QK_PAD), BF16),
                   jax.ShapeDtypeStruct((n, MLA_WIDTH), BF16),
                   jax.ShapeDtypeStruct((n, SGU_WIDTH), BF16)],
        compiler_params=pltpu.CompilerParams(dimension_semantics=("arbitrary",), vmem_limit_bytes=VMEM_LIMIT),
        name="inproj",
    )(x2, mod3, pn, win, qn, wuq, kvn, wukv, sgn, sgw, sgb, cs)


def _ctx_kv(c2, mod3, ctx_row, pn, win, kvn, wukv, cs, tm):
    n = c2.shape[0]
    row = lambda w: pl.BlockSpec((tm, w), lambda i: (i, 0))
    return pl.pallas_call(
        _ctx_kv_kernel,
        grid=(n // tm,),
        in_specs=[row(D),
                  pl.BlockSpec((1, N_MOD, D), lambda i: (ctx_row, 0, 0)),
                  _const_spec((1, D)), _const_spec(win.shape), _const_spec((1, KV_LORA)),
                  _const_spec(wukv.shape),
                  pl.BlockSpec((tm, 2 * LANES), lambda i: (0, 0))],
        out_specs=[row(HEADS * QK_PAD), row(MLA_WIDTH)],
        out_shape=[jax.ShapeDtypeStruct((n, HEADS * QK_PAD), BF16),
                   jax.ShapeDtypeStruct((n, MLA_WIDTH), BF16)],
        compiler_params=pltpu.CompilerParams(dimension_semantics=("arbitrary",), vmem_limit_bytes=VMEM_LIMIT),
        name="ctx_kv",
    )(c2, mod3, pn, win, kvn, wukv, cs)


def _attn_kernel(q_ref, k_ref, v_ref, kc_ref, vc_ref, o_ref, *, tk, streams):
    tq = q_ref.shape[1]
    ts = tq // streams
    nk = k_ref.shape[1] // tk
    qs = [q_ref[0, s * ts:(s + 1) * ts, :] for s in range(streams)]

    def step(q, kb, vb, carry):
        m, l, acc = carry
        s = lax.dot_general(q, kb, (((1,), (1,)), ((), ())), preferred_element_type=F32)
        m_new = jnp.maximum(m, jnp.max(s, axis=-1, keepdims=True))
        alpha = jnp.exp2(m - m_new)
        p = jnp.exp2(s - m_new)
        l = alpha * l + jnp.sum(p, axis=-1, keepdims=True)
        acc = alpha * acc + _dot(p.astype(BF16), vb)
        return m_new, l, acc

    def body(j, carries):
        off = pl.multiple_of(j * tk, tk)
        kb = k_ref[0, pl.ds(off, tk), :]
        vb = v_ref[0, pl.ds(off, tk), :]
        return tuple(step(qs[s], kb, vb, carries[s]) for s in range(streams))

    init = (jnp.full((ts, 1), -jnp.inf, F32), jnp.zeros((ts, 1), F32), jnp.zeros((ts, V_HEAD), F32))
    carries = (init,) * streams
    for j in range(nk):
        carries = body(j, carries)
    for s in range(streams):
        m, l, acc = step(qs[s], kc_ref[0], vc_ref[0], carries[s])
        o_ref[0, s * ts:(s + 1) * ts, :] = (acc / l).astype(BF16)


def _attention(q, k, v, kc, vc, tq, tk):
    b, l, _ = q.shape
    lc = kc.shape[1]
    tk = min(tk, l)
    assert l % tk == 0 and l % tq == 0
    return pl.pallas_call(
        functools.partial(_attn_kernel, tk=tk, streams=1),
        grid=(b, HEADS, l // tq),
        in_specs=[pl.BlockSpec((1, tq, QK_PAD), lambda b_, h, i: (b_, i, h)),
                  pl.BlockSpec((1, l, QK_PAD), lambda b_, h, i: (b_, 0, h)),
                  pl.BlockSpec((1, l, V_HEAD), lambda b_, h, i: (b_, 0, h)),
                  pl.BlockSpec((1, lc, QK_PAD), lambda b_, h, i: (b_, 0, h)),
                  pl.BlockSpec((1, lc, V_HEAD), lambda b_, h, i: (b_, 0, h))],
        out_specs=pl.BlockSpec((1, tq, V_HEAD), lambda b_, h, i: (b_, i, h)),
        out_shape=jax.ShapeDtypeStruct((b, l, MLA_WIDTH), BF16),
        compiler_params=pltpu.CompilerParams(dimension_semantics=("arbitrary",) * 3, vmem_limit_bytes=VMEM_LIMIT),
        name="attn",
    )(q, k, v, kc, vc)


def _outproj_kernel(at_ref, sg_ref, x_ref, mod_ref, wo_ref, pn1_ref, pn2_ref, wr_ref, br_ref,
                    xn_ref, h2_ref, aff_ref):
    tm = x_ref.shape[0]
    mod = mod_ref[0]
    y = _dot(at_ref[...], wo_ref[:MLA_WIDTH, :]) + _dot(sg_ref[...], wo_ref[MLA_WIDTH:, :])
    xn = x_ref[...] + mod[2:3] * _rms(y, pn1_ref[...])
    xn_ref[...] = xn
    h2 = _rms(xn, pn2_ref[...]) * (1.0 + mod[4:5]) + mod[3:4]
    h2_ref[...] = h2
    logits = _dot(h2.astype(BF16), wr_ref[...]) + br_ref[...]
    lane = lax.broadcasted_iota(I32, logits.shape, 1)
    logits = jnp.where(lane < N_EXPERTS, logits, -jnp.inf)
    e = jnp.exp(logits - jnp.max(logits, axis=-1, keepdims=True))
    aff_ref[...] = e / jnp.sum(e, axis=-1, keepdims=True)


def _outproj(attn2, sgu2, x2, mod3, wo, pn1, pn2, wr, br, rows_per_batch, tm):
    n = x2.shape[0]
    tpb = rows_per_batch // tm
    row = lambda w: pl.BlockSpec((tm, w), lambda i: (i, 0))
    return pl.pallas_call(
        _outproj_kernel,
        grid=(n // tm,),
        in_specs=[row(MLA_WIDTH), row(SGU_WIDTH), row(D),
                  pl.BlockSpec((1, N_MOD, D), lambda i: (i // tpb, 0, 0)),
                  _const_spec(wo.shape), _const_spec((1, D)), _const_spec((1, D)),
                  _const_spec(wr.shape), _const_spec((1, LANES))],
        out_specs=[row(D), row(D), row(LANES)],
        out_shape=[jax.ShapeDtypeStruct((n, D), F32),
                   jax.ShapeDtypeStruct((n, D), F32),
                   jax.ShapeDtypeStruct((n, LANES), F32)],
        compiler_params=pltpu.CompilerParams(dimension_semantics=("arbitrary",), vmem_limit_bytes=VMEM_LIMIT),
        name="outproj",
    )(attn2, sgu2, x2, mod3, wo, pn1, pn2, wr, br)


def _prefix_count(x01):
    r, t = x01.shape
    nb = t // LANES
    stacked = jnp.concatenate([x01[:, k * LANES:(k + 1) * LANES] for k in range(nb)], axis=0).astype(BF16)
    ii = lax.broadcasted_iota(I32, (LANES, LANES), 0)
    jj = lax.broadcasted_iota(I32, (LANES, LANES), 1)
    tri = jnp.where(ii <= jj, 1.0, 0.0).astype(BF16)
    within = _dot(stacked, tri)
    off = jnp.zeros((r, 1), F32)
    blocks = []
    for k in range(nb):
        w = within[k * r:(k + 1) * r, :]
        blocks.append(w + off)
        off = off + w[:, LANES - 1:LANES]
    return jnp.concatenate(blocks, axis=1)


ROUTE_SLOTS = 128
ROUTE_SEARCH_STEPS = 192


def _route_kernel(aff_ref, idx_ref, gate_ref, incl_s, isel_s, aff_s, *, cap):
    aff = aff_ref[...]
    nr, lt = aff.shape
    capf = float(cap)

    def search(i, c):
        lo, hi = c
        mid = 0.5 * (lo + hi)
        ge = jnp.sum(jnp.where(aff >= mid, 1.0, 0.0), axis=1, keepdims=True) >= capf
        return jnp.where(ge, mid, lo), jnp.where(ge, hi, mid)

    lo, _ = lax.fori_loop(0, ROUTE_SEARCH_STEPS, search,
                          (jnp.zeros((nr, 1), F32), jnp.full((nr, 1), 2.0, F32)))
    thr = jnp.min(jnp.where(aff >= lo, aff, jnp.inf), axis=1, keepdims=True)
    gt = aff > thr
    eq = aff == thr
    n_gt = jnp.sum(jnp.where(gt, 1.0, 0.0), axis=1, keepdims=True)
    eq_rank = _prefix_count(jnp.where(eq, 1.0, 0.0))
    sel = jnp.where(gt, 1.0, jnp.where(eq, jnp.where(eq_rank <= capf - n_gt, 1.0, 0.0), 0.0))
    incl = _prefix_count(sel)
    isel = incl * sel
    for r in range(nr):
        incl_s[r] = incl[r:r + 1, :]
        isel_s[r] = isel[r:r + 1, :]
        aff_s[r] = aff[r:r + 1, :]

    ns = min(cap, ROUTE_SLOTS)

    def compact(r, _):
        for c0 in range(0, cap, ns):
            slot = (lax.broadcasted_iota(I32, (ns, LANES), 0) + c0).astype(F32)
            cnt = jnp.zeros((ns, LANES), F32)
            gat = jnp.zeros((ns, LANES), F32)
            for k in range(lt // LANES):
                ks = slice(k * LANES, (k + 1) * LANES)
                cnt = cnt + jnp.where(incl_s[r, :, ks] <= slot, 1.0, 0.0)
                gat = gat + jnp.where(isel_s[r, :, ks] == slot + 1.0, aff_s[r, :, ks], 0.0)
            idx_ref[r, c0:c0 + ns, :] = jnp.sum(cnt, axis=1, keepdims=True).astype(I32)
            gate_ref[r, c0:c0 + ns, :] = jnp.sum(gat, axis=1, keepdims=True)
        return 0

    lax.fori_loop(0, nr, compact, 0)


def _route(aff_rows, cap):
    nr, lt = aff_rows.shape
    return pl.pallas_call(
        functools.partial(_route_kernel, cap=cap),
        out_shape=[jax.ShapeDtypeStruct((nr, cap, 1), I32),
                   jax.ShapeDtypeStruct((nr, cap, 1), F32)],
        scratch_shapes=[pltpu.VMEM((nr, 1, lt), F32)] * 3,
        compiler_params=pltpu.CompilerParams(vmem_limit_bytes=VMEM_LIMIT),
        name="route",
    )(aff_rows)


BATCH_GROUP = 2


def _ffn_kernel(idx_ref, h2_ref, gate_ref, wg_ref, wu_ref, wd_ref, o_ref, ring, lhs, acc, wgb, wub, wdb, sems,
                *, cap, seq, chunk):
    p = pl.program_id(0)
    e = pl.program_id(1)
    f = pl.program_id(2)
    bb = pl.program_id(3)
    ne = pl.num_programs(1)
    nf = pl.num_programs(2)
    steps = nf * BATCH_GROUP
    cpb = cap // chunk
    n_items = pl.num_programs(0) * ne
    w = p * ne + e
    s = f * BATCH_GROUP + bb
    g = w * steps + s

    def issue_chunk(item, k, slot):
        bi = lax.div(item, ne) * BATCH_GROUP + lax.div(k, cpb)
        base = (bi * ne + lax.rem(item, ne)) * cap + lax.rem(k, cpb) * chunk
        for i in range(chunk):
            tok = bi * seq + idx_ref[base + i]
            pltpu.make_async_copy(h2_ref.at[pl.ds(tok, 1)], ring.at[slot, pl.ds(i, 1)], sems.at[slot]).start()

    def wait_chunk(slot):
        pltpu.make_async_copy(h2_ref.at[pl.ds(0, chunk)], ring.at[slot], sems.at[slot]).wait()

    def drain_chunk(slot, par, k):
        wait_chunk(slot)
        r0 = pl.multiple_of(lax.rem(k, cpb) * chunk, chunk)
        lhs[par, lax.div(k, cpb), pl.ds(r0, chunk), :] = ring[slot].astype(BF16)

    @pl.when(g == 0)
    def _prologue():
        def fetch(k, _):
            issue_chunk(0, k, 0)
            drain_chunk(0, 0, k)
            return 0
        lax.fori_loop(0, steps - 1, fetch, 0)
        issue_chunk(0, steps - 1, 1)

    gq = g - 1 + steps
    drain_chunk(lax.rem(g + 1, 2), lax.rem(lax.div(gq, steps), 2), lax.rem(gq, steps))

    @pl.when(bb == 0)
    def _cast_weights():
        wgb[...] = wg_ref[0].astype(BF16)
        wub[...] = wu_ref[0].astype(BF16)
        wdb[...] = wd_ref[0].astype(BF16)

    @pl.when(f == 0)
    def _zero():
        acc[bb] = jnp.zeros((cap, D), F32)

    issue_chunk(jnp.minimum(w + 1, n_items - 1), s, lax.rem(g, 2))
    x = lhs[lax.rem(w, 2), bb]
    a = _dot(x, wgb[...])
    gg = _dot(x, wub[...])
    hm = (_silu(a) * gg).astype(BF16)
    acc[bb] += _dot(hm, wdb[...])

    @pl.when(f == nf - 1)
    def _emit():
        gate = gate_ref[0]
        for j in range(ROW_TILES):
            o_ref[pl.ds(j, cap, stride=ROW_TILES), :] = acc[bb, :, j * LANES:(j + 1) * LANES] * gate

    @pl.when(g == n_items * steps - 1)
    def _tail():
        wait_chunk(lax.rem(g, 2))


def _ffn(idx_flat, h2, gate_col, w_gate, w_up, w_down, nb, seq, cap, tf):
    ne, _, ff = w_gate.shape
    nf = ff // tf
    chunk = cap // nf
    assert nb % BATCH_GROUP == 0 and cap % nf == 0 and chunk % 16 == 0
    pair_block = lambda p, e, f, bb, idx: ((p * BATCH_GROUP + jnp.where(f == nf - 1, bb, 0)) * ne + e, 0)
    grid_spec = pltpu.PrefetchScalarGridSpec(
        num_scalar_prefetch=1,
        grid=(nb // BATCH_GROUP, ne, nf, BATCH_GROUP),
        in_specs=[pl.BlockSpec(memory_space=pl.ANY),
                  pl.BlockSpec((1, cap, 1), lambda p, e, f, bb, idx: pair_block(p, e, f, bb, idx) + (0,)),
                  pl.BlockSpec((1, D, tf), lambda p, e, f, bb, idx: (e, 0, f)),
                  pl.BlockSpec((1, D, tf), lambda p, e, f, bb, idx: (e, 0, f)),
                  pl.BlockSpec((1, tf, D), lambda p, e, f, bb, idx: (e, f, 0))],
        out_specs=pl.BlockSpec((cap * ROW_TILES, LANES), pair_block),
        scratch_shapes=[pltpu.VMEM((2, chunk, D), F32),
                        pltpu.VMEM((2, BATCH_GROUP, cap, D), BF16),
                        pltpu.VMEM((BATCH_GROUP, cap, D), F32),
                        pltpu.VMEM((D, tf), BF16),
                        pltpu.VMEM((D, tf), BF16),
                        pltpu.VMEM((tf, D), BF16),
                        pltpu.SemaphoreType.DMA((2,))],
    )
    return pl.pallas_call(
        functools.partial(_ffn_kernel, cap=cap, seq=seq, chunk=chunk),
        grid_spec=grid_spec,
        out_shape=jax.ShapeDtypeStruct((nb * ne * cap * ROW_TILES, LANES), F32),
        compiler_params=pltpu.CompilerParams(dimension_semantics=("arbitrary",) * 4, vmem_limit_bytes=VMEM_LIMIT),
        name="ffn",
    )(idx_flat, h2, gate_col, w_gate, w_up, w_down)


COMBINE_UNROLL = 8


def _combine_kernel(idx_ref, ye_ref, y_ref, *, cap):
    b = pl.program_id(0)
    e = pl.program_id(1)
    pair = b * pl.num_programs(1) + e

    @pl.when(e == 0)
    def _zero():
        y_ref[...] = jnp.zeros(y_ref.shape, F32)

    def group(gi, _):
        base = gi * COMBINE_UNROLL
        toks = [idx_ref[pair * cap + base + u] for u in range(COMBINE_UNROLL)]
        rows = [y_ref[toks[u]] + ye_ref[base + u] for u in range(COMBINE_UNROLL)]
        for u in range(COMBINE_UNROLL):
            y_ref[toks[u]] = rows[u]
        return 0

    lax.fori_loop(0, cap // COMBINE_UNROLL, group, 0)


def _combine(idx_flat, ye, nb, seq, cap):
    ne = ye.shape[0] // (nb * cap * ROW_TILES)
    ye4 = ye.reshape(nb * ne, cap, ROW_TILES, LANES)
    grid_spec = pltpu.PrefetchScalarGridSpec(
        num_scalar_prefetch=1,
        grid=(nb, ne),
        in_specs=[pl.BlockSpec((None, cap, ROW_TILES, LANES), lambda b, e, idx: (b * ne + e, 0, 0, 0))],
        out_specs=pl.BlockSpec((None, seq, ROW_TILES, LANES), lambda b, e, idx: (b, 0, 0, 0),
                               pipeline_mode=pl.Buffered(1)),
    )
    return pl.pallas_call(
        functools.partial(_combine_kernel, cap=cap),
        grid_spec=grid_spec,
        out_shape=jax.ShapeDtypeStruct((nb, seq, ROW_TILES, LANES), F32),
        compiler_params=pltpu.CompilerParams(dimension_semantics=("arbitrary",) * 2, vmem_limit_bytes=VMEM_LIMIT),
        name="combine",
    )(idx_flat, ye4)


def _final_kernel(y_ref, xn_ref, mod_ref, pn_ref, o_ref):
    tm = xn_ref.shape[0]
    g2 = mod_ref[0][5:6]
    ss = jnp.zeros((tm, LANES), F32)
    for j in range(ROW_TILES):
        v = y_ref[pl.ds(j, tm, stride=ROW_TILES), :]
        ss = ss + v * v
    rs = lax.rsqrt(jnp.sum(ss, axis=-1, keepdims=True) * (1.0 / D) + EPS)
    for j in range(ROW_TILES):
        cs = slice(j * LANES, (j + 1) * LANES)
        v = y_ref[pl.ds(j, tm, stride=ROW_TILES), :]
        o_ref[:, cs] = xn_ref[:, cs] + g2[:, cs] * (v * rs * pn_ref[:, cs])


def _final(y2_rows, xn, mod3, pn, rows_per_batch, tm):
    n = xn.shape[0]
    tpb = rows_per_batch // tm
    return pl.pallas_call(
        _final_kernel,
        grid=(n // tm,),
        in_specs=[pl.BlockSpec((tm * ROW_TILES, LANES), lambda i: (i, 0)),
                  pl.BlockSpec((tm, D), lambda i: (i, 0)),
                  pl.BlockSpec((1, N_MOD, D), lambda i: (i // tpb, 0, 0)),
                  _const_spec((1, D))],
        out_specs=pl.BlockSpec((tm, D), lambda i: (i, 0)),
        out_shape=jax.ShapeDtypeStruct((n, D), F32),
        compiler_params=pltpu.CompilerParams(dimension_semantics=("arbitrary",)),
        name="final",
    )(y2_rows, xn, mod3, pn)


def _rope_partner():
    q = QK_ROPE // 4
    return np.concatenate([np.arange(q, 2 * q), np.arange(0, q), np.arange(3 * q, 4 * q), np.arange(2 * q, 3 * q)])


def _rope_table(length):
    pos = np.arange(length)
    half = QK_ROPE // 2
    inv = (1.0 / (ROPE_THETA ** (np.arange(0, half, 2, dtype=np.float32) / half))).astype(np.float32)
    ar = (pos // GRID_W).astype(np.float32)[:, None] * inv
    ac = (pos % GRID_W).astype(np.float32)[:, None] * inv
    cos = np.concatenate([np.cos(ar), np.cos(ar), np.cos(ac), np.cos(ac)], axis=1)
    sin = np.concatenate([-np.sin(ar), np.sin(ar), -np.sin(ac), np.sin(ac)], axis=1)
    z = np.zeros((length, LANES - QK_ROPE), np.float32)
    return jnp.asarray(np.concatenate([cos, z, sin, z], axis=1).astype(np.float32))


def _identity_rope_table(length):
    t = np.zeros((length, 2 * LANES), np.float32)
    t[:, :QK_ROPE] = 1.0
    return jnp.asarray(t)


def _prep_w_in(w_in):
    perm = _rope_partner()
    kpe = w_in[:, 2 * Q_LORA:2 * Q_LORA + QK_ROPE]
    z = jnp.zeros((D, LANES - QK_ROPE), w_in.dtype)
    rest = w_in[:, 2 * Q_LORA + QK_ROPE:]
    return jnp.concatenate([w_in[:, :2 * Q_LORA], kpe, z, kpe[:, perm], z, rest], axis=1).astype(BF16)


def _prep_w_uq(w_uq):
    perm = _rope_partner()
    w = w_uq.reshape(Q_LORA, HEADS, QK_NOPE + QK_ROPE)
    z = jnp.zeros((Q_LORA, HEADS, LANES - QK_ROPE), w_uq.dtype)
    main = jnp.concatenate([w, z], axis=2).reshape(Q_LORA, HEADS * QK_PAD)
    partner = jnp.concatenate([w[:, :, QK_NOPE:][:, :, perm], z], axis=2).reshape(Q_LORA, HEADS * LANES)
    return jnp.concatenate([main, partner], axis=1).astype(BF16)


def _prep_w_ukv(w_ukv):
    w = w_ukv.reshape(KV_LORA, HEADS, 2, QK_NOPE)
    return w.transpose(0, 2, 1, 3).reshape(KV_LORA, 2 * HEADS * QK_NOPE).astype(BF16)


def kernel(x, c, ctx, c_ctx, w_ada, b_ada, pre_norm1, w_in, q_norm_w, w_uq, kv_norm_w, w_ukv, sgu_norm_w, sgu_w,
           sgu_b, w_out, post_norm1, pre_norm2, w_router, b_router, w_e_gate, w_e_up, w_e_down, post_norm2):
    nb, seq, _ = x.shape
    lc = ctx.shape[1]
    depth = w_ada.shape[0]
    assert depth == 1 and seq % 512 == 0 and lc % 128 == 0 and nb < SUBLANES
    cap = CAP_FACTOR * seq // N_EXPERTS
    n = nb * seq
    tm = 256

    cc = jnp.zeros((SUBLANES, D), F32).at[:nb].set(c).at[nb].set(c_ctx)
    mod3 = _ada(cc, w_ada[0], b_ada[0]).reshape(SUBLANES, N_MOD, D)

    row = lambda w: w.reshape(1, -1)
    win = _prep_w_in(w_in[0])
    wuq = _prep_w_uq(w_uq[0])
    wukv = _prep_w_ukv(w_ukv[0])
    x2 = x.reshape(n, D)
    q, k, v, sgu = _inproj(x2, mod3, row(pre_norm1[0]), win, row(q_norm_w[0]), wuq, row(kv_norm_w[0]), wukv,
                           row(sgu_norm_w[0]), sgu_w[0].astype(BF16), sgu_b[0].T, _rope_table(seq), seq, 2 * tm)
    kc, vc = _ctx_kv(ctx.reshape(nb * lc, D), mod3, nb, row(pre_norm1[0]), win, row(kv_norm_w[0]), wukv,
                     _identity_rope_table(lc), lc)

    attn = _attention(q.reshape(nb, seq, -1), k.reshape(nb, seq, -1), v.reshape(nb, seq, -1),
                      kc.reshape(nb, lc, -1), vc.reshape(nb, lc, -1), tq=512, tk=4096)

    wr = jnp.zeros((D, LANES), BF16).at[:, :N_EXPERTS].set(w_router[0].astype(BF16))
    br = jnp.zeros((1, LANES), F32).at[0, :N_EXPERTS].set(b_router[0])
    xn, h2_rows, aff = _outproj(attn.reshape(n, MLA_WIDTH), sgu, x2, mod3, w_out[0].astype(BF16),
                                row(post_norm1[0]), row(pre_norm2[0]), wr, br, seq, tm)

    aff_t = aff[:, :N_EXPERTS].reshape(nb, seq, N_EXPERTS).transpose(0, 2, 1)
    idx_col, gate_col = _route(aff_t.reshape(nb * N_EXPERTS, seq), cap)
    idx_flat = idx_col.reshape(-1)

    ye = _ffn(idx_flat, h2_rows, gate_col, w_e_gate[0], w_e_up[0], w_e_down[0], nb, seq, cap, tf=256)
    y2 = _combine(idx_flat, ye, nb, seq, cap)
    out = _final(y2.reshape(n * ROW_TILES, LANES), xn, mod3, row(post_norm2[0]), seq, tm)
    return out.reshape(nb, seq, D)
```

```python
import functools

import numpy as np
import jax
import jax.numpy as jnp
from jax import lax
from jax.experimental import pallas as pl
from jax.experimental.pallas import tpu as pltpu

F32 = jnp.float32
BF16 = jnp.bfloat16
I32 = jnp.int32

D = 2048
GRID_W = 64
EPS = 1e-6
N_MOD = 6
HEADS = 8
Q_LORA = 512
KV_LORA = 512
QK_NOPE = 128
QK_ROPE = 64
V_HEAD = 128
ROPE_THETA = 10000.0
ATTN_SCALE = (QK_NOPE + QK_ROPE) ** -0.5
Q_SCALE = ATTN_SCALE * float(np.log2(np.e))
SGU_HEADS = 8
SGU_DIM = 128
CHUNK = 128
MLA_WIDTH = HEADS * V_HEAD
SGU_WIDTH = SGU_HEADS * SGU_DIM
N_EXPERTS = 16
CAP_FACTOR = 2
EXPERT_FF = D

LANES = 128
SUBLANES = 8
ROW_TILES = D // LANES
QK_PAD = 256
VMEM_LIMIT = 56 * 1024 * 1024

C_Q = 0
C_KV = C_Q + Q_LORA
C_KPE = C_KV + KV_LORA
C_KPP = C_KPE + LANES
C_U = C_KPP + LANES
C_V = C_U + SGU_WIDTH
IN_EXT = C_V + SGU_WIDTH


def _rms(x, w):
    return x * lax.rsqrt(jnp.mean(x * x, axis=-1, keepdims=True) + EPS) * w


def _gelu_tanh(x):
    return 0.5 * x * (1.0 + jnp.tanh(np.sqrt(2.0 / np.pi).astype(np.float32) * (x + 0.044715 * (x * x * x))))


def _silu(x):
    return x * (1.0 / (1.0 + jnp.exp(-x)))


def _dot(a, b):
    return jnp.dot(a, b, preferred_element_type=F32)


def _ada_kernel(c_ref, w_ref, b_ref, o_ref):
    s = _silu(c_ref[...]).astype(BF16)
    o_ref[...] = _dot(s, w_ref[...].astype(BF16)) + b_ref[...]


def _ada(cc, w_ada, b_ada):
    n = w_ada.shape[1]
    tn = 1024
    return pl.pallas_call(
        _ada_kernel,
        grid=(n // tn,),
        in_specs=[pl.BlockSpec((SUBLANES, D), lambda j: (0, 0)),
                  pl.BlockSpec((D, tn), lambda j: (0, j)),
                  pl.BlockSpec((1, tn), lambda j: (0, j))],
        out_specs=pl.BlockSpec((SUBLANES, tn), lambda j: (0, j)),
        out_shape=jax.ShapeDtypeStruct((SUBLANES, n), F32),
        compiler_params=pltpu.CompilerParams(dimension_semantics=("arbitrary",), vmem_limit_bytes=VMEM_LIMIT),
        name="ada",
    )(cc, w_ada, b_ada.reshape(1, n))


def _kv_path(hb, win_ref, kvn_ref, wukv_ref, cos, sin, k_ref, v_ref):
    ckv = _dot(hb, win_ref[:, C_KV:C_KV + KV_LORA])
    ckvn = _rms(ckv, kvn_ref[...]).astype(BF16)
    kn = _dot(ckvn, wukv_ref[:, :HEADS * QK_NOPE])
    v_ref[...] = _dot(ckvn, wukv_ref[:, HEADS * QK_NOPE:]).astype(BF16)
    kp = _dot(hb, win_ref[:, C_KPE:C_KPE + 2 * LANES])
    kpe = (kp[:, :LANES] * cos + kp[:, LANES:] * sin).astype(BF16)
    for h in range(HEADS):
        k_ref[:, h * QK_PAD:h * QK_PAD + QK_NOPE] = kn[:, h * QK_NOPE:(h + 1) * QK_NOPE].astype(BF16)
        k_ref[:, h * QK_PAD + QK_NOPE:(h + 1) * QK_PAD] = kpe


def _inproj_kernel(x_ref, mod_ref, pn_ref, win_ref, qn_ref, wuq_ref, kvn_ref, wukv_ref,
                   sgn_ref, sgw_ref, sgb_ref, cs_ref, q_ref, k_ref, v_ref, sgu_ref):
    tm = x_ref.shape[0]
    mod = mod_ref[0]
    h = _rms(x_ref[...], pn_ref[...]) * (1.0 + mod[1:2]) + mod[0:1]
    hb = h.astype(BF16)
    cos = cs_ref[:, :LANES]
    sin = cs_ref[:, LANES:]

    _kv_path(hb, win_ref, kvn_ref, wukv_ref, cos, sin, k_ref, v_ref)

    cq = _dot(hb, win_ref[:, C_Q:C_Q + Q_LORA])
    cqn = _rms(cq, qn_ref[...]).astype(BF16)
    for hp in range(HEADS // 2):
        qm2 = _dot(cqn, wuq_ref[:, hp * 2 * QK_PAD:(hp + 1) * 2 * QK_PAD])
        qp2 = _dot(cqn, wuq_ref[:, HEADS * QK_PAD + hp * 2 * LANES:HEADS * QK_PAD + (hp + 1) * 2 * LANES])
        for i in range(2):
            h_ = 2 * hp + i
            qm = qm2[:, i * QK_PAD:(i + 1) * QK_PAD]
            qp = qp2[:, i * LANES:(i + 1) * LANES]
            q_ref[:, h_ * QK_PAD:h_ * QK_PAD + QK_NOPE] = (qm[:, :QK_NOPE] * Q_SCALE).astype(BF16)
            q_ref[:, h_ * QK_PAD + QK_NOPE:(h_ + 1) * QK_PAD] = (
                (qm[:, QK_NOPE:] * cos + qp * sin) * Q_SCALE).astype(BF16)

    for gp in range(SGU_HEADS // 2):
        u2 = _gelu_tanh(_dot(hb, win_ref[:, C_U + gp * 2 * SGU_DIM:C_U + (gp + 1) * 2 * SGU_DIM]))
        vs2 = _gelu_tanh(_dot(hb, win_ref[:, C_V + gp * 2 * SGU_DIM:C_V + (gp + 1) * 2 * SGU_DIM]))
        for i in range(2):
            g = 2 * gp + i
            gs = slice(g * SGU_DIM, (g + 1) * SGU_DIM)
            u = u2[:, i * SGU_DIM:(i + 1) * SGU_DIM]
            vs = vs2[:, i * SGU_DIM:(i + 1) * SGU_DIM]
            mu = jnp.mean(vs, axis=-1, keepdims=True)
            vc = vs - mu
            var = jnp.mean(vc * vc, axis=-1, keepdims=True)
            vn = (vc * lax.rsqrt(var + EPS) * sgn_ref[:, gs]).astype(BF16)
            wg = sgw_ref[g]
            bias = sgb_ref[:, g:g + 1]
            for n in range(tm // CHUNK):
                rs = slice(n * CHUNK, (n + 1) * CHUNK)
                mixed = _dot(wg, vn[rs, :]) + bias
                sgu_ref[rs, gs] = (u[rs, :] * mixed).astype(BF16)


def _ctx_kv_kernel(x_ref, mod_ref, pn_ref, win_ref, kvn_ref, wukv_ref, cs_ref, k_ref, v_ref):
    mod = mod_ref[0]
    h = _rms(x_ref[...], pn_ref[...]) * (1.0 + mod[1:2]) + mod[0:1]
    _kv_path(h.astype(BF16), win_ref, kvn_ref, wukv_ref, cs_ref[:, :LANES], cs_ref[:, LANES:], k_ref, v_ref)


def _const_spec(shape):
    nd = len(shape)
    return pl.BlockSpec(shape, lambda i: (0,) * nd, pipeline_mode=pl.Buffered(1))


def _inproj(x2, mod3, pn, win, qn, wuq, kvn, wukv, sgn, sgw, sgb, cs, rows_per_batch, tm):
    n = x2.shape[0]
    tpb = rows_per_batch // tm
    row = lambda w: pl.BlockSpec((tm, w), lambda i: (i, 0))
    return pl.pallas_call(
        _inproj_kernel,
        grid=(n // tm,),
        in_specs=[row(D),
                  pl.BlockSpec((1, N_MOD, D), lambda i: (i // tpb, 0, 0)),
                  _const_spec((1, D)), _const_spec(win.shape), _const_spec((1, Q_LORA)),
                  _const_spec(wuq.shape), _const_spec((1, KV_LORA)), _const_spec(wukv.shape),
                  _const_spec((1, SGU_WIDTH)), _const_spec(sgw.shape), _const_spec(sgb.shape),
                  pl.BlockSpec((tm, 2 * LANES), lambda i: (i % tpb, 0))],
        out_specs=[row(HEADS * QK_PAD), row(HEADS * QK_PAD), row(MLA_WIDTH), row(SGU_WIDTH)],
        out_shape=[jax.ShapeDtypeStruct((n, HEADS * QK_PAD), BF16),
                   jax.ShapeDtypeStruct((n, HEADS * QK_PAD), BF16),
                   jax.ShapeDtypeStruct((n, MLA_WIDTH), BF16),
                   jax.ShapeDtypeStruct((n, SGU_WIDTH), BF16)],
        compiler_params=pltpu.CompilerParams(dimension_semantics=("arbitrary",), vmem_limit_bytes=VMEM_LIMIT),
        name="inproj",
    )(x2, mod3, pn, win, qn, wuq, kvn, wukv, sgn, sgw, sgb, cs)


def _ctx_kv(c2, mod3, ctx_row, pn, win, kvn, wukv, cs, tm):
    n = c2.shape[0]
    row = lambda w: pl.BlockSpec((tm, w), lambda i: (i, 0))
    return pl.pallas_call(
        _ctx_kv_kernel,
        grid=(n // tm,),
        in_specs=[row(D),
                  pl.BlockSpec((1, N_MOD, D), lambda i: (ctx_row, 0, 0)),
                  _const_spec((1, D)), _const_spec(win.shape), _const_spec((1, KV_LORA)),
                  _const_spec(wukv.shape),
                  pl.BlockSpec((tm, 2 * LANES), lambda i: (0, 0))],
        out_specs=[row(HEADS * QK_PAD), row(MLA_WIDTH)],
        out_shape=[jax.ShapeDtypeStruct((n, HEADS * QK_PAD), BF16),
                   jax.ShapeDtypeStruct((n, MLA_WIDTH), BF16)],
        compiler_params=pltpu.CompilerParams(dimension_semantics=("arbitrary",), vmem_limit_bytes=VMEM_LIMIT),
        name="ctx_kv",
    )(c2, mod3, pn, win, kvn, wukv, cs)


def _attn_kernel(q_ref, k_ref, v_ref, kc_ref, vc_ref, o_ref, *, tk, streams):
    tq = q_ref.shape[1]
    ts = tq // streams
    nk = k_ref.shape[1] // tk
    qs = [q_ref[0, s * ts:(s + 1) * ts, :] for s in range(streams)]

    def step(q, kb, vb, carry):
        m, l, acc = carry
        s = lax.dot_general(q, kb, (((1,), (1,)), ((), ())), preferred_element_type=F32)
        m_new = jnp.maximum(m, jnp.max(s, axis=-1, keepdims=True))
        alpha = jnp.exp2(m - m_new)
        p = jnp.exp2(s - m_new)
        l = alpha * l + jnp.sum(p, axis=-1, keepdims=True)
        acc = alpha * acc + _dot(p.astype(BF16), vb)
        return m_new, l, acc

    def body(j, carries):
        off = pl.multiple_of(j * tk, tk)
        kb = k_ref[0, pl.ds(off, tk), :]
        vb = v_ref[0, pl.ds(off, tk), :]
        return tuple(step(qs[s], kb, vb, carries[s]) for s in range(streams))

    init = (jnp.full((ts, 1), -jnp.inf, F32), jnp.zeros((ts, 1), F32), jnp.zeros((ts, V_HEAD), F32))
    carries = (init,) * streams
    for j in range(nk):
        carries = body(j, carries)
    for s in range(streams):
        m, l, acc = step(qs[s], kc_ref[0], vc_ref[0], carries[s])
        o_ref[0, s * ts:(s + 1) * ts, :] = (acc / l).astype(BF16)


def _attention(q, k, v, kc, vc, tq, tk):
    b, l, _ = q.shape
    lc = kc.shape[1]
    tk = min(tk, l)
    assert l % tk == 0 and l % tq == 0
    return pl.pallas_call(
        functools.partial(_attn_kernel, tk=tk, streams=1),
        grid=(b, HEADS, l // tq),
        in_specs=[pl.BlockSpec((1, tq, QK_PAD), lambda b_, h, i: (b_, i, h)),
                  pl.BlockSpec((1, l, QK_PAD), lambda b_, h, i: (b_, 0, h)),
                  pl.BlockSpec((1, l, V_HEAD), lambda b_, h, i: (b_, 0, h)),
                  pl.BlockSpec((1, lc, QK_PAD), lambda b_, h, i: (b_, 0, h)),
                  pl.BlockSpec((1, lc, V_HEAD), lambda b_, h, i: (b_, 0, h))],
        out_specs=pl.BlockSpec((1, tq, V_HEAD), lambda b_, h, i: (b_, i, h)),
        out_shape=jax.ShapeDtypeStruct((b, l, MLA_WIDTH), BF16),
        compiler_params=pltpu.CompilerParams(dimension_semantics=("arbitrary",) * 3, vmem_limit_bytes=VMEM_LIMIT),
        name="attn",
    )(q, k, v, kc, vc)


def _outproj_kernel(at_ref, sg_ref, x_ref, mod_ref, wo_ref, pn1_ref, pn2_ref, wr_ref, br_ref,
                    xn_ref, h2_ref, aff_ref):
    tm = x_ref.shape[0]
    mod = mod_ref[0]
    y = _dot(at_ref[...], wo_ref[:MLA_WIDTH, :]) + _dot(sg_ref[...], wo_ref[MLA_WIDTH:, :])
    xn = x_ref[...] + mod[2:3] * _rms(y, pn1_ref[...])
    xn_ref[...] = xn
    h2 = _rms(xn, pn2_ref[...]) * (1.0 + mod[4:5]) + mod[3:4]
    h2_ref[...] = h2
    logits = _dot(h2.astype(BF16), wr_ref[...]) + br_ref[...]
    lane = lax.broadcasted_iota(I32, logits.shape, 1)
    logits = jnp.where(lane < N_EXPERTS, logits, -jnp.inf)
    e = jnp.exp(logits - jnp.max(logits, axis=-1, keepdims=True))
    aff_ref[...] = e / jnp.sum(e, axis=-1, keepdims=True)


def _outproj(attn2, sgu2, x2, mod3, wo, pn1, pn2, wr, br, rows_per_batch, tm):
    n = x2.shape[0]
    tpb = rows_per_batch // tm
    row = lambda w: pl.BlockSpec((tm, w), lambda i: (i, 0))
    return pl.pallas_call(
        _outproj_kernel,
        grid=(n // tm,),
        in_specs=[row(MLA_WIDTH), row(SGU_WIDTH), row(D),
                  pl.BlockSpec((1, N_MOD, D), lambda i: (i // tpb, 0, 0)),
                  _const_spec(wo.shape), _const_spec((1, D)), _const_spec((1, D)),
                  _const_spec(wr.shape), _const_spec((1, LANES))],
        out_specs=[row(D), row(D), row(LANES)],
        out_shape=[jax.ShapeDtypeStruct((n, D), F32),
                   jax.ShapeDtypeStruct((n, D), F32),
                   jax.ShapeDtypeStruct((n, LANES), F32)],
        compiler_params=pltpu.CompilerParams(dimension_semantics=("arbitrary",), vmem_limit_bytes=VMEM_LIMIT),
        name="outproj",
    )(attn2, sgu2, x2, mod3, wo, pn1, pn2, wr, br)


def _prefix_count(x01):
    r, t = x01.shape
    nb = t // LANES
    stacked = jnp.concatenate([x01[:, k * LANES:(k + 1) * LANES] for k in range(nb)], axis=0).astype(BF16)
    ii = lax.broadcasted_iota(I32, (LANES, LANES), 0)
    jj = lax.broadcasted_iota(I32, (LANES, LANES), 1)
    tri = jnp.where(ii <= jj, 1.0, 0.0).astype(BF16)
    within = _dot(stacked, tri)
    off = jnp.zeros((r, 1), F32)
    blocks = []
    for k in range(nb):
        w = within[k * r:(k + 1) * r, :]
        blocks.append(w + off)
        off = off + w[:, LANES - 1:LANES]
    return jnp.concatenate(blocks, axis=1)


ROUTE_SLOTS = 128
ROUTE_SEARCH_STEPS = 192


def _route_kernel(aff_ref, idx_ref, gate_ref, incl_s, isel_s, aff_s, *, cap):
    aff = aff_ref[...]
    nr, lt = aff.shape
    capf = float(cap)

    def search(i, c):
        lo, hi = c
        mid = 0.5 * (lo + hi)
        ge = jnp.sum(jnp.where(aff >= mid, 1.0, 0.0), axis=1, keepdims=True) >= capf
        return jnp.where(ge, mid, lo), jnp.where(ge, hi, mid)

    lo, _ = lax.fori_loop(0, ROUTE_SEARCH_STEPS, search,
                          (jnp.zeros((nr, 1), F32), jnp.full((nr, 1), 2.0, F32)))
    thr = jnp.min(jnp.where(aff >= lo, aff, jnp.inf), axis=1, keepdims=True)
    gt = aff > thr
    eq = aff == thr
    n_gt = jnp.sum(jnp.where(gt, 1.0, 0.0), axis=1, keepdims=True)
    eq_rank = _prefix_count(jnp.where(eq, 1.0, 0.0))
    sel = jnp.where(gt, 1.0, jnp.where(eq, jnp.where(eq_rank <= capf - n_gt, 1.0, 0.0), 0.0))
    incl = _prefix_count(sel)
    isel = incl * sel
    for r in range(nr):
        incl_s[r] = incl[r:r + 1, :]
        isel_s[r] = isel[r:r + 1, :]
        aff_s[r] = aff[r:r + 1, :]

    ns = min(cap, ROUTE_SLOTS)

    def compact(r, _):
        for c0 in range(0, cap, ns):
            slot = (lax.broadcasted_iota(I32, (ns, LANES), 0) + c0).astype(F32)
            cnt = jnp.zeros((ns, LANES), F32)
            gat = jnp.zeros((ns, LANES), F32)
            for k in range(lt // LANES):
                ks = slice(k * LANES, (k + 1) * LANES)
                cnt = cnt + jnp.where(incl_s[r, :, ks] <= slot, 1.0, 0.0)
                gat = gat + jnp.where(isel_s[r, :, ks] == slot + 1.0, aff_s[r, :, ks], 0.0)
            idx_ref[r, c0:c0 + ns, :] = jnp.sum(cnt, axis=1, keepdims=True).astype(I32)
            gate_ref[r, c0:c0 + ns, :] = jnp.sum(gat, axis=1, keepdims=True)
        return 0

    lax.fori_loop(0, nr, compact, 0)


def _route(aff_rows, cap):
    nr, lt = aff_rows.shape
    return pl.pallas_call(
        functools.partial(_route_kernel, cap=cap),
        out_shape=[jax.ShapeDtypeStruct((nr, cap, 1), I32),
                   jax.ShapeDtypeStruct((nr, cap, 1), F32)],
        scratch_shapes=[pltpu.VMEM((nr, 1, lt), F32)] * 3,
        compiler_params=pltpu.CompilerParams(vmem_limit_bytes=VMEM_LIMIT),
        name="route",
    )(aff_rows)


BATCH_GROUP = 4


def _ffn_kernel(idx_ref, h2_ref, gate_ref, wg_ref, wu_ref, wd_ref, o_ref, ring, lhs, acc, wgs, wus, wds,
                wgb, wub, wdb, sems, wsem, *, cap, seq, chunk):
    p = pl.program_id(0)
    e = pl.program_id(1)
    f = pl.program_id(2)
    bb = pl.program_id(3)
    ne = pl.num_programs(1)
    nf = pl.num_programs(2)
    steps = nf * BATCH_GROUP
    cpb = cap // chunk
    n_items = pl.num_programs(0) * ne
    w = p * ne + e
    s = f * BATCH_GROUP + bb
    g = w * steps + s

    def issue_chunk(item, k, slot):
        bi = lax.div(item, ne) * BATCH_GROUP + lax.div(k, cpb)
        base = (bi * ne + lax.rem(item, ne)) * cap + lax.rem(k, cpb) * chunk
        for i in range(chunk):
            tok = bi * seq + idx_ref[base + i]
            pltpu.make_async_copy(h2_ref.at[pl.ds(tok, 1)], ring.at[slot, pl.ds(i, 1)], sems.at[slot]).start()

    def wait_chunk(slot):
        pltpu.make_async_copy(h2_ref.at[pl.ds(0, chunk)], ring.at[slot], sems.at[slot]).wait()

    def drain_chunk(slot, par, k):
        wait_chunk(slot)
        r0 = pl.multiple_of(lax.rem(k, cpb) * chunk, chunk)
        lhs[par, lax.div(k, cpb), pl.ds(r0, chunk), :] = ring[slot].astype(BF16)

    tf = wgb.shape[1]

    def weight_copies(tile):
        ei = lax.rem(lax.div(tile, nf), ne)
        c0 = pl.multiple_of(lax.rem(tile, nf) * tf, tf)
        return (pltpu.make_async_copy(wg_ref.at[ei, :, pl.ds(c0, tf)], wgs, wsem.at[0]),
                pltpu.make_async_copy(wu_ref.at[ei, :, pl.ds(c0, tf)], wus, wsem.at[1]),
                pltpu.make_async_copy(wd_ref.at[ei, pl.ds(c0, tf), :], wds, wsem.at[2]))

    @pl.when(g == 0)
    def _prologue():
        for cp in weight_copies(0):
            cp.start()

        def fetch(k, _):
            issue_chunk(0, k, 0)
            drain_chunk(0, 0, k)
            return 0
        lax.fori_loop(0, steps - 1, fetch, 0)
        issue_chunk(0, steps - 1, 1)

    gq = g - 1 + steps
    drain_chunk(lax.rem(g + 1, 2), lax.rem(lax.div(gq, steps), 2), lax.rem(gq, steps))

    @pl.when(bb == 0)
    def _next_weights():
        tile = w * nf + f
        for cp in weight_copies(tile):
            cp.wait()
        wgb[...] = wgs[...].astype(BF16)
        wub[...] = wus[...].astype(BF16)
        wdb[...] = wds[...].astype(BF16)

        @pl.when(tile + 1 < n_items * nf)
        def _prefetch():
            for cp in weight_copies(tile + 1):
                cp.start()

    @pl.when(f == 0)
    def _zero():
        acc[bb] = jnp.zeros((cap, D), F32)

    issue_chunk(jnp.minimum(w + 1, n_items - 1), s, lax.rem(g, 2))
    x = lhs[lax.rem(w, 2), bb]
    a = _dot(x, wgb[...])
    gg = _dot(x, wub[...])
    hm = (_silu(a) * gg).astype(BF16)
    acc[bb] += _dot(hm, wdb[...])

    @pl.when(f == nf - 1)
    def _emit():
        gate = gate_ref[0]
        for j in range(ROW_TILES):
            o_ref[pl.ds(j, cap, stride=ROW_TILES), :] = acc[bb, :, j * LANES:(j + 1) * LANES] * gate

    @pl.when(g == n_items * steps - 1)
    def _tail():
        wait_chunk(lax.rem(g, 2))


def _ffn(idx_flat, h2, gate_col, w_gate, w_up, w_down, nb, seq, cap, tf):
    ne, _, ff = w_gate.shape
    nf = ff // tf
    chunk = cap // nf
    assert nb % BATCH_GROUP == 0 and cap % nf == 0 and chunk % 16 == 0
    pair_block = lambda p, e, f, bb, idx: ((p * BATCH_GROUP + jnp.where(f == nf - 1, bb, 0)) * ne + e, 0)
    grid_spec = pltpu.PrefetchScalarGridSpec(
        num_scalar_prefetch=1,
        grid=(nb // BATCH_GROUP, ne, nf, BATCH_GROUP),
        in_specs=[pl.BlockSpec(memory_space=pl.ANY),
                  pl.BlockSpec((1, cap, 1), lambda p, e, f, bb, idx: pair_block(p, e, f, bb, idx) + (0,)),
                  pl.BlockSpec(memory_space=pl.ANY),
                  pl.BlockSpec(memory_space=pl.ANY),
                  pl.BlockSpec(memory_space=pl.ANY)],
        out_specs=pl.BlockSpec((cap * ROW_TILES, LANES), pair_block),
        scratch_shapes=[pltpu.VMEM((2, chunk, D), F32),
                        pltpu.VMEM((2, BATCH_GROUP, cap, D), BF16),
                        pltpu.VMEM((BATCH_GROUP, cap, D), F32),
                        pltpu.VMEM((D, tf), F32),
                        pltpu.VMEM((D, tf), F32),
                        pltpu.VMEM((tf, D), F32),
                        pltpu.VMEM((D, tf), BF16),
                        pltpu.VMEM((D, tf), BF16),
                        pltpu.VMEM((tf, D), BF16),
                        pltpu.SemaphoreType.DMA((2,)),
                        pltpu.SemaphoreType.DMA((3,))],
    )
    return pl.pallas_call(
        functools.partial(_ffn_kernel, cap=cap, seq=seq, chunk=chunk),
        grid_spec=grid_spec,
        out_shape=jax.ShapeDtypeStruct((nb * ne * cap * ROW_TILES, LANES), F32),
        compiler_params=pltpu.CompilerParams(dimension_semantics=("arbitrary",) * 4, vmem_limit_bytes=VMEM_LIMIT),
        name="ffn",
    )(idx_flat, h2, gate_col, w_gate, w_up, w_down)


COMBINE_UNROLL = 8


def _combine_kernel(idx_ref, ye_ref, y_ref, *, cap):
    b = pl.program_id(0)
    e = pl.program_id(1)
    pair = b * pl.num_programs(1) + e

    @pl.when(e == 0)
    def _zero():
        y_ref[...] = jnp.zeros(y_ref.shape, F32)

    def group(gi, _):
        base = gi * COMBINE_UNROLL
        toks = [idx_ref[pair * cap + base + u] for u in range(COMBINE_UNROLL)]
        rows = [y_ref[toks[u]] + ye_ref[base + u] for u in range(COMBINE_UNROLL)]
        for u in range(COMBINE_UNROLL):
            y_ref[toks[u]] = rows[u]
        return 0

    lax.fori_loop(0, cap // COMBINE_UNROLL, group, 0)


def _combine(idx_flat, ye, nb, seq, cap):
    ne = ye.shape[0] // (nb * cap * ROW_TILES)
    ye4 = ye.reshape(nb * ne, cap, ROW_TILES, LANES)
    grid_spec = pltpu.PrefetchScalarGridSpec(
        num_scalar_prefetch=1,
        grid=(nb, ne),
        in_specs=[pl.BlockSpec((None, cap, ROW_TILES, LANES), lambda b, e, idx: (b * ne + e, 0, 0, 0))],
        out_specs=pl.BlockSpec((None, seq, ROW_TILES, LANES), lambda b, e, idx: (b, 0, 0, 0),
                               pipeline_mode=pl.Buffered(1)),
    )
    return pl.pallas_call(
        functools.partial(_combine_kernel, cap=cap),
        grid_spec=grid_spec,
        out_shape=jax.ShapeDtypeStruct((nb, seq, ROW_TILES, LANES), F32),
        compiler_params=pltpu.CompilerParams(dimension_semantics=("arbitrary",) * 2, vmem_limit_bytes=VMEM_LIMIT),
        name="combine",
    )(idx_flat, ye4)


def _final_kernel(y_ref, xn_ref, mod_ref, pn_ref, o_ref):
    tm = xn_ref.shape[0]
    g2 = mod_ref[0][5:6]
    ss = jnp.zeros((tm, LANES), F32)
    for j in range(ROW_TILES):
        v = y_ref[pl.ds(j, tm, stride=ROW_TILES), :]
        ss = ss + v * v
    rs = lax.rsqrt(jnp.sum(ss, axis=-1, keepdims=True) * (1.0 / D) + EPS)
    for j in range(ROW_TILES):
        cs = slice(j * LANES, (j + 1) * LANES)
        v = y_ref[pl.ds(j, tm, stride=ROW_TILES), :]
        o_ref[:, cs] = xn_ref[:, cs] + g2[:, cs] * (v * rs * pn_ref[:, cs])


def _final(y2_rows, xn, mod3, pn, rows_per_batch, tm):
    n = xn.shape[0]
    tpb = rows_per_batch // tm
    return pl.pallas_call(
        _final_kernel,
        grid=(n // tm,),
        in_specs=[pl.BlockSpec((tm * ROW_TILES, LANES), lambda i: (i, 0)),
                  pl.BlockSpec((tm, D), lambda i: (i, 0)),
                  pl.BlockSpec((1, N_MOD, D), lambda i: (i // tpb, 0, 0)),
                  _const_spec((1, D))],
        out_specs=pl.BlockSpec((tm, D), lambda i: (i, 0)),
        out_shape=jax.ShapeDtypeStruct((n, D), F32),
        compiler_params=pltpu.CompilerParams(dimension_semantics=("arbitrary",)),
        name="final",
    )(y2_rows, xn, mod3, pn)


def _rope_partner():
    q = QK_ROPE // 4
    return np.concatenate([np.arange(q, 2 * q), np.arange(0, q), np.arange(3 * q, 4 * q), np.arange(2 * q, 3 * q)])


def _rope_table(length):
    pos = np.arange(length)
    half = QK_ROPE // 2
    inv = (1.0 / (ROPE_THETA ** (np.arange(0, half, 2, dtype=np.float32) / half))).astype(np.float32)
    ar = (pos // GRID_W).astype(np.float32)[:, None] * inv
    ac = (pos % GRID_W).astype(np.float32)[:, None] * inv
    cos = np.concatenate([np.cos(ar), np.cos(ar), np.cos(ac), np.cos(ac)], axis=1)
    sin = np.concatenate([-np.sin(ar), np.sin(ar), -np.sin(ac), np.sin(ac)], axis=1)
    z = np.zeros((length, LANES - QK_ROPE), np.float32)
    return jnp.asarray(np.concatenate([cos, z, sin, z], axis=1).astype(np.float32))


def _identity_rope_table(length):
    t = np.zeros((length, 2 * LANES), np.float32)
    t[:, :QK_ROPE] = 1.0
    return jnp.asarray(t)


def _prep_w_in(w_in):
    perm = _rope_partner()
    kpe = w_in[:, 2 * Q_LORA:2 * Q_LORA + QK_ROPE]
    z = jnp.zeros((D, LANES - QK_ROPE), w_in.dtype)
    rest = w_in[:, 2 * Q_LORA + QK_ROPE:]
    return jnp.concatenate([w_in[:, :2 * Q_LORA], kpe, z, kpe[:, perm], z, rest], axis=1).astype(BF16)


def _prep_w_uq(w_uq):
    perm = _rope_partner()
    w = w_uq.reshape(Q_LORA, HEADS, QK_NOPE + QK_ROPE)
    z = jnp.zeros((Q_LORA, HEADS, LANES - QK_ROPE), w_uq.dtype)
    main = jnp.concatenate([w, z], axis=2).reshape(Q_LORA, HEADS * QK_PAD)
    partner = jnp.concatenate([w[:, :, QK_NOPE:][:, :, perm], z], axis=2).reshape(Q_LORA, HEADS * LANES)
    return jnp.concatenate([main, partner], axis=1).astype(BF16)


def _prep_w_ukv(w_ukv):
    w = w_ukv.reshape(KV_LORA, HEADS, 2, QK_NOPE)
    return w.transpose(0, 2, 1, 3).reshape(KV_LORA, 2 * HEADS * QK_NOPE).astype(BF16)


def kernel(x, c, ctx, c_ctx, w_ada, b_ada, pre_norm1, w_in, q_norm_w, w_uq, kv_norm_w, w_ukv, sgu_norm_w, sgu_w,
           sgu_b, w_out, post_norm1, pre_norm2, w_router, b_router, w_e_gate, w_e_up, w_e_down, post_norm2):
    nb, seq, _ = x.shape
    lc = ctx.shape[1]
    depth = w_ada.shape[0]
    assert depth == 1 and seq % 512 == 0 and lc % 128 == 0 and nb < SUBLANES
    cap = CAP_FACTOR * seq // N_EXPERTS
    n = nb * seq
    tm = 256

    cc = jnp.zeros((SUBLANES, D), F32).at[:nb].set(c).at[nb].set(c_ctx)
    mod3 = _ada(cc, w_ada[0], b_ada[0]).reshape(SUBLANES, N_MOD, D)

    row = lambda w: w.reshape(1, -1)
    win = _prep_w_in(w_in[0])
    wuq = _prep_w_uq(w_uq[0])
    wukv = _prep_w_ukv(w_ukv[0])
    x2 = x.reshape(n, D)
    q, k, v, sgu = _inproj(x2, mod3, row(pre_norm1[0]), win, row(q_norm_w[0]), wuq, row(kv_norm_w[0]), wukv,
                           row(sgu_norm_w[0]), sgu_w[0].astype(BF16), sgu_b[0].T, _rope_table(seq), seq, 2 * tm)
    kc, vc = _ctx_kv(ctx.reshape(nb * lc, D), mod3, nb, row(pre_norm1[0]), win, row(kv_norm_w[0]), wukv,
                     _identity_rope_table(lc), lc)

    attn = _attention(q.reshape(nb, seq, -1), k.reshape(nb, seq, -1), v.reshape(nb, seq, -1),
                      kc.reshape(nb, lc, -1), vc.reshape(nb, lc, -1), tq=512, tk=4096)

    wr = jnp.zeros((D, LANES), BF16).at[:, :N_EXPERTS].set(w_router[0].astype(BF16))
    br = jnp.zeros((1, LANES), F32).at[0, :N_EXPERTS].set(b_router[0])
    xn, h2_rows, aff = _outproj(attn.reshape(n, MLA_WIDTH), sgu, x2, mod3, w_out[0].astype(BF16),
                                row(post_norm1[0]), row(pre_norm2[0]), wr, br, seq, tm)

    aff_t = aff[:, :N_EXPERTS].reshape(nb, seq, N_EXPERTS).transpose(0, 2, 1)
    idx_col, gate_col = _route(aff_t.reshape(nb * N_EXPERTS, seq), cap)
    idx_flat = idx_col.reshape(-1)

    ye = _ffn(idx_flat, h2_rows, gate_col, w_e_gate[0], w_e_up[0], w_e_down[0], nb, seq, cap, tf=256)
    y2 = _combine(idx_flat, ye, nb, seq, cap)
    out = _final(y2.reshape(n * ROW_TILES, LANES), xn, mod3, row(post_norm2[0]), seq, tm)
    return out.reshape(nb, seq, D)
```

```python
import functools

import numpy as np
import jax
import jax.numpy as jnp
from jax import lax
from jax.experimental import pallas as pl
from jax.experimental.pallas import tpu as pltpu

F32 = jnp.float32
BF16 = jnp.bfloat16
I32 = jnp.int32

D = 2048
GRID_W = 64
EPS = 1e-6
N_MOD = 6
HEADS = 8
Q_LORA = 512
KV_LORA = 512
QK_NOPE = 128
QK_ROPE = 64
V_HEAD = 128
ROPE_THETA = 10000.0
ATTN_SCALE = (QK_NOPE + QK_ROPE) ** -0.5
Q_SCALE = ATTN_SCALE * float(np.log2(np.e))
SGU_HEADS = 8
SGU_DIM = 128
CHUNK = 128
MLA_WIDTH = HEADS * V_HEAD
SGU_WIDTH = SGU_HEADS * SGU_DIM
N_EXPERTS = 16
CAP_FACTOR = 2
EXPERT_FF = D

LANES = 128
SUBLANES = 8
ROW_TILES = D // LANES
QK_PAD = 256
VMEM_LIMIT = 56 * 1024 * 1024

C_Q = 0
C_KV = C_Q + Q_LORA
C_KPE = C_KV + KV_LORA
C_KPP = C_KPE + LANES
C_U = C_KPP + LANES
C_V = C_U + SGU_WIDTH
IN_EXT = C_V + SGU_WIDTH


def _rms(x, w):
    return x * lax.rsqrt(jnp.mean(x * x, axis=-1, keepdims=True) + EPS) * w


def _gelu_tanh(x):
    return 0.5 * x * (1.0 + jnp.tanh(np.sqrt(2.0 / np.pi).astype(np.float32) * (x + 0.044715 * (x * x * x))))


def _silu(x):
    return x * (1.0 / (1.0 + jnp.exp(-x)))


def _dot(a, b):
    return jnp.dot(a, b, preferred_element_type=F32)


def _ada_kernel(c_ref, w_ref, b_ref, o_ref):
    s = _silu(c_ref[...]).astype(BF16)
    o_ref[...] = _dot(s, w_ref[...].astype(BF16)) + b_ref[...]


def _ada(cc, w_ada, b_ada):
    n = w_ada.shape[1]
    tn = 1024
    return pl.pallas_call(
        _ada_kernel,
        grid=(n // tn,),
        in_specs=[pl.BlockSpec((SUBLANES, D), lambda j: (0, 0)),
                  pl.BlockSpec((D, tn), lambda j: (0, j)),
                  pl.BlockSpec((1, tn), lambda j: (0, j))],
        out_specs=pl.BlockSpec((SUBLANES, tn), lambda j: (0, j)),
        out_shape=jax.ShapeDtypeStruct((SUBLANES, n), F32),
        compiler_params=pltpu.CompilerParams(dimension_semantics=("arbitrary",), vmem_limit_bytes=VMEM_LIMIT),
        name="ada",
    )(cc, w_ada, b_ada.reshape(1, n))


def _kv_path(hb, win_ref, kvn_ref, wukv_ref, cos, sin, k_ref, v_ref):
    ckv = _dot(hb, win_ref[:, C_KV:C_KV + KV_LORA])
    ckvn = _rms(ckv, kvn_ref[...]).astype(BF16)
    kn = _dot(ckvn, wukv_ref[:, :HEADS * QK_NOPE])
    v_ref[...] = _dot(ckvn, wukv_ref[:, HEADS * QK_NOPE:]).astype(BF16)
    kp = _dot(hb, win_ref[:, C_KPE:C_KPE + 2 * LANES])
    kpe = (kp[:, :LANES] * cos + kp[:, LANES:] * sin).astype(BF16)
    for h in range(HEADS):
        k_ref[:, h * QK_PAD:h * QK_PAD + QK_NOPE] = kn[:, h * QK_NOPE:(h + 1) * QK_NOPE].astype(BF16)
        k_ref[:, h * QK_PAD + QK_NOPE:(h + 1) * QK_PAD] = kpe


def _inproj_kernel(x_ref, mod_ref, pn_ref, win_ref, qn_ref, wuq_ref, kvn_ref, wukv_ref,
                   sgn_ref, sgw_ref, sgb_ref, cs_ref, q_ref, k_ref, v_ref, sgu_ref):
    tm = x_ref.shape[0]
    mod = mod_ref[0]
    h = _rms(x_ref[...], pn_ref[...]) * (1.0 + mod[1:2]) + mod[0:1]
    hb = h.astype(BF16)
    cos = cs_ref[:, :LANES]
    sin = cs_ref[:, LANES:]

    _kv_path(hb, win_ref, kvn_ref, wukv_ref, cos, sin, k_ref, v_ref)

    cq = _dot(hb, win_ref[:, C_Q:C_Q + Q_LORA])
    cqn = _rms(cq, qn_ref[...]).astype(BF16)
    for hp in range(HEADS // 2):
        qm2 = _dot(cqn, wuq_ref[:, hp * 2 * QK_PAD:(hp + 1) * 2 * QK_PAD])
        qp2 = _dot(cqn, wuq_ref[:, HEADS * QK_PAD + hp * 2 * LANES:HEADS * QK_PAD + (hp + 1) * 2 * LANES])
        for i in range(2):
            h_ = 2 * hp + i
            qm = qm2[:, i * QK_PAD:(i + 1) * QK_PAD]
            qp = qp2[:, i * LANES:(i + 1) * LANES]
            q_ref[:, h_ * QK_PAD:h_ * QK_PAD + QK_NOPE] = (qm[:, :QK_NOPE] * Q_SCALE).astype(BF16)
            q_ref[:, h_ * QK_PAD + QK_NOPE:(h_ + 1) * QK_PAD] = (
                (qm[:, QK_NOPE:] * cos + qp * sin) * Q_SCALE).astype(BF16)

    for gp in range(SGU_HEADS // 2):
        u2 = _gelu_tanh(_dot(hb, win_ref[:, C_U + gp * 2 * SGU_DIM:C_U + (gp + 1) * 2 * SGU_DIM]))
        vs2 = _gelu_tanh(_dot(hb, win_ref[:, C_V + gp * 2 * SGU_DIM:C_V + (gp + 1) * 2 * SGU_DIM]))
        for i in range(2):
            g = 2 * gp + i
            gs = slice(g * SGU_DIM, (g + 1) * SGU_DIM)
            u = u2[:, i * SGU_DIM:(i + 1) * SGU_DIM]
            vs = vs2[:, i * SGU_DIM:(i + 1) * SGU_DIM]
            mu = jnp.mean(vs, axis=-1, keepdims=True)
            vc = vs - mu
            var = jnp.mean(vc * vc, axis=-1, keepdims=True)
            vn = (vc * lax.rsqrt(var + EPS) * sgn_ref[:, gs]).astype(BF16)
            wg = sgw_ref[g]
            bias = sgb_ref[:, g:g + 1]
            for n in range(tm // CHUNK):
                rs = slice(n * CHUNK, (n + 1) * CHUNK)
                mixed = _dot(wg, vn[rs, :]) + bias
                sgu_ref[rs, gs] = (u[rs, :] * mixed).astype(BF16)


def _ctx_kv_kernel(x_ref, mod_ref, pn_ref, win_ref, kvn_ref, wukv_ref, cs_ref, k_ref, v_ref):
    mod = mod_ref[0]
    h = _rms(x_ref[...], pn_ref[...]) * (1.0 + mod[1:2]) + mod[0:1]
    _kv_path(h.astype(BF16), win_ref, kvn_ref, wukv_ref, cs_ref[:, :LANES], cs_ref[:, LANES:], k_ref, v_ref)


def _const_spec(shape):
    nd = len(shape)
    return pl.BlockSpec(shape, lambda i: (0,) * nd, pipeline_mode=pl.Buffered(1))


def _inproj(x2, mod3, pn, win, qn, wuq, kvn, wukv, sgn, sgw, sgb, cs, rows_per_batch, tm):
    n = x2.shape[0]
    tpb = rows_per_batch // tm
    row = lambda w: pl.BlockSpec((tm, w), lambda i: (i, 0))
    return pl.pallas_call(
        _inproj_kernel,
        grid=(n // tm,),
        in_specs=[row(D),
                  pl.BlockSpec((1, N_MOD, D), lambda i: (i // tpb, 0, 0)),
                  _const_spec((1, D)), _const_spec(win.shape), _const_spec((1, Q_LORA)),
                  _const_spec(wuq.shape), _const_spec((1, KV_LORA)), _const_spec(wukv.shape),
                  _const_spec((1, SGU_WIDTH)), _const_spec(sgw.shape), _const_spec(sgb.shape),
                  pl.BlockSpec((tm, 2 * LANES), lambda i: (i % tpb, 0))],
        out_specs=[row(HEADS * QK_PAD), row(HEADS * QK_PAD), row(MLA_WIDTH), row(SGU_WIDTH)],
        out_shape=[jax.ShapeDtypeStruct((n, HEADS * QK_PAD), BF16),
                   jax.ShapeDtypeStruct((n, HEADS * QK_PAD), BF16),
                   jax.ShapeDtypeStruct((n, MLA_WIDTH), BF16),
                   jax.ShapeDtypeStruct((n, SGU_WIDTH), BF16)],
        compiler_params=pltpu.CompilerParams(dimension_semantics=("arbitrary",), vmem_limit_bytes=VMEM_LIMIT),
        name="inproj",
    )(x2, mod3, pn, win, qn, wuq, kvn, wukv, sgn, sgw, sgb, cs)


def _ctx_kv(c2, mod3, ctx_row, pn, win, kvn, wukv, cs, tm):
    n = c2.shape[0]
    row = lambda w: pl.BlockSpec((tm, w), lambda i: (i, 0))
    return pl.pallas_call(
        _ctx_kv_kernel,
        grid=(n // tm,),
        in_specs=[row(D),
                  pl.BlockSpec((1, N_MOD, D), lambda i: (ctx_row, 0, 0)),
                  _const_spec((1, D)), _const_spec(win.shape), _const_spec((1, KV_LORA)),
                  _const_spec(wukv.shape),
                  pl.BlockSpec((tm, 2 * LANES), lambda i: (0, 0))],
        out_specs=[row(HEADS * QK_PAD), row(MLA_WIDTH)],
        out_shape=[jax.ShapeDtypeStruct((n, HEADS * QK_PAD), BF16),
                   jax.ShapeDtypeStruct((n, MLA_WIDTH), BF16)],
        compiler_params=pltpu.CompilerParams(dimension_semantics=("arbitrary",), vmem_limit_bytes=VMEM_LIMIT),
        name="ctx_kv",
    )(c2, mod3, pn, win, kvn, wukv, cs)


def _attn_kernel(q_ref, k_ref, v_ref, kc_ref, vc_ref, o_ref, *, tk, streams):
    tq = q_ref.shape[1]
    ts = tq // streams
    nk = k_ref.shape[1] // tk
    qs = [q_ref[0, s * ts:(s + 1) * ts, :] for s in range(streams)]

    def step(q, kb, vb, carry):
        m, l, acc = carry
        s = lax.dot_general(q, kb, (((1,), (1,)), ((), ())), preferred_element_type=F32)
        m_new = jnp.maximum(m, jnp.max(s, axis=-1, keepdims=True))
        alpha = jnp.exp2(m - m_new)
        p = jnp.exp2(s - m_new)
        l = alpha * l + jnp.sum(p, axis=-1, keepdims=True)
        acc = alpha * acc + _dot(p.astype(BF16), vb)
        return m_new, l, acc

    def body(j, carries):
        off = pl.multiple_of(j * tk, tk)
        kb = k_ref[0, pl.ds(off, tk), :]
        vb = v_ref[0, pl.ds(off, tk), :]
        return tuple(step(qs[s], kb, vb, carries[s]) for s in range(streams))

    init = (jnp.full((ts, 1), -jnp.inf, F32), jnp.zeros((ts, 1), F32), jnp.zeros((ts, V_HEAD), F32))
    carries = (init,) * streams
    for j in range(nk):
        carries = body(j, carries)
    for s in range(streams):
        m, l, acc = step(qs[s], kc_ref[0], vc_ref[0], carries[s])
        o_ref[0, s * ts:(s + 1) * ts, :] = (acc / l).astype(BF16)


def _attention(q, k, v, kc, vc, tq, tk):
    b, l, _ = q.shape
    lc = kc.shape[1]
    tk = min(tk, l)
    assert l % tk == 0 and l % tq == 0
    return pl.pallas_call(
        functools.partial(_attn_kernel, tk=tk, streams=1),
        grid=(b, HEADS, l // tq),
        in_specs=[pl.BlockSpec((1, tq, QK_PAD), lambda b_, h, i: (b_, i, h)),
                  pl.BlockSpec((1, l, QK_PAD), lambda b_, h, i: (b_, 0, h)),
                  pl.BlockSpec((1, l, V_HEAD), lambda b_, h, i: (b_, 0, h)),
                  pl.BlockSpec((1, lc, QK_PAD), lambda b_, h, i: (b_, 0, h)),
                  pl.BlockSpec((1, lc, V_HEAD), lambda b_, h, i: (b_, 0, h))],
        out_specs=pl.BlockSpec((1, tq, V_HEAD), lambda b_, h, i: (b_, i, h)),
        out_shape=jax.ShapeDtypeStruct((b, l, MLA_WIDTH), BF16),
        compiler_params=pltpu.CompilerParams(dimension_semantics=("arbitrary",) * 3, vmem_limit_bytes=VMEM_LIMIT),
        name="attn",
    )(q, k, v, kc, vc)


def _outproj_kernel(at_ref, sg_ref, x_ref, mod_ref, wo_ref, pn1_ref, pn2_ref, wr_ref, br_ref,
                    xn_ref, h2_ref, aff_ref):
    tm = x_ref.shape[0]
    mod = mod_ref[0]
    y = _dot(at_ref[...], wo_ref[:MLA_WIDTH, :]) + _dot(sg_ref[...], wo_ref[MLA_WIDTH:, :])
    xn = x_ref[...] + mod[2:3] * _rms(y, pn1_ref[...])
    xn_ref[...] = xn
    h2 = _rms(xn, pn2_ref[...]) * (1.0 + mod[4:5]) + mod[3:4]
    h2_ref[...] = h2
    logits = _dot(h2.astype(BF16), wr_ref[...]) + br_ref[...]
    lane = lax.broadcasted_iota(I32, logits.shape, 1)
    logits = jnp.where(lane < N_EXPERTS, logits, -jnp.inf)
    e = jnp.exp(logits - jnp.max(logits, axis=-1, keepdims=True))
    aff_ref[...] = e / jnp.sum(e, axis=-1, keepdims=True)


def _outproj(attn2, sgu2, x2, mod3, wo, pn1, pn2, wr, br, rows_per_batch, tm):
    n = x2.shape[0]
    tpb = rows_per_batch // tm
    row = lambda w: pl.BlockSpec((tm, w), lambda i: (i, 0))
    return pl.pallas_call(
        _outproj_kernel,
        grid=(n // tm,),
        in_specs=[row(MLA_WIDTH), row(SGU_WIDTH), row(D),
                  pl.BlockSpec((1, N_MOD, D), lambda i: (i // tpb, 0, 0)),
                  _const_spec(wo.shape), _const_spec((1, D)), _const_spec((1, D)),
                  _const_spec(wr.shape), _const_spec((1, LANES))],
        out_specs=[row(D), row(D), row(LANES)],
        out_shape=[jax.ShapeDtypeStruct((n, D), F32),
                   jax.ShapeDtypeStruct((n, D), F32),
                   jax.ShapeDtypeStruct((n, LANES), F32)],
        compiler_params=pltpu.CompilerParams(dimension_semantics=("arbitrary",), vmem_limit_bytes=VMEM_LIMIT),
        name="outproj",
    )(attn2, sgu2, x2, mod3, wo, pn1, pn2, wr, br)


def _prefix_count(x01):
    r, t = x01.shape
    nb = t // LANES
    stacked = jnp.concatenate([x01[:, k * LANES:(k + 1) * LANES] for k in range(nb)], axis=0).astype(BF16)
    ii = lax.broadcasted_iota(I32, (LANES, LANES), 0)
    jj = lax.broadcasted_iota(I32, (LANES, LANES), 1)
    tri = jnp.where(ii <= jj, 1.0, 0.0).astype(BF16)
    within = _dot(stacked, tri)
    off = jnp.zeros((r, 1), F32)
    blocks = []
    for k in range(nb):
        w = within[k * r:(k + 1) * r, :]
        blocks.append(w + off)
        off = off + w[:, LANES - 1:LANES]
    return jnp.concatenate(blocks, axis=1)


ROUTE_SLOTS = 128
ROUTE_SEARCH_STEPS = 192


def _route_kernel(aff_ref, idx_ref, gate_ref, incl_s, isel_s, aff_s, *, cap):
    aff = aff_ref[...]
    nr, lt = aff.shape
    capf = float(cap)

    def search(i, c):
        lo, hi = c
        mid = 0.5 * (lo + hi)
        ge = jnp.sum(jnp.where(aff >= mid, 1.0, 0.0), axis=1, keepdims=True) >= capf
        return jnp.where(ge, mid, lo), jnp.where(ge, hi, mid)

    lo, _ = lax.fori_loop(0, ROUTE_SEARCH_STEPS, search,
                          (jnp.zeros((nr, 1), F32), jnp.full((nr, 1), 2.0, F32)))
    thr = jnp.min(jnp.where(aff >= lo, aff, jnp.inf), axis=1, keepdims=True)
    gt = aff > thr
    eq = aff == thr
    n_gt = jnp.sum(jnp.where(gt, 1.0, 0.0), axis=1, keepdims=True)
    eq_rank = _prefix_count(jnp.where(eq, 1.0, 0.0))
    sel = jnp.where(gt, 1.0, jnp.where(eq, jnp.where(eq_rank <= capf - n_gt, 1.0, 0.0), 0.0))
    incl = _prefix_count(sel)
    isel = incl * sel
    for r in range(nr):
        incl_s[r] = incl[r:r + 1, :]
        isel_s[r] = isel[r:r + 1, :]
        aff_s[r] = aff[r:r + 1, :]

    ns = min(cap, ROUTE_SLOTS)

    def compact(r, _):
        for c0 in range(0, cap, ns):
            slot = (lax.broadcasted_iota(I32, (ns, LANES), 0) + c0).astype(F32)
            cnt = jnp.zeros((ns, LANES), F32)
            gat = jnp.zeros((ns, LANES), F32)
            for k in range(lt // LANES):
                ks = slice(k * LANES, (k + 1) * LANES)
                cnt = cnt + jnp.where(incl_s[r, :, ks] <= slot, 1.0, 0.0)
                gat = gat + jnp.where(isel_s[r, :, ks] == slot + 1.0, aff_s[r, :, ks], 0.0)
            idx_ref[r, c0:c0 + ns, :] = jnp.sum(cnt, axis=1, keepdims=True).astype(I32)
            gate_ref[r, c0:c0 + ns, :] = jnp.sum(gat, axis=1, keepdims=True)
        return 0

    lax.fori_loop(0, nr, compact, 0)


def _route(aff_rows, cap):
    nr, lt = aff_rows.shape
    return pl.pallas_call(
        functools.partial(_route_kernel, cap=cap),
        out_shape=[jax.ShapeDtypeStruct((nr, cap, 1), I32),
                   jax.ShapeDtypeStruct((nr, cap, 1), F32)],
        scratch_shapes=[pltpu.VMEM((nr, 1, lt), F32)] * 3,
        compiler_params=pltpu.CompilerParams(vmem_limit_bytes=VMEM_LIMIT),
        name="route",
    )(aff_rows)


BATCH_GROUP = 4
WEIGHT_DMA_PRIORITY = 1


def _ffn_kernel(idx_ref, h2_ref, gate_ref, wg_ref, wu_ref, wd_ref, o_ref, ring, lhs, acc, wgs, wus, wds,
                wgb, wub, wdb, sems, wsem, *, cap, seq, chunk):
    p = pl.program_id(0)
    e = pl.program_id(1)
    f = pl.program_id(2)
    bb = pl.program_id(3)
    ne = pl.num_programs(1)
    nf = pl.num_programs(2)
    steps = nf * BATCH_GROUP
    cpb = cap // chunk
    n_items = pl.num_programs(0) * ne
    w = p * ne + e
    s = f * BATCH_GROUP + bb
    g = w * steps + s

    def issue_chunk(item, k, slot):
        bi = lax.div(item, ne) * BATCH_GROUP + lax.div(k, cpb)
        base = (bi * ne + lax.rem(item, ne)) * cap + lax.rem(k, cpb) * chunk
        for i in range(chunk):
            tok = bi * seq + idx_ref[base + i]
            pltpu.make_async_copy(h2_ref.at[pl.ds(tok, 1)], ring.at[slot, pl.ds(i, 1)], sems.at[slot]).start()

    def wait_chunk(slot):
        pltpu.make_async_copy(h2_ref.at[pl.ds(0, chunk)], ring.at[slot], sems.at[slot]).wait()

    def drain_chunk(slot, par, k):
        wait_chunk(slot)
        r0 = pl.multiple_of(lax.rem(k, cpb) * chunk, chunk)
        lhs[par, lax.div(k, cpb), pl.ds(r0, chunk), :] = ring[slot].astype(BF16)

    tf = wgb.shape[1]

    def weight_copies(tile):
        ei = lax.rem(lax.div(tile, nf), ne)
        c0 = pl.multiple_of(lax.rem(tile, nf) * tf, tf)
        return (pltpu.make_async_copy(wg_ref.at[ei, :, pl.ds(c0, tf)], wgs, wsem.at[0]),
                pltpu.make_async_copy(wu_ref.at[ei, :, pl.ds(c0, tf)], wus, wsem.at[1]),
                pltpu.make_async_copy(wd_ref.at[ei, pl.ds(c0, tf), :], wds, wsem.at[2]))

    @pl.when(g == 0)
    def _prologue():
        for cp in weight_copies(0):
            cp.start(priority=WEIGHT_DMA_PRIORITY)

        def fetch(k, _):
            issue_chunk(0, k, 0)
            drain_chunk(0, 0, k)
            return 0
        lax.fori_loop(0, steps - 1, fetch, 0)
        issue_chunk(0, steps - 1, 1)

    gq = g - 1 + steps
    drain_chunk(lax.rem(g + 1, 2), lax.rem(lax.div(gq, steps), 2), lax.rem(gq, steps))

    @pl.when(bb == 0)
    def _next_weights():
        tile = w * nf + f
        for cp in weight_copies(tile):
            cp.wait()
        wgb[...] = wgs[...].astype(BF16)
        wub[...] = wus[...].astype(BF16)
        wdb[...] = wds[...].astype(BF16)

        @pl.when(tile + 1 < n_items * nf)
        def _prefetch():
            for cp in weight_copies(tile + 1):
                cp.start(priority=WEIGHT_DMA_PRIORITY)

    @pl.when(f == 0)
    def _zero():
        acc[bb] = jnp.zeros((cap, D), F32)

    issue_chunk(jnp.minimum(w + 1, n_items - 1), s, lax.rem(g, 2))
    x = lhs[lax.rem(w, 2), bb]
    a = _dot(x, wgb[...])
    gg = _dot(x, wub[...])
    hm = (_silu(a) * gg).astype(BF16)
    acc[bb] += _dot(hm, wdb[...])

    @pl.when(f == nf - 1)
    def _emit():
        gate = gate_ref[0]
        for j in range(ROW_TILES):
            o_ref[pl.ds(j, cap, stride=ROW_TILES), :] = acc[bb, :, j * LANES:(j + 1) * LANES] * gate

    @pl.when(g == n_items * steps - 1)
    def _tail():
        wait_chunk(lax.rem(g, 2))


def _ffn(idx_flat, h2, gate_col, w_gate, w_up, w_down, nb, seq, cap, tf):
    ne, _, ff = w_gate.shape
    nf = ff // tf
    chunk = cap // nf
    assert nb % BATCH_GROUP == 0 and cap % nf == 0 and chunk % 16 == 0
    pair_block = lambda p, e, f, bb, idx: ((p * BATCH_GROUP + jnp.where(f == nf - 1, bb, 0)) * ne + e, 0)
    grid_spec = pltpu.PrefetchScalarGridSpec(
        num_scalar_prefetch=1,
        grid=(nb // BATCH_GROUP, ne, nf, BATCH_GROUP),
        in_specs=[pl.BlockSpec(memory_space=pl.ANY),
                  pl.BlockSpec((1, cap, 1), lambda p, e, f, bb, idx: pair_block(p, e, f, bb, idx) + (0,)),
                  pl.BlockSpec(memory_space=pl.ANY),
                  pl.BlockSpec(memory_space=pl.ANY),
                  pl.BlockSpec(memory_space=pl.ANY)],
        out_specs=pl.BlockSpec((cap * ROW_TILES, LANES), pair_block),
        scratch_shapes=[pltpu.VMEM((2, chunk, D), F32),
                        pltpu.VMEM((2, BATCH_GROUP, cap, D), BF16),
                        pltpu.VMEM((BATCH_GROUP, cap, D), F32),
                        pltpu.VMEM((D, tf), F32),
                        pltpu.VMEM((D, tf), F32),
                        pltpu.VMEM((tf, D), F32),
                        pltpu.VMEM((D, tf), BF16),
                        pltpu.VMEM((D, tf), BF16),
                        pltpu.VMEM((tf, D), BF16),
                        pltpu.SemaphoreType.DMA((2,)),
                        pltpu.SemaphoreType.DMA((3,))],
    )
    return pl.pallas_call(
        functools.partial(_ffn_kernel, cap=cap, seq=seq, chunk=chunk),
        grid_spec=grid_spec,
        out_shape=jax.ShapeDtypeStruct((nb * ne * cap * ROW_TILES, LANES), F32),
        compiler_params=pltpu.CompilerParams(dimension_semantics=("arbitrary",) * 4, vmem_limit_bytes=VMEM_LIMIT),
        name="ffn",
    )(idx_flat, h2, gate_col, w_gate, w_up, w_down)


COMBINE_UNROLL = 8


def _combine_kernel(idx_ref, ye_ref, y_ref, *, cap):
    b = pl.program_id(0)
    e = pl.program_id(1)
    pair = b * pl.num_programs(1) + e

    @pl.when(e == 0)
    def _zero():
        y_ref[...] = jnp.zeros(y_ref.shape, F32)

    def group(gi, _):
        base = gi * COMBINE_UNROLL
        toks = [idx_ref[pair * cap + base + u] for u in range(COMBINE_UNROLL)]
        rows = [y_ref[toks[u]] + ye_ref[base + u] for u in range(COMBINE_UNROLL)]
        for u in range(COMBINE_UNROLL):
            y_ref[toks[u]] = rows[u]
        return 0

    lax.fori_loop(0, cap // COMBINE_UNROLL, group, 0)


def _combine(idx_flat, ye, nb, seq, cap):
    ne = ye.shape[0] // (nb * cap * ROW_TILES)
    ye4 = ye.reshape(nb * ne, cap, ROW_TILES, LANES)
    grid_spec = pltpu.PrefetchScalarGridSpec(
        num_scalar_prefetch=1,
        grid=(nb, ne),
        in_specs=[pl.BlockSpec((None, cap, ROW_TILES, LANES), lambda b, e, idx: (b * ne + e, 0, 0, 0))],
        out_specs=pl.BlockSpec((None, seq, ROW_TILES, LANES), lambda b, e, idx: (b, 0, 0, 0),
                               pipeline_mode=pl.Buffered(1)),
    )
    return pl.pallas_call(
        functools.partial(_combine_kernel, cap=cap),
        grid_spec=grid_spec,
        out_shape=jax.ShapeDtypeStruct((nb, seq, ROW_TILES, LANES), F32),
        compiler_params=pltpu.CompilerParams(dimension_semantics=("arbitrary",) * 2, vmem_limit_bytes=VMEM_LIMIT),
        name="combine",
    )(idx_flat, ye4)


def _final_kernel(y_ref, xn_ref, mod_ref, pn_ref, o_ref):
    tm = xn_ref.shape[0]
    g2 = mod_ref[0][5:6]
    ss = jnp.zeros((tm, LANES), F32)
    for j in range(ROW_TILES):
        v = y_ref[pl.ds(j, tm, stride=ROW_TILES), :]
        ss = ss + v * v
    rs = lax.rsqrt(jnp.sum(ss, axis=-1, keepdims=True) * (1.0 / D) + EPS)
    for j in range(ROW_TILES):
        cs = slice(j * LANES, (j + 1) * LANES)
        v = y_ref[pl.ds(j, tm, stride=ROW_TILES), :]
        o_ref[:, cs] = xn_ref[:, cs] + g2[:, cs] * (v * rs * pn_ref[:, cs])


def _final(y2_rows, xn, mod3, pn, rows_per_batch, tm):
    n = xn.shape[0]
    tpb = rows_per_batch // tm
    return pl.pallas_call(
        _final_kernel,
        grid=(n // tm,),
        in_specs=[pl.BlockSpec((tm * ROW_TILES, LANES), lambda i: (i, 0)),
                  pl.BlockSpec((tm, D), lambda i: (i, 0)),
                  pl.BlockSpec((1, N_MOD, D), lambda i: (i // tpb, 0, 0)),
                  _const_spec((1, D))],
        out_specs=pl.BlockSpec((tm, D), lambda i: (i, 0)),
        out_shape=jax.ShapeDtypeStruct((n, D), F32),
        compiler_params=pltpu.CompilerParams(dimension_semantics=("arbitrary",)),
        name="final",
    )(y2_rows, xn, mod3, pn)


def _rope_partner():
    q = QK_ROPE // 4
    return np.concatenate([np.arange(q, 2 * q), np.arange(0, q), np.arange(3 * q, 4 * q), np.arange(2 * q, 3 * q)])


def _rope_table(length):
    pos = np.arange(length)
    half = QK_ROPE // 2
    inv = (1.0 / (ROPE_THETA ** (np.arange(0, half, 2, dtype=np.float32) / half))).astype(np.float32)
    ar = (pos // GRID_W).astype(np.float32)[:, None] * inv
    ac = (pos % GRID_W).astype(np.float32)[:, None] * inv
    cos = np.concatenate([np.cos(ar), np.cos(ar), np.cos(ac), np.cos(ac)], axis=1)
    sin = np.concatenate([-np.sin(ar), np.sin(ar), -np.sin(ac), np.sin(ac)], axis=1)
    z = np.zeros((length, LANES - QK_ROPE), np.float32)
    return jnp.asarray(np.concatenate([cos, z, sin, z], axis=1).astype(np.float32))


def _identity_rope_table(length):
    t = np.zeros((length, 2 * LANES), np.float32)
    t[:, :QK_ROPE] = 1.0
    return jnp.asarray(t)


def _prep_w_in(w_in):
    perm = _rope_partner()
    kpe = w_in[:, 2 * Q_LORA:2 * Q_LORA + QK_ROPE]
    z = jnp.zeros((D, LANES - QK_ROPE), w_in.dtype)
    rest = w_in[:, 2 * Q_LORA + QK_ROPE:]
    return jnp.concatenate([w_in[:, :2 * Q_LORA], kpe, z, kpe[:, perm], z, rest], axis=1).astype(BF16)


def _prep_w_uq(w_uq):
    perm = _rope_partner()
    w = w_uq.reshape(Q_LORA, HEADS, QK_NOPE + QK_ROPE)
    z = jnp.zeros((Q_LORA, HEADS, LANES - QK_ROPE), w_uq.dtype)
    main = jnp.concatenate([w, z], axis=2).reshape(Q_LORA, HEADS * QK_PAD)
    partner = jnp.concatenate([w[:, :, QK_NOPE:][:, :, perm], z], axis=2).reshape(Q_LORA, HEADS * LANES)
    return jnp.concatenate([main, partner], axis=1).astype(BF16)


def _prep_w_ukv(w_ukv):
    w = w_ukv.reshape(KV_LORA, HEADS, 2, QK_NOPE)
    return w.transpose(0, 2, 1, 3).reshape(KV_LORA, 2 * HEADS * QK_NOPE).astype(BF16)


def kernel(x, c, ctx, c_ctx, w_ada, b_ada, pre_norm1, w_in, q_norm_w, w_uq, kv_norm_w, w_ukv, sgu_norm_w, sgu_w,
           sgu_b, w_out, post_norm1, pre_norm2, w_router, b_router, w_e_gate, w_e_up, w_e_down, post_norm2):
    nb, seq, _ = x.shape
    lc = ctx.shape[1]
    depth = w_ada.shape[0]
    assert depth == 1 and seq % 512 == 0 and lc % 128 == 0 and nb < SUBLANES
    cap = CAP_FACTOR * seq // N_EXPERTS
    n = nb * seq
    tm = 256

    cc = jnp.zeros((SUBLANES, D), F32).at[:nb].set(c).at[nb].set(c_ctx)
    mod3 = _ada(cc, w_ada[0], b_ada[0]).reshape(SUBLANES, N_MOD, D)

    row = lambda w: w.reshape(1, -1)
    win = _prep_w_in(w_in[0])
    wuq = _prep_w_uq(w_uq[0])
    wukv = _prep_w_ukv(w_ukv[0])
    x2 = x.reshape(n, D)
    q, k, v, sgu = _inproj(x2, mod3, row(pre_norm1[0]), win, row(q_norm_w[0]), wuq, row(kv_norm_w[0]), wukv,
                           row(sgu_norm_w[0]), sgu_w[0].astype(BF16), sgu_b[0].T, _rope_table(seq), seq, 2 * tm)
    kc, vc = _ctx_kv(ctx.reshape(nb * lc, D), mod3, nb, row(pre_norm1[0]), win, row(kv_norm_w[0]), wukv,
                     _identity_rope_table(lc), lc)

    attn = _attention(q.reshape(nb, seq, -1), k.reshape(nb, seq, -1), v.reshape(nb, seq, -1),
                      kc.reshape(nb, lc, -1), vc.reshape(nb, lc, -1), tq=512, tk=4096)

    wr = jnp.zeros((D, LANES), BF16).at[:, :N_EXPERTS].set(w_router[0].astype(BF16))
    br = jnp.zeros((1, LANES), F32).at[0, :N_EXPERTS].set(b_router[0])
    xn, h2_rows, aff = _outproj(attn.reshape(n, MLA_WIDTH), sgu, x2, mod3, w_out[0].astype(BF16),
                                row(post_norm1[0]), row(pre_norm2[0]), wr, br, seq, tm)

    aff_t = aff[:, :N_EXPERTS].reshape(nb, seq, N_EXPERTS).transpose(0, 2, 1)
    idx_col, gate_col = _route(aff_t.reshape(nb * N_EXPERTS, seq), cap)
    idx_flat = idx_col.reshape(-1)

    ye = _ffn(idx_flat, h2_rows, gate_col, w_e_gate[0], w_e_up[0], w_e_down[0], nb, seq, cap, tf=256)
    y2 = _combine(idx_flat, ye, nb, seq, cap)
    out = _final(y2.reshape(n * ROW_TILES, LANES), xn, mod3, row(post_norm2[0]), seq, tm)
    return out.reshape(nb, seq, D)
```

```python
import functools

import numpy as np
import jax
import jax.numpy as jnp
from jax import lax
from jax.experimental import pallas as pl
from jax.experimental.pallas import tpu as pltpu

F32 = jnp.float32
BF16 = jnp.bfloat16
I32 = jnp.int32

D = 2048
GRID_W = 64
EPS = 1e-6
N_MOD = 6
HEADS = 8
Q_LORA = 512
KV_LORA = 512
QK_NOPE = 128
QK_ROPE = 64
V_HEAD = 128
ROPE_THETA = 10000.0
ATTN_SCALE = (QK_NOPE + QK_ROPE) ** -0.5
Q_SCALE = ATTN_SCALE * float(np.log2(np.e))
SGU_HEADS = 8
SGU_DIM = 128
CHUNK = 128
MLA_WIDTH = HEADS * V_HEAD
SGU_WIDTH = SGU_HEADS * SGU_DIM
N_EXPERTS = 16
CAP_FACTOR = 2
EXPERT_FF = D

LANES = 128
SUBLANES = 8
ROW_TILES = D // LANES
QK_PAD = 256
VMEM_LIMIT = 56 * 1024 * 1024

C_Q = 0
C_KV = C_Q + Q_LORA
C_KPE = C_KV + KV_LORA
C_KPP = C_KPE + LANES
C_U = C_KPP + LANES
C_V = C_U + SGU_WIDTH
IN_EXT = C_V + SGU_WIDTH


def _rms(x, w):
    return x * lax.rsqrt(jnp.mean(x * x, axis=-1, keepdims=True) + EPS) * w


def _gelu_tanh(x):
    return 0.5 * x * (1.0 + jnp.tanh(np.sqrt(2.0 / np.pi).astype(np.float32) * (x + 0.044715 * (x * x * x))))


def _silu(x):
    return x * (1.0 / (1.0 + jnp.exp(-x)))


def _dot(a, b):
    return jnp.dot(a, b, preferred_element_type=F32)


def _ada_kernel(c_ref, w_ref, b_ref, o_ref):
    s = _silu(c_ref[...]).astype(BF16)
    o_ref[...] = _dot(s, w_ref[...].astype(BF16)) + b_ref[...]


def _ada(cc, w_ada, b_ada):
    n = w_ada.shape[1]
    tn = 1024
    return pl.pallas_call(
        _ada_kernel,
        grid=(n // tn,),
        in_specs=[pl.BlockSpec((SUBLANES, D), lambda j: (0, 0)),
                  pl.BlockSpec((D, tn), lambda j: (0, j)),
                  pl.BlockSpec((1, tn), lambda j: (0, j))],
        out_specs=pl.BlockSpec((SUBLANES, tn), lambda j: (0, j)),
        out_shape=jax.ShapeDtypeStruct((SUBLANES, n), F32),
        compiler_params=pltpu.CompilerParams(dimension_semantics=("arbitrary",), vmem_limit_bytes=VMEM_LIMIT),
        name="ada",
    )(cc, w_ada, b_ada.reshape(1, n))


def _kv_path(hb, win_ref, kvn_ref, wukv_ref, cos, sin, k_ref, v_ref):
    ckv = _dot(hb, win_ref[:, C_KV:C_KV + KV_LORA])
    ckvn = _rms(ckv, kvn_ref[...]).astype(BF16)
    kn = _dot(ckvn, wukv_ref[:, :HEADS * QK_NOPE])
    v_ref[...] = _dot(ckvn, wukv_ref[:, HEADS * QK_NOPE:]).astype(BF16)
    kp = _dot(hb, win_ref[:, C_KPE:C_KPE + 2 * LANES])
    kpe = (kp[:, :LANES] * cos + kp[:, LANES:] * sin).astype(BF16)
    for h in range(HEADS):
        k_ref[:, h * QK_PAD:h * QK_PAD + QK_NOPE] = kn[:, h * QK_NOPE:(h + 1) * QK_NOPE].astype(BF16)
        k_ref[:, h * QK_PAD + QK_NOPE:(h + 1) * QK_PAD] = kpe


def _inproj_kernel(x_ref, mod_ref, pn_ref, win_ref, qn_ref, wuq_ref, kvn_ref, wukv_ref,
                   sgn_ref, sgw_ref, sgb_ref, cs_ref, q_ref, k_ref, v_ref, sgu_ref):
    tm = x_ref.shape[0]
    mod = mod_ref[0]
    h = _rms(x_ref[...], pn_ref[...]) * (1.0 + mod[1:2]) + mod[0:1]
    hb = h.astype(BF16)
    cos = cs_ref[:, :LANES]
    sin = cs_ref[:, LANES:]

    _kv_path(hb, win_ref, kvn_ref, wukv_ref, cos, sin, k_ref, v_ref)

    cq = _dot(hb, win_ref[:, C_Q:C_Q + Q_LORA])
    cqn = _rms(cq, qn_ref[...]).astype(BF16)
    for hp in range(HEADS // 2):
        qm2 = _dot(cqn, wuq_ref[:, hp * 2 * QK_PAD:(hp + 1) * 2 * QK_PAD])
        qp2 = _dot(cqn, wuq_ref[:, HEADS * QK_PAD + hp * 2 * LANES:HEADS * QK_PAD + (hp + 1) * 2 * LANES])
        for i in range(2):
            h_ = 2 * hp + i
            qm = qm2[:, i * QK_PAD:(i + 1) * QK_PAD]
            qp = qp2[:, i * LANES:(i + 1) * LANES]
            q_ref[:, h_ * QK_PAD:h_ * QK_PAD + QK_NOPE] = (qm[:, :QK_NOPE] * Q_SCALE).astype(BF16)
            q_ref[:, h_ * QK_PAD + QK_NOPE:(h_ + 1) * QK_PAD] = (
                (qm[:, QK_NOPE:] * cos + qp * sin) * Q_SCALE).astype(BF16)

    for gp in range(SGU_HEADS // 2):
        u2 = _gelu_tanh(_dot(hb, win_ref[:, C_U + gp * 2 * SGU_DIM:C_U + (gp + 1) * 2 * SGU_DIM]))
        vs2 = _gelu_tanh(_dot(hb, win_ref[:, C_V + gp * 2 * SGU_DIM:C_V + (gp + 1) * 2 * SGU_DIM]))
        for i in range(2):
            g = 2 * gp + i
            gs = slice(g * SGU_DIM, (g + 1) * SGU_DIM)
            u = u2[:, i * SGU_DIM:(i + 1) * SGU_DIM]
            vs = vs2[:, i * SGU_DIM:(i + 1) * SGU_DIM]
            mu = jnp.mean(vs, axis=-1, keepdims=True)
            vc = vs - mu
            var = jnp.mean(vc * vc, axis=-1, keepdims=True)
            vn = (vc * lax.rsqrt(var + EPS) * sgn_ref[:, gs]).astype(BF16)
            wg = sgw_ref[g]
            bias = sgb_ref[:, g:g + 1]
            for n in range(tm // CHUNK):
                rs = slice(n * CHUNK, (n + 1) * CHUNK)
                mixed = _dot(wg, vn[rs, :]) + bias
                sgu_ref[rs, gs] = (u[rs, :] * mixed).astype(BF16)


def _ctx_kv_kernel(x_ref, mod_ref, pn_ref, win_ref, kvn_ref, wukv_ref, cs_ref, k_ref, v_ref):
    mod = mod_ref[0]
    h = _rms(x_ref[...], pn_ref[...]) * (1.0 + mod[1:2]) + mod[0:1]
    _kv_path(h.astype(BF16), win_ref, kvn_ref, wukv_ref, cs_ref[:, :LANES], cs_ref[:, LANES:], k_ref, v_ref)


def _const_spec(shape):
    nd = len(shape)
    return pl.BlockSpec(shape, lambda i: (0,) * nd, pipeline_mode=pl.Buffered(1))


def _inproj(x2, mod3, pn, win, qn, wuq, kvn, wukv, sgn, sgw, sgb, cs, rows_per_batch, tm):
    n = x2.shape[0]
    tpb = rows_per_batch // tm
    row = lambda w: pl.BlockSpec((tm, w), lambda i: (i, 0))
    return pl.pallas_call(
        _inproj_kernel,
        grid=(n // tm,),
        in_specs=[row(D),
                  pl.BlockSpec((1, N_MOD, D), lambda i: (i // tpb, 0, 0)),
                  _const_spec((1, D)), _const_spec(win.shape), _const_spec((1, Q_LORA)),
                  _const_spec(wuq.shape), _const_spec((1, KV_LORA)), _const_spec(wukv.shape),
                  _const_spec((1, SGU_WIDTH)), _const_spec(sgw.shape), _const_spec(sgb.shape),
                  pl.BlockSpec((tm, 2 * LANES), lambda i: (i % tpb, 0))],
        out_specs=[row(HEADS * QK_PAD), row(HEADS * QK_PAD), row(MLA_WIDTH), row(SGU_WIDTH)],
        out_shape=[jax.ShapeDtypeStruct((n, HEADS * QK_PAD), BF16),
                   jax.ShapeDtypeStruct((n, HEADS * QK_PAD), BF16),
                   jax.ShapeDtypeStruct((n, MLA_WIDTH), BF16),
                   jax.ShapeDtypeStruct((n, SGU_WIDTH), BF16)],
        compiler_params=pltpu.CompilerParams(dimension_semantics=("arbitrary",), vmem_limit_bytes=VMEM_LIMIT),
        name="inproj",
    )(x2, mod3, pn, win, qn, wuq, kvn, wukv, sgn, sgw, sgb, cs)


def _ctx_kv(c2, mod3, ctx_row, pn, win, kvn, wukv, cs, tm):
    n = c2.shape[0]
    row = lambda w: pl.BlockSpec((tm, w), lambda i: (i, 0))
    return pl.pallas_call(
        _ctx_kv_kernel,
        grid=(n // tm,),
        in_specs=[row(D),
                  pl.BlockSpec((1, N_MOD, D), lambda i: (ctx_row, 0, 0)),
                  _const_spec((1, D)), _const_spec(win.shape), _const_spec((1, KV_LORA)),
                  _const_spec(wukv.shape),
                  pl.BlockSpec((tm, 2 * LANES), lambda i: (0, 0))],
        out_specs=[row(HEADS * QK_PAD), row(MLA_WIDTH)],
        out_shape=[jax.ShapeDtypeStruct((n, HEADS * QK_PAD), BF16),
                   jax.ShapeDtypeStruct((n, MLA_WIDTH), BF16)],
        compiler_params=pltpu.CompilerParams(dimension_semantics=("arbitrary",), vmem_limit_bytes=VMEM_LIMIT),
        name="ctx_kv",
    )(c2, mod3, pn, win, kvn, wukv, cs)


def _attn_kernel(q_ref, k_ref, v_ref, kc_ref, vc_ref, o_ref, *, tk, streams):
    tq = q_ref.shape[1]
    ts = tq // streams
    nk = k_ref.shape[1] // tk
    qs = [q_ref[0, s * ts:(s + 1) * ts, :] for s in range(streams)]

    def step(q, kb, vb, carry):
        m, l, acc = carry
        s = lax.dot_general(q, kb, (((1,), (1,)), ((), ())), preferred_element_type=F32)
        m_new = jnp.maximum(m, jnp.max(s, axis=-1, keepdims=True))
        alpha = jnp.exp2(m - m_new)
        p = jnp.exp2(s - m_new)
        l = alpha * l + jnp.sum(p, axis=-1, keepdims=True)
        acc = alpha * acc + _dot(p.astype(BF16), vb)
        return m_new, l, acc

    def body(j, carries):
        off = pl.multiple_of(j * tk, tk)
        kb = k_ref[0, pl.ds(off, tk), :]
        vb = v_ref[0, pl.ds(off, tk), :]
        return tuple(step(qs[s], kb, vb, carries[s]) for s in range(streams))

    init = (jnp.full((ts, 1), -jnp.inf, F32), jnp.zeros((ts, 1), F32), jnp.zeros((ts, V_HEAD), F32))
    carries = (init,) * streams
    for j in range(nk):
        carries = body(j, carries)
    for s in range(streams):
        m, l, acc = step(qs[s], kc_ref[0], vc_ref[0], carries[s])
        o_ref[0, s * ts:(s + 1) * ts, :] = (acc / l).astype(BF16)


def _attention(q, k, v, kc, vc, tq, tk):
    b, l, _ = q.shape
    lc = kc.shape[1]
    tk = min(tk, l)
    assert l % tk == 0 and l % tq == 0
    return pl.pallas_call(
        functools.partial(_attn_kernel, tk=tk, streams=1),
        grid=(b, HEADS, l // tq),
        in_specs=[pl.BlockSpec((1, tq, QK_PAD), lambda b_, h, i: (b_, i, h)),
                  pl.BlockSpec((1, l, QK_PAD), lambda b_, h, i: (b_, 0, h)),
                  pl.BlockSpec((1, l, V_HEAD), lambda b_, h, i: (b_, 0, h)),
                  pl.BlockSpec((1, lc, QK_PAD), lambda b_, h, i: (b_, 0, h)),
                  pl.BlockSpec((1, lc, V_HEAD), lambda b_, h, i: (b_, 0, h))],
        out_specs=pl.BlockSpec((1, tq, V_HEAD), lambda b_, h, i: (b_, i, h)),
        out_shape=jax.ShapeDtypeStruct((b, l, MLA_WIDTH), BF16),
        compiler_params=pltpu.CompilerParams(dimension_semantics=("arbitrary",) * 3, vmem_limit_bytes=VMEM_LIMIT),
        name="attn",
    )(q, k, v, kc, vc)


def _outproj_kernel(at_ref, sg_ref, x_ref, mod_ref, wo_ref, pn1_ref, pn2_ref, wr_ref, br_ref,
                    xn_ref, h2_ref, aff_ref):
    tm = x_ref.shape[0]
    mod = mod_ref[0]
    y = _dot(at_ref[...], wo_ref[:MLA_WIDTH, :]) + _dot(sg_ref[...], wo_ref[MLA_WIDTH:, :])
    xn = x_ref[...] + mod[2:3] * _rms(y, pn1_ref[...])
    xn_ref[...] = xn
    h2 = _rms(xn, pn2_ref[...]) * (1.0 + mod[4:5]) + mod[3:4]
    h2_ref[...] = h2
    logits = _dot(h2.astype(BF16), wr_ref[...]) + br_ref[...]
    lane = lax.broadcasted_iota(I32, logits.shape, 1)
    logits = jnp.where(lane < N_EXPERTS, logits, -jnp.inf)
    e = jnp.exp(logits - jnp.max(logits, axis=-1, keepdims=True))
    aff = e / jnp.sum(e, axis=-1, keepdims=True)
    aff_ref[...] = jnp.transpose(aff)[:N_EXPERTS, :]


def _outproj(attn2, sgu2, x2, mod3, wo, pn1, pn2, wr, br, rows_per_batch, tm):
    n = x2.shape[0]
    tpb = rows_per_batch // tm
    row = lambda w: pl.BlockSpec((tm, w), lambda i: (i, 0))
    return pl.pallas_call(
        _outproj_kernel,
        grid=(n // tm,),
        in_specs=[row(MLA_WIDTH), row(SGU_WIDTH), row(D),
                  pl.BlockSpec((1, N_MOD, D), lambda i: (i // tpb, 0, 0)),
                  _const_spec(wo.shape), _const_spec((1, D)), _const_spec((1, D)),
                  _const_spec(wr.shape), _const_spec((1, LANES))],
        out_specs=[row(D), row(D), pl.BlockSpec((N_EXPERTS, tm), lambda i: (0, i))],
        out_shape=[jax.ShapeDtypeStruct((n, D), F32),
                   jax.ShapeDtypeStruct((n, D), F32),
                   jax.ShapeDtypeStruct((N_EXPERTS, n), F32)],
        compiler_params=pltpu.CompilerParams(dimension_semantics=("arbitrary",), vmem_limit_bytes=VMEM_LIMIT),
        name="outproj",
    )(attn2, sgu2, x2, mod3, wo, pn1, pn2, wr, br)


def _prefix_count(x01):
    r, t = x01.shape
    nb = t // LANES
    stacked = jnp.concatenate([x01[:, k * LANES:(k + 1) * LANES] for k in range(nb)], axis=0).astype(BF16)
    ii = lax.broadcasted_iota(I32, (LANES, LANES), 0)
    jj = lax.broadcasted_iota(I32, (LANES, LANES), 1)
    tri = jnp.where(ii <= jj, 1.0, 0.0).astype(BF16)
    within = _dot(stacked, tri)
    off = jnp.zeros((r, 1), F32)
    blocks = []
    for k in range(nb):
        w = within[k * r:(k + 1) * r, :]
        blocks.append(w + off)
        off = off + w[:, LANES - 1:LANES]
    return jnp.concatenate(blocks, axis=1)


ROUTE_SLOTS = 128
ROUTE_SEARCH_STEPS = 192


def _route_kernel(aff_ref, idx_ref, gate_ref, incl_s, isel_s, aff_s, *, cap, lt):
    aff = jnp.concatenate([aff_ref[:, b * lt:(b + 1) * lt] for b in range(aff_ref.shape[1] // lt)], axis=0)
    nr = aff.shape[0]
    capf = float(cap)

    def search(i, c):
        lo, hi = c
        mid = 0.5 * (lo + hi)
        ge = jnp.sum(jnp.where(aff >= mid, 1.0, 0.0), axis=1, keepdims=True) >= capf
        return jnp.where(ge, mid, lo), jnp.where(ge, hi, mid)

    lo, _ = lax.fori_loop(0, ROUTE_SEARCH_STEPS, search,
                          (jnp.zeros((nr, 1), F32), jnp.full((nr, 1), 2.0, F32)))
    thr = jnp.min(jnp.where(aff >= lo, aff, jnp.inf), axis=1, keepdims=True)
    gt = aff > thr
    eq = aff == thr
    n_gt = jnp.sum(jnp.where(gt, 1.0, 0.0), axis=1, keepdims=True)
    eq_rank = _prefix_count(jnp.where(eq, 1.0, 0.0))
    sel = jnp.where(gt, 1.0, jnp.where(eq, jnp.where(eq_rank <= capf - n_gt, 1.0, 0.0), 0.0))
    incl = _prefix_count(sel)
    isel = incl * sel
    for r in range(nr):
        incl_s[r] = incl[r:r + 1, :]
        isel_s[r] = isel[r:r + 1, :]
        aff_s[r] = aff[r:r + 1, :]

    ns = min(cap, ROUTE_SLOTS)

    def compact(r, _):
        for c0 in range(0, cap, ns):
            slot = (lax.broadcasted_iota(I32, (ns, LANES), 0) + c0).astype(F32)
            cnt = jnp.zeros((ns, LANES), F32)
            gat = jnp.zeros((ns, LANES), F32)
            for k in range(lt // LANES):
                ks = slice(k * LANES, (k + 1) * LANES)
                cnt = cnt + jnp.where(incl_s[r, :, ks] <= slot, 1.0, 0.0)
                gat = gat + jnp.where(isel_s[r, :, ks] == slot + 1.0, aff_s[r, :, ks], 0.0)
            idx_ref[r, c0:c0 + ns, :] = jnp.sum(cnt, axis=1, keepdims=True).astype(I32)
            gate_ref[r, c0:c0 + ns, :] = jnp.sum(gat, axis=1, keepdims=True)
        return 0

    lax.fori_loop(0, nr, compact, 0)


def _route(aff_em, lt, cap):
    nr = aff_em.shape[0] * (aff_em.shape[1] // lt)
    return pl.pallas_call(
        functools.partial(_route_kernel, cap=cap, lt=lt),
        out_shape=[jax.ShapeDtypeStruct((nr, cap, 1), I32),
                   jax.ShapeDtypeStruct((nr, cap, 1), F32)],
        scratch_shapes=[pltpu.VMEM((nr, 1, lt), F32)] * 3,
        compiler_params=pltpu.CompilerParams(vmem_limit_bytes=VMEM_LIMIT),
        name="route",
    )(aff_em)


BATCH_GROUP = 4
WEIGHT_DMA_PRIORITY = 1


def _ffn_kernel(idx_ref, h2_ref, gate_ref, wg_ref, wu_ref, wd_ref, o_ref, ring, lhs, acc, wgs, wus, wds,
                wgb, wub, wdb, sems, wsem, *, cap, seq, chunk):
    p = pl.program_id(0)
    e = pl.program_id(1)
    f = pl.program_id(2)
    bb = pl.program_id(3)
    ne = pl.num_programs(1)
    nf = pl.num_programs(2)
    steps = nf * BATCH_GROUP
    cpb = cap // chunk
    n_items = pl.num_programs(0) * ne
    w = p * ne + e
    s = f * BATCH_GROUP + bb
    g = w * steps + s

    def issue_chunk(item, k, slot):
        bi = lax.div(item, ne) * BATCH_GROUP + lax.div(k, cpb)
        base = (bi * ne + lax.rem(item, ne)) * cap + lax.rem(k, cpb) * chunk
        for i in range(chunk):
            tok = bi * seq + idx_ref[base + i]
            pltpu.make_async_copy(h2_ref.at[pl.ds(tok, 1)], ring.at[slot, pl.ds(i, 1)],
                                  sems.at[slot]).start(priority=i % 2)

    def wait_chunk(slot):
        pltpu.make_async_copy(h2_ref.at[pl.ds(0, chunk)], ring.at[slot], sems.at[slot]).wait()

    def drain_chunk(slot, par, k):
        wait_chunk(slot)
        r0 = pl.multiple_of(lax.rem(k, cpb) * chunk, chunk)
        lhs[par, lax.div(k, cpb), pl.ds(r0, chunk), :] = ring[slot].astype(BF16)

    tf = wgb.shape[1]

    def weight_copies(tile):
        ei = lax.rem(lax.div(tile, nf), ne)
        c0 = pl.multiple_of(lax.rem(tile, nf) * tf, tf)
        return (pltpu.make_async_copy(wg_ref.at[ei, :, pl.ds(c0, tf)], wgs, wsem.at[0]),
                pltpu.make_async_copy(wu_ref.at[ei, :, pl.ds(c0, tf)], wus, wsem.at[1]),
                pltpu.make_async_copy(wd_ref.at[ei, pl.ds(c0, tf), :], wds, wsem.at[2]))

    @pl.when(g == 0)
    def _prologue():
        for cp in weight_copies(0):
            cp.start(priority=WEIGHT_DMA_PRIORITY)

        def fetch(k, _):
            issue_chunk(0, k, 0)
            drain_chunk(0, 0, k)
            return 0
        lax.fori_loop(0, steps - 1, fetch, 0)
        issue_chunk(0, steps - 1, 1)

    gq = g - 1 + steps
    drain_chunk(lax.rem(g + 1, 2), lax.rem(lax.div(gq, steps), 2), lax.rem(gq, steps))

    @pl.when(bb == 0)
    def _next_weights():
        tile = w * nf + f
        for cp in weight_copies(tile):
            cp.wait()
        wgb[...] = wgs[...].astype(BF16)
        wub[...] = wus[...].astype(BF16)
        wdb[...] = wds[...].astype(BF16)

        @pl.when(tile + 1 < n_items * nf)
        def _prefetch():
            for cp in weight_copies(tile + 1):
                cp.start(priority=WEIGHT_DMA_PRIORITY)

    @pl.when(f == 0)
    def _zero():
        acc[bb] = jnp.zeros((cap, D), F32)

    issue_chunk(jnp.minimum(w + 1, n_items - 1), s, lax.rem(g, 2))
    x = lhs[lax.rem(w, 2), bb]
    a = _dot(x, wgb[...])
    gg = _dot(x, wub[...])
    hm = (_silu(a) * gg).astype(BF16)
    acc[bb] += _dot(hm, wdb[...])

    @pl.when(f == nf - 1)
    def _emit():
        gate = gate_ref[0]
        for j in range(ROW_TILES):
            o_ref[pl.ds(j, cap, stride=ROW_TILES), :] = acc[bb, :, j * LANES:(j + 1) * LANES] * gate

    @pl.when(g == n_items * steps - 1)
    def _tail():
        wait_chunk(lax.rem(g, 2))


def _ffn(idx_flat, h2, gate_col, w_gate, w_up, w_down, nb, seq, cap, tf):
    ne, _, ff = w_gate.shape
    nf = ff // tf
    chunk = cap // nf
    assert nb % BATCH_GROUP == 0 and cap % nf == 0 and chunk % 16 == 0
    pair_block = lambda p, e, f, bb, idx: ((p * BATCH_GROUP + jnp.where(f == nf - 1, bb, 0)) * ne + e, 0)
    grid_spec = pltpu.PrefetchScalarGridSpec(
        num_scalar_prefetch=1,
        grid=(nb // BATCH_GROUP, ne, nf, BATCH_GROUP),
        in_specs=[pl.BlockSpec(memory_space=pl.ANY),
                  pl.BlockSpec((1, cap, 1), lambda p, e, f, bb, idx: pair_block(p, e, f, bb, idx) + (0,)),
                  pl.BlockSpec(memory_space=pl.ANY),
                  pl.BlockSpec(memory_space=pl.ANY),
                  pl.BlockSpec(memory_space=pl.ANY)],
        out_specs=pl.BlockSpec((cap * ROW_TILES, LANES), pair_block),
        scratch_shapes=[pltpu.VMEM((2, chunk, D), F32),
                        pltpu.VMEM((2, BATCH_GROUP, cap, D), BF16),
                        pltpu.VMEM((BATCH_GROUP, cap, D), F32),
                        pltpu.VMEM((D, tf), F32),
                        pltpu.VMEM((D, tf), F32),
                        pltpu.VMEM((tf, D), F32),
                        pltpu.VMEM((D, tf), BF16),
                        pltpu.VMEM((D, tf), BF16),
                        pltpu.VMEM((tf, D), BF16),
                        pltpu.SemaphoreType.DMA((2,)),
                        pltpu.SemaphoreType.DMA((3,))],
    )
    return pl.pallas_call(
        functools.partial(_ffn_kernel, cap=cap, seq=seq, chunk=chunk),
        grid_spec=grid_spec,
        out_shape=jax.ShapeDtypeStruct((nb * ne * cap * ROW_TILES, LANES), F32),
        compiler_params=pltpu.CompilerParams(dimension_semantics=("arbitrary",) * 4, vmem_limit_bytes=VMEM_LIMIT),
        name="ffn",
    )(idx_flat, h2, gate_col, w_gate, w_up, w_down)


COMBINE_UNROLL = 8


def _combine_kernel(idx_ref, ye_ref, y_ref, *, cap):
    b = pl.program_id(0)
    e = pl.program_id(1)
    pair = b * pl.num_programs(1) + e

    @pl.when(e == 0)
    def _zero():
        y_ref[...] = jnp.zeros(y_ref.shape, F32)

    def group(gi, _):
        base = gi * COMBINE_UNROLL
        toks = [idx_ref[pair * cap + base + u] for u in range(COMBINE_UNROLL)]
        rows = [y_ref[toks[u]] + ye_ref[base + u] for u in range(COMBINE_UNROLL)]
        for u in range(COMBINE_UNROLL):
            y_ref[toks[u]] = rows[u]
        return 0

    lax.fori_loop(0, cap // COMBINE_UNROLL, group, 0)


def _combine(idx_flat, ye, nb, seq, cap):
    ne = ye.shape[0] // (nb * cap * ROW_TILES)
    ye4 = ye.reshape(nb * ne, cap, ROW_TILES, LANES)
    grid_spec = pltpu.PrefetchScalarGridSpec(
        num_scalar_prefetch=1,
        grid=(nb, ne),
        in_specs=[pl.BlockSpec((None, cap, ROW_TILES, LANES), lambda b, e, idx: (b * ne + e, 0, 0, 0))],
        out_specs=pl.BlockSpec((None, seq, ROW_TILES, LANES), lambda b, e, idx: (b, 0, 0, 0),
                               pipeline_mode=pl.Buffered(1)),
    )
    return pl.pallas_call(
        functools.partial(_combine_kernel, cap=cap),
        grid_spec=grid_spec,
        out_shape=jax.ShapeDtypeStruct((nb, seq, ROW_TILES, LANES), F32),
        compiler_params=pltpu.CompilerParams(dimension_semantics=("arbitrary",) * 2, vmem_limit_bytes=VMEM_LIMIT),
        name="combine",
    )(idx_flat, ye4)


def _final_kernel(y_ref, xn_ref, mod_ref, pn_ref, o_ref):
    tm = xn_ref.shape[0]
    g2 = mod_ref[0][5:6]
    ss = jnp.zeros((tm, LANES), F32)
    for j in range(ROW_TILES):
        v = y_ref[pl.ds(j, tm, stride=ROW_TILES), :]
        ss = ss + v * v
    rs = lax.rsqrt(jnp.sum(ss, axis=-1, keepdims=True) * (1.0 / D) + EPS)
    for j in range(ROW_TILES):
        cs = slice(j * LANES, (j + 1) * LANES)
        v = y_ref[pl.ds(j, tm, stride=ROW_TILES), :]
        o_ref[:, cs] = xn_ref[:, cs] + g2[:, cs] * (v * rs * pn_ref[:, cs])


def _final(y2_rows, xn, mod3, pn, rows_per_batch, tm):
    n = xn.shape[0]
    tpb = rows_per_batch // tm
    return pl.pallas_call(
        _final_kernel,
        grid=(n // tm,),
        in_specs=[pl.BlockSpec((tm * ROW_TILES, LANES), lambda i: (i, 0)),
                  pl.BlockSpec((tm, D), lambda i: (i, 0)),
                  pl.BlockSpec((1, N_MOD, D), lambda i: (i // tpb, 0, 0)),
                  _const_spec((1, D))],
        out_specs=pl.BlockSpec((tm, D), lambda i: (i, 0)),
        out_shape=jax.ShapeDtypeStruct((n, D), F32),
        compiler_params=pltpu.CompilerParams(dimension_semantics=("arbitrary",)),
        name="final",
    )(y2_rows, xn, mod3, pn)


def _rope_partner():
    q = QK_ROPE // 4
    return np.concatenate([np.arange(q, 2 * q), np.arange(0, q), np.arange(3 * q, 4 * q), np.arange(2 * q, 3 * q)])


def _rope_table(length):
    pos = np.arange(length)
    half = QK_ROPE // 2
    inv = (1.0 / (ROPE_THETA ** (np.arange(0, half, 2, dtype=np.float32) / half))).astype(np.float32)
    ar = (pos // GRID_W).astype(np.float32)[:, None] * inv
    ac = (pos % GRID_W).astype(np.float32)[:, None] * inv
    cos = np.concatenate([np.cos(ar), np.cos(ar), np.cos(ac), np.cos(ac)], axis=1)
    sin = np.concatenate([-np.sin(ar), np.sin(ar), -np.sin(ac), np.sin(ac)], axis=1)
    z = np.zeros((length, LANES - QK_ROPE), np.float32)
    return jnp.asarray(np.concatenate([cos, z, sin, z], axis=1).astype(np.float32))


def _identity_rope_table(length):
    t = np.zeros((length, 2 * LANES), np.float32)
    t[:, :QK_ROPE] = 1.0
    return jnp.asarray(t)


def _prep_w_in(w_in):
    perm = _rope_partner()
    kpe = w_in[:, 2 * Q_LORA:2 * Q_LORA + QK_ROPE]
    z = jnp.zeros((D, LANES - QK_ROPE), w_in.dtype)
    rest = w_in[:, 2 * Q_LORA + QK_ROPE:]
    return jnp.concatenate([w_in[:, :2 * Q_LORA], kpe, z, kpe[:, perm], z, rest], axis=1).astype(BF16)


def _prep_w_uq(w_uq):
    perm = _rope_partner()
    w = w_uq.reshape(Q_LORA, HEADS, QK_NOPE + QK_ROPE)
    z = jnp.zeros((Q_LORA, HEADS, LANES - QK_ROPE), w_uq.dtype)
    main = jnp.concatenate([w, z], axis=2).reshape(Q_LORA, HEADS * QK_PAD)
    partner = jnp.concatenate([w[:, :, QK_NOPE:][:, :, perm], z], axis=2).reshape(Q_LORA, HEADS * LANES)
    return jnp.concatenate([main, partner], axis=1).astype(BF16)


def _prep_w_ukv(w_ukv):
    w = w_ukv.reshape(KV_LORA, HEADS, 2, QK_NOPE)
    return w.transpose(0, 2, 1, 3).reshape(KV_LORA, 2 * HEADS * QK_NOPE).astype(BF16)


def kernel(x, c, ctx, c_ctx, w_ada, b_ada, pre_norm1, w_in, q_norm_w, w_uq, kv_norm_w, w_ukv, sgu_norm_w, sgu_w,
           sgu_b, w_out, post_norm1, pre_norm2, w_router, b_router, w_e_gate, w_e_up, w_e_down, post_norm2):
    nb, seq, _ = x.shape
    lc = ctx.shape[1]
    depth = w_ada.shape[0]
    assert depth == 1 and seq % 512 == 0 and lc % 128 == 0 and nb < SUBLANES
    cap = CAP_FACTOR * seq // N_EXPERTS
    n = nb * seq
    tm = 256

    cc = jnp.zeros((SUBLANES, D), F32).at[:nb].set(c).at[nb].set(c_ctx)
    mod3 = _ada(cc, w_ada[0], b_ada[0]).reshape(SUBLANES, N_MOD, D)

    row = lambda w: w.reshape(1, -1)
    win = _prep_w_in(w_in[0])
    wuq = _prep_w_uq(w_uq[0])
    wukv = _prep_w_ukv(w_ukv[0])
    x2 = x.reshape(n, D)
    q, k, v, sgu = _inproj(x2, mod3, row(pre_norm1[0]), win, row(q_norm_w[0]), wuq, row(kv_norm_w[0]), wukv,
                           row(sgu_norm_w[0]), sgu_w[0].astype(BF16), sgu_b[0].T, _rope_table(seq), seq, 2 * tm)
    kc, vc = _ctx_kv(ctx.reshape(nb * lc, D), mod3, nb, row(pre_norm1[0]), win, row(kv_norm_w[0]), wukv,
                     _identity_rope_table(lc), lc)

    attn = _attention(q.reshape(nb, seq, -1), k.reshape(nb, seq, -1), v.reshape(nb, seq, -1),
                      kc.reshape(nb, lc, -1), vc.reshape(nb, lc, -1), tq=512, tk=4096)

    wr = jnp.zeros((D, LANES), BF16).at[:, :N_EXPERTS].set(w_router[0].astype(BF16))
    br = jnp.zeros((1, LANES), F32).at[0, :N_EXPERTS].set(b_router[0])
    xn, h2_rows, aff = _outproj(attn.reshape(n, MLA_WIDTH), sgu, x2, mod3, w_out[0].astype(BF16),
                                row(post_norm1[0]), row(pre_norm2[0]), wr, br, seq, tm)

    idx_col, gate_col = _route(aff, seq, cap)
    idx_flat = idx_col.reshape(-1)

    ye = _ffn(idx_flat, h2_rows, gate_col, w_e_gate[0], w_e_up[0], w_e_down[0], nb, seq, cap, tf=256)
    y2 = _combine(idx_flat, ye, nb, seq, cap)
    out = _final(y2.reshape(n * ROW_TILES, LANES), xn, mod3, row(post_norm2[0]), seq, tm)
    return out.reshape(nb, seq, D)
```

```python
import functools

import numpy as np
import jax
import jax.numpy as jnp
from jax import lax
from jax.experimental import pallas as pl
from jax.experimental.pallas import tpu as pltpu

F32 = jnp.float32
BF16 = jnp.bfloat16
I32 = jnp.int32

D = 2048
GRID_W = 64
EPS = 1e-6
N_MOD = 6
HEADS = 8
Q_LORA = 512
KV_LORA = 512
QK_NOPE = 128
QK_ROPE = 64
V_HEAD = 128
ROPE_THETA = 10000.0
ATTN_SCALE = (QK_NOPE + QK_ROPE) ** -0.5
Q_SCALE = ATTN_SCALE * float(np.log2(np.e))
SGU_HEADS = 8
SGU_DIM = 128
CHUNK = 128
MLA_WIDTH = HEADS * V_HEAD
SGU_WIDTH = SGU_HEADS * SGU_DIM
N_EXPERTS = 16
CAP_FACTOR = 2
EXPERT_FF = D

LANES = 128
SUBLANES = 8
ROW_TILES = D // LANES
QK_PAD = 256
VMEM_LIMIT = 56 * 1024 * 1024

C_Q = 0
C_KV = C_Q + Q_LORA
C_KPE = C_KV + KV_LORA
C_U = C_KPE + LANES
C_V = C_U + SGU_WIDTH
IN_EXT = C_V + SGU_WIDTH


def _rms(x, w):
    return x * lax.rsqrt(jnp.mean(x * x, axis=-1, keepdims=True) + EPS) * w


def _gelu_tanh(x):
    return 0.5 * x * (1.0 + jnp.tanh(np.sqrt(2.0 / np.pi).astype(np.float32) * (x + 0.044715 * (x * x * x))))


def _silu(x):
    return x * (1.0 / (1.0 + jnp.exp(-x)))


def _dot(a, b):
    return jnp.dot(a, b, preferred_element_type=F32)


def _ada_kernel(c_ref, w_ref, b_ref, o_ref):
    s = _silu(c_ref[...]).astype(BF16)
    o_ref[...] = _dot(s, w_ref[...].astype(BF16)) + b_ref[...]


def _ada(cc, w_ada, b_ada):
    n = w_ada.shape[1]
    tn = 1024
    return pl.pallas_call(
        _ada_kernel,
        grid=(n // tn,),
        in_specs=[pl.BlockSpec((SUBLANES, D), lambda j: (0, 0)),
                  pl.BlockSpec((D, tn), lambda j: (0, j)),
                  pl.BlockSpec((1, tn), lambda j: (0, j))],
        out_specs=pl.BlockSpec((SUBLANES, tn), lambda j: (0, j)),
        out_shape=jax.ShapeDtypeStruct((SUBLANES, n), F32),
        compiler_params=pltpu.CompilerParams(dimension_semantics=("arbitrary",), vmem_limit_bytes=VMEM_LIMIT),
        name="ada",
    )(cc, w_ada, b_ada.reshape(1, n))


def _rope(blk, cos, sin):
    return blk * cos + pltpu.roll(blk, QK_ROPE, axis=1) * sin


def _kv_path(hb, win_ref, kvn_ref, wukv_ref, cos, sin, k_ref, v_ref):
    ckv_kp = _dot(hb, win_ref[:, C_KV:C_KV + KV_LORA + LANES])
    ckvn = _rms(ckv_kp[:, :KV_LORA], kvn_ref[...]).astype(BF16)
    kn = _dot(ckvn, wukv_ref[:, :HEADS * QK_NOPE])
    v_ref[...] = _dot(ckvn, wukv_ref[:, HEADS * QK_NOPE:]).astype(BF16)
    kpe = _rope(ckv_kp[:, KV_LORA:], cos, sin).astype(BF16)
    for h in range(HEADS):
        k_ref[:, h * QK_PAD:h * QK_PAD + QK_NOPE] = kn[:, h * QK_NOPE:(h + 1) * QK_NOPE].astype(BF16)
        k_ref[:, h * QK_PAD + QK_NOPE:(h + 1) * QK_PAD] = kpe


def _inproj_kernel(x_ref, mod_ref, pn_ref, win_ref, qn_ref, wuq_ref, kvn_ref, wukv_ref,
                   sgn_ref, sgw_ref, sgb_ref, cs_ref, q_ref, k_ref, v_ref, sgu_ref):
    tm = x_ref.shape[0]
    mod = mod_ref[0]
    h = _rms(x_ref[...], pn_ref[...]) * (1.0 + mod[1:2]) + mod[0:1]
    hb = h.astype(BF16)
    cos = cs_ref[:, :LANES]
    sin = cs_ref[:, LANES:]

    _kv_path(hb, win_ref, kvn_ref, wukv_ref, cos, sin, k_ref, v_ref)

    cq = _dot(hb, win_ref[:, C_Q:C_Q + Q_LORA])
    cqn = _rms(cq, qn_ref[...]).astype(BF16)
    for hp in range(HEADS // 2):
        qm2 = _dot(cqn, wuq_ref[:, hp * 2 * QK_PAD:(hp + 1) * 2 * QK_PAD])
        for i in range(2):
            h_ = 2 * hp + i
            qm = qm2[:, i * QK_PAD:(i + 1) * QK_PAD]
            q_ref[:, h_ * QK_PAD:h_ * QK_PAD + QK_NOPE] = (qm[:, :QK_NOPE] * Q_SCALE).astype(BF16)
            q_ref[:, h_ * QK_PAD + QK_NOPE:(h_ + 1) * QK_PAD] = (
                _rope(qm[:, QK_NOPE:], cos, sin) * Q_SCALE).astype(BF16)

    for gp in range(SGU_HEADS // 2):
        u2 = _gelu_tanh(_dot(hb, win_ref[:, C_U + gp * 2 * SGU_DIM:C_U + (gp + 1) * 2 * SGU_DIM]))
        vs2 = _gelu_tanh(_dot(hb, win_ref[:, C_V + gp * 2 * SGU_DIM:C_V + (gp + 1) * 2 * SGU_DIM]))
        for i in range(2):
            g = 2 * gp + i
            gs = slice(g * SGU_DIM, (g + 1) * SGU_DIM)
            u = u2[:, i * SGU_DIM:(i + 1) * SGU_DIM]
            vs = vs2[:, i * SGU_DIM:(i + 1) * SGU_DIM]
            mu = jnp.mean(vs, axis=-1, keepdims=True)
            vc = vs - mu
            var = jnp.mean(vc * vc, axis=-1, keepdims=True)
            vn = (vc * lax.rsqrt(var + EPS) * sgn_ref[:, gs]).astype(BF16)
            wg = sgw_ref[g]
            bias = sgb_ref[:, g:g + 1]
            for n in range(tm // CHUNK):
                rs = slice(n * CHUNK, (n + 1) * CHUNK)
                mixed = _dot(wg, vn[rs, :]) + bias
                sgu_ref[rs, gs] = (u[rs, :] * mixed).astype(BF16)


def _ctx_kv_kernel(x_ref, mod_ref, pn_ref, win_ref, kvn_ref, wukv_ref, cs_ref, k_ref, v_ref):
    mod = mod_ref[0]
    h = _rms(x_ref[...], pn_ref[...]) * (1.0 + mod[1:2]) + mod[0:1]
    _kv_path(h.astype(BF16), win_ref, kvn_ref, wukv_ref, cs_ref[:, :LANES], cs_ref[:, LANES:], k_ref, v_ref)


def _const_spec(shape):
    nd = len(shape)
    return pl.BlockSpec(shape, lambda i: (0,) * nd, pipeline_mode=pl.Buffered(1))


def _inproj(x2, mod3, pn, win, qn, wuq, kvn, wukv, sgn, sgw, sgb, cs, rows_per_batch, tm):
    n = x2.shape[0]
    tpb = rows_per_batch // tm
    row = lambda w: pl.BlockSpec((tm, w), lambda i: (i, 0))
    return pl.pallas_call(
        _inproj_kernel,
        grid=(n // tm,),
        in_specs=[row(D),
                  pl.BlockSpec((1, N_MOD, D), lambda i: (i // tpb, 0, 0)),
                  _const_spec((1, D)), _const_spec(win.shape), _const_spec((1, Q_LORA)),
                  _const_spec(wuq.shape), _const_spec((1, KV_LORA)), _const_spec(wukv.shape),
                  _const_spec((1, SGU_WIDTH)), _const_spec(sgw.shape), _const_spec(sgb.shape),
                  pl.BlockSpec((tm, 2 * LANES), lambda i: (i % tpb, 0))],
        out_specs=[row(HEADS * QK_PAD), row(HEADS * QK_PAD), row(MLA_WIDTH), row(SGU_WIDTH)],
        out_shape=[jax.ShapeDtypeStruct((n, HEADS * QK_PAD), BF16),
                   jax.ShapeDtypeStruct((n, HEADS * QK_PAD), BF16),
                   jax.ShapeDtypeStruct((n, MLA_WIDTH), BF16),
                   jax.ShapeDtypeStruct((n, SGU_WIDTH), BF16)],
        compiler_params=pltpu.CompilerParams(dimension_semantics=("arbitrary",), vmem_limit_bytes=VMEM_LIMIT),
        name="inproj",
    )(x2, mod3, pn, win, qn, wuq, kvn, wukv, sgn, sgw, sgb, cs)


def _ctx_kv(c2, mod3, ctx_row, pn, win, kvn, wukv, cs, tm):
    n = c2.shape[0]
    row = lambda w: pl.BlockSpec((tm, w), lambda i: (i, 0))
    return pl.pallas_call(
        _ctx_kv_kernel,
        grid=(n // tm,),
        in_specs=[row(D),
                  pl.BlockSpec((1, N_MOD, D), lambda i: (ctx_row, 0, 0)),
                  _const_spec((1, D)), _const_spec(win.shape), _const_spec((1, KV_LORA)),
                  _const_spec(wukv.shape),
                  pl.BlockSpec((tm, 2 * LANES), lambda i: (0, 0))],
        out_specs=[row(HEADS * QK_PAD), row(MLA_WIDTH)],
        out_shape=[jax.ShapeDtypeStruct((n, HEADS * QK_PAD), BF16),
                   jax.ShapeDtypeStruct((n, MLA_WIDTH), BF16)],
        compiler_params=pltpu.CompilerParams(dimension_semantics=("arbitrary",), vmem_limit_bytes=VMEM_LIMIT),
        name="ctx_kv",
    )(c2, mod3, pn, win, kvn, wukv, cs)


def _attn_kernel(q_ref, k_ref, v_ref, kc_ref, vc_ref, o_ref, *, tk, streams):
    tq = q_ref.shape[1]
    ts = tq // streams
    nk = k_ref.shape[1] // tk
    qs = [q_ref[0, s * ts:(s + 1) * ts, :] for s in range(streams)]

    def step(q, kb, vb, carry):
        m, l, acc = carry
        s = lax.dot_general(q, kb, (((1,), (1,)), ((), ())), preferred_element_type=F32)
        m_new = jnp.maximum(m, jnp.max(s, axis=-1, keepdims=True))
        alpha = jnp.exp2(m - m_new)
        p = jnp.exp2(s - m_new)
        l = alpha * l + jnp.sum(p, axis=-1, keepdims=True)
        acc = alpha * acc + _dot(p.astype(BF16), vb)
        return m_new, l, acc

    def body(j, carries):
        off = pl.multiple_of(j * tk, tk)
        kb = k_ref[0, pl.ds(off, tk), :]
        vb = v_ref[0, pl.ds(off, tk), :]
        return tuple(step(qs[s], kb, vb, carries[s]) for s in range(streams))

    init = (jnp.full((ts, 1), -jnp.inf, F32), jnp.zeros((ts, 1), F32), jnp.zeros((ts, V_HEAD), F32))
    carries = (init,) * streams
    for j in range(nk):
        carries = body(j, carries)
    for s in range(streams):
        m, l, acc = step(qs[s], kc_ref[0], vc_ref[0], carries[s])
        o_ref[0, s * ts:(s + 1) * ts, :] = (acc / l).astype(BF16)


def _attention(q, k, v, kc, vc, tq, tk):
    b, l, _ = q.shape
    lc = kc.shape[1]
    tk = min(tk, l)
    assert l % tk == 0 and l % tq == 0
    return pl.pallas_call(
        functools.partial(_attn_kernel, tk=tk, streams=1),
        grid=(b, HEADS, l // tq),
        in_specs=[pl.BlockSpec((1, tq, QK_PAD), lambda b_, h, i: (b_, i, h)),
                  pl.BlockSpec((1, l, QK_PAD), lambda b_, h, i: (b_, 0, h)),
                  pl.BlockSpec((1, l, V_HEAD), lambda b_, h, i: (b_, 0, h)),
                  pl.BlockSpec((1, lc, QK_PAD), lambda b_, h, i: (b_, 0, h)),
                  pl.BlockSpec((1, lc, V_HEAD), lambda b_, h, i: (b_, 0, h))],
        out_specs=pl.BlockSpec((1, tq, V_HEAD), lambda b_, h, i: (b_, i, h)),
        out_shape=jax.ShapeDtypeStruct((b, l, MLA_WIDTH), BF16),
        compiler_params=pltpu.CompilerParams(dimension_semantics=("arbitrary",) * 3, vmem_limit_bytes=VMEM_LIMIT),
        name="attn",
    )(q, k, v, kc, vc)


OUTPROJ_SUB = 256


def _outproj_kernel(at_ref, sg_ref, x_ref, mod_ref, wo_ref, pn1_ref, pn2_ref, wr_ref, br_ref,
                    xn_ref, h2_ref, aff_ref):
    tm = x_ref.shape[0]
    mod = mod_ref[0]
    for r0 in range(0, tm, OUTPROJ_SUB):
        rs = slice(r0, r0 + OUTPROJ_SUB)
        y = _dot(at_ref[rs, :], wo_ref[:MLA_WIDTH, :]) + _dot(sg_ref[rs, :], wo_ref[MLA_WIDTH:, :])
        xn = x_ref[rs, :] + mod[2:3] * _rms(y, pn1_ref[...])
        xn_ref[rs, :] = xn
        h2 = _rms(xn, pn2_ref[...]) * (1.0 + mod[4:5]) + mod[3:4]
        h2_ref[rs, :] = h2
        logits = _dot(h2.astype(BF16), wr_ref[...]) + br_ref[...]
        lane = lax.broadcasted_iota(I32, logits.shape, 1)
        logits = jnp.where(lane < N_EXPERTS, logits, -jnp.inf)
        e = jnp.exp(logits - jnp.max(logits, axis=-1, keepdims=True))
        aff = e / jnp.sum(e, axis=-1, keepdims=True)
        aff_ref[:, rs] = jnp.transpose(aff)[:N_EXPERTS, :]


def _outproj(attn2, sgu2, x2, mod3, wo, pn1, pn2, wr, br, rows_per_batch, tm):
    n = x2.shape[0]
    tpb = rows_per_batch // tm
    row = lambda w: pl.BlockSpec((tm, w), lambda i: (i, 0))
    return pl.pallas_call(
        _outproj_kernel,
        grid=(n // tm,),
        in_specs=[row(MLA_WIDTH), row(SGU_WIDTH), row(D),
                  pl.BlockSpec((1, N_MOD, D), lambda i: (i // tpb, 0, 0)),
                  _const_spec(wo.shape), _const_spec((1, D)), _const_spec((1, D)),
                  _const_spec(wr.shape), _const_spec((1, LANES))],
        out_specs=[row(D), row(D), pl.BlockSpec((N_EXPERTS, tm), lambda i: (0, i))],
        out_shape=[jax.ShapeDtypeStruct((n, D), F32),
                   jax.ShapeDtypeStruct((n, D), F32),
                   jax.ShapeDtypeStruct((N_EXPERTS, n), F32)],
        compiler_params=pltpu.CompilerParams(dimension_semantics=("arbitrary",), vmem_limit_bytes=VMEM_LIMIT),
        name="outproj",
    )(attn2, sgu2, x2, mod3, wo, pn1, pn2, wr, br)


def _prefix_count(x01):
    r, t = x01.shape
    nb = t // LANES
    stacked = jnp.concatenate([x01[:, k * LANES:(k + 1) * LANES] for k in range(nb)], axis=0).astype(BF16)
    ii = lax.broadcasted_iota(I32, (LANES, LANES), 0)
    jj = lax.broadcasted_iota(I32, (LANES, LANES), 1)
    tri = jnp.where(ii <= jj, 1.0, 0.0).astype(BF16)
    within = _dot(stacked, tri)
    off = jnp.zeros((r, 1), F32)
    blocks = []
    for k in range(nb):
        w = within[k * r:(k + 1) * r, :]
        blocks.append(w + off)
        off = off + w[:, LANES - 1:LANES]
    return jnp.concatenate(blocks, axis=1)


ROUTE_SLOTS = 128
ROUTE_SEARCH_STEPS = 192


def _route_kernel(aff_ref, idx_ref, gate_ref, incl_s, isel_s, aff_s, *, cap, lt):
    aff = jnp.concatenate([aff_ref[:, b * lt:(b + 1) * lt] for b in range(aff_ref.shape[1] // lt)], axis=0)
    nr = aff.shape[0]
    capf = float(cap)

    def search(i, c):
        lo, hi = c
        mid = 0.5 * (lo + hi)
        ge = jnp.sum(jnp.where(aff >= mid, 1.0, 0.0), axis=1, keepdims=True) >= capf
        return jnp.where(ge, mid, lo), jnp.where(ge, hi, mid)

    lo, _ = lax.fori_loop(0, ROUTE_SEARCH_STEPS, search,
                          (jnp.zeros((nr, 1), F32), jnp.full((nr, 1), 2.0, F32)))
    thr = jnp.min(jnp.where(aff >= lo, aff, jnp.inf), axis=1, keepdims=True)
    gt = aff > thr
    eq = aff == thr
    n_gt = jnp.sum(jnp.where(gt, 1.0, 0.0), axis=1, keepdims=True)
    eq_rank = _prefix_count(jnp.where(eq, 1.0, 0.0))
    sel = jnp.where(gt, 1.0, jnp.where(eq, jnp.where(eq_rank <= capf - n_gt, 1.0, 0.0), 0.0))
    incl = _prefix_count(sel)
    isel = incl * sel
    for r in range(nr):
        incl_s[r] = incl[r:r + 1, :]
        isel_s[r] = isel[r:r + 1, :]
        aff_s[r] = aff[r:r + 1, :]

    ns = min(cap, ROUTE_SLOTS)

    def compact(r, _):
        for c0 in range(0, cap, ns):
            slot = (lax.broadcasted_iota(I32, (ns, LANES), 0) + c0).astype(F32)
            cnt = jnp.zeros((ns, LANES), F32)
            gat = jnp.zeros((ns, LANES), F32)
            for k in range(lt // LANES):
                ks = slice(k * LANES, (k + 1) * LANES)
                cnt = cnt + jnp.where(incl_s[r, :, ks] <= slot, 1.0, 0.0)
                gat = gat + jnp.where(isel_s[r, :, ks] == slot + 1.0, aff_s[r, :, ks], 0.0)
            tot = lax.dot_general(jnp.ones((SUBLANES, LANES), BF16), cnt.astype(BF16),
                                  (((1,), (1,)), ((), ())), preferred_element_type=F32)
            idx_ref[r, :, c0:c0 + ns] = tot[0:1, :].astype(I32)
            gate_ref[r, c0:c0 + ns, :] = jnp.sum(gat, axis=1, keepdims=True)
        return 0

    lax.fori_loop(0, nr, compact, 0)


def _route(aff_em, lt, cap):
    nr = aff_em.shape[0] * (aff_em.shape[1] // lt)
    return pl.pallas_call(
        functools.partial(_route_kernel, cap=cap, lt=lt),
        out_shape=[jax.ShapeDtypeStruct((nr, 1, cap), I32),
                   jax.ShapeDtypeStruct((nr, cap, 1), F32)],
        scratch_shapes=[pltpu.VMEM((nr, 1, lt), F32)] * 3,
        compiler_params=pltpu.CompilerParams(vmem_limit_bytes=VMEM_LIMIT),
        name="route",
    )(aff_em)


BATCH_GROUP = 4
WEIGHT_DMA_PRIORITY = 1


def _ffn_kernel(idx_ref, h2_ref, gate_ref, wg_ref, wu_ref, wd_ref, o_ref, ring, lhs, acc, wgs, wus, wds,
                wgb, wub, wdb, sems, wsem, *, cap, seq, chunk):
    p = pl.program_id(0)
    e = pl.program_id(1)
    f = pl.program_id(2)
    bb = pl.program_id(3)
    ne = pl.num_programs(1)
    nf = pl.num_programs(2)
    steps = nf * BATCH_GROUP
    cpb = cap // chunk
    n_items = pl.num_programs(0) * ne
    w = p * ne + e
    s = f * BATCH_GROUP + bb
    g = w * steps + s

    def issue_chunk(item, k, slot):
        bi = lax.div(item, ne) * BATCH_GROUP + lax.div(k, cpb)
        base = (bi * ne + lax.rem(item, ne)) * cap + lax.rem(k, cpb) * chunk
        for i in range(chunk):
            tok = bi * seq + idx_ref[base + i]
            pltpu.make_async_copy(h2_ref.at[pl.ds(tok, 1)], ring.at[slot, pl.ds(i, 1)],
                                  sems.at[slot]).start(priority=i % 2)

    def wait_chunk(slot):
        pltpu.make_async_copy(h2_ref.at[pl.ds(0, chunk)], ring.at[slot], sems.at[slot]).wait()

    def drain_chunk(slot, par, k):
        wait_chunk(slot)
        r0 = pl.multiple_of(lax.rem(k, cpb) * chunk, chunk)
        lhs[par, lax.div(k, cpb), pl.ds(r0, chunk), :] = ring[slot].astype(BF16)

    tf = wgb.shape[1]

    def weight_copies(tile):
        ei = lax.rem(lax.div(tile, nf), ne)
        c0 = pl.multiple_of(lax.rem(tile, nf) * tf, tf)
        return (pltpu.make_async_copy(wg_ref.at[ei, :, pl.ds(c0, tf)], wgs, wsem.at[0]),
                pltpu.make_async_copy(wu_ref.at[ei, :, pl.ds(c0, tf)], wus, wsem.at[1]),
                pltpu.make_async_copy(wd_ref.at[ei, pl.ds(c0, tf), :], wds, wsem.at[2]))

    @pl.when(g == 0)
    def _prologue():
        for cp in weight_copies(0):
            cp.start(priority=WEIGHT_DMA_PRIORITY)

        def fetch(k, _):
            issue_chunk(0, k, 0)
            drain_chunk(0, 0, k)
            return 0
        lax.fori_loop(0, steps - 1, fetch, 0)
        issue_chunk(0, steps - 1, 1)

    gq = g - 1 + steps
    drain_chunk(lax.rem(g + 1, 2), lax.rem(lax.div(gq, steps), 2), lax.rem(gq, steps))

    @pl.when(bb == 0)
    def _next_weights():
        tile = w * nf + f
        for cp in weight_copies(tile):
            cp.wait()
        wgb[...] = wgs[...].astype(BF16)
        wub[...] = wus[...].astype(BF16)
        wdb[...] = wds[...].astype(BF16)

        @pl.when(tile + 1 < n_items * nf)
        def _prefetch():
            for cp in weight_copies(tile + 1):
                cp.start(priority=WEIGHT_DMA_PRIORITY)

    @pl.when(f == 0)
    def _zero():
        acc[bb] = jnp.zeros((cap, D), F32)

    issue_chunk(jnp.minimum(w + 1, n_items - 1), s, lax.rem(g, 2))
    x = lhs[lax.rem(w, 2), bb]
    a = _dot(x, wgb[...])
    gg = _dot(x, wub[...])
    hm = (_silu(a) * gg).astype(BF16)
    acc[bb] += _dot(hm, wdb[...])

    @pl.when(f == nf - 1)
    def _emit():
        gate = gate_ref[0]
        for j in range(ROW_TILES):
            o_ref[pl.ds(j, cap, stride=ROW_TILES), :] = acc[bb, :, j * LANES:(j + 1) * LANES] * gate

    @pl.when(g == n_items * steps - 1)
    def _tail():
        wait_chunk(lax.rem(g, 2))


def _ffn(idx_flat, h2, gate_col, w_gate, w_up, w_down, nb, seq, cap, tf):
    ne, _, ff = w_gate.shape
    nf = ff // tf
    chunk = cap // nf
    assert nb % BATCH_GROUP == 0 and cap % nf == 0 and chunk % 16 == 0
    pair_block = lambda p, e, f, bb, idx: ((p * BATCH_GROUP + jnp.where(f == nf - 1, bb, 0)) * ne + e, 0)
    grid_spec = pltpu.PrefetchScalarGridSpec(
        num_scalar_prefetch=1,
        grid=(nb // BATCH_GROUP, ne, nf, BATCH_GROUP),
        in_specs=[pl.BlockSpec(memory_space=pl.ANY),
                  pl.BlockSpec((1, cap, 1), lambda p, e, f, bb, idx: pair_block(p, e, f, bb, idx) + (0,)),
                  pl.BlockSpec(memory_space=pl.ANY),
                  pl.BlockSpec(memory_space=pl.ANY),
                  pl.BlockSpec(memory_space=pl.ANY)],
        out_specs=pl.BlockSpec((cap * ROW_TILES, LANES), pair_block),
        scratch_shapes=[pltpu.VMEM((2, chunk, D), F32),
                        pltpu.VMEM((2, BATCH_GROUP, cap, D), BF16),
                        pltpu.VMEM((BATCH_GROUP, cap, D), F32),
                        pltpu.VMEM((D, tf), F32),
                        pltpu.VMEM((D, tf), F32),
                        pltpu.VMEM((tf, D), F32),
                        pltpu.VMEM((D, tf), BF16),
                        pltpu.VMEM((D, tf), BF16),
                        pltpu.VMEM((tf, D), BF16),
                        pltpu.SemaphoreType.DMA((2,)),
                        pltpu.SemaphoreType.DMA((3,))],
    )
    return pl.pallas_call(
        functools.partial(_ffn_kernel, cap=cap, seq=seq, chunk=chunk),
        grid_spec=grid_spec,
        out_shape=jax.ShapeDtypeStruct((nb * ne * cap * ROW_TILES, LANES), F32),
        compiler_params=pltpu.CompilerParams(dimension_semantics=("arbitrary",) * 4, vmem_limit_bytes=VMEM_LIMIT),
        name="ffn",
    )(idx_flat, h2, gate_col, w_gate, w_up, w_down)


COMBINE_UNROLL = 8


def _combine_kernel(idx_ref, ye_ref, y_ref, *, cap):
    b = pl.program_id(0)
    e = pl.program_id(1)
    pair = b * pl.num_programs(1) + e

    @pl.when(e == 0)
    def _zero():
        y_ref[...] = jnp.zeros(y_ref.shape, F32)

    def group(gi, _):
        base = gi * COMBINE_UNROLL
        toks = [idx_ref[pair * cap + base + u] for u in range(COMBINE_UNROLL)]
        rows = [y_ref[toks[u]] + ye_ref[base + u] for u in range(COMBINE_UNROLL)]
        for u in range(COMBINE_UNROLL):
            y_ref[toks[u]] = rows[u]
        return 0

    lax.fori_loop(0, cap // COMBINE_UNROLL, group, 0)


def _combine(idx_flat, ye, nb, seq, cap):
    ne = ye.shape[0] // (nb * cap * ROW_TILES)
    ye4 = ye.reshape(nb * ne, cap, ROW_TILES, LANES)
    grid_spec = pltpu.PrefetchScalarGridSpec(
        num_scalar_prefetch=1,
        grid=(nb, ne),
        in_specs=[pl.BlockSpec((None, cap, ROW_TILES, LANES), lambda b, e, idx: (b * ne + e, 0, 0, 0))],
        out_specs=pl.BlockSpec((None, seq, ROW_TILES, LANES), lambda b, e, idx: (b, 0, 0, 0),
                               pipeline_mode=pl.Buffered(1)),
    )
    return pl.pallas_call(
        functools.partial(_combine_kernel, cap=cap),
        grid_spec=grid_spec,
        out_shape=jax.ShapeDtypeStruct((nb, seq, ROW_TILES, LANES), F32),
        compiler_params=pltpu.CompilerParams(dimension_semantics=("arbitrary",) * 2, vmem_limit_bytes=VMEM_LIMIT),
        name="combine",
    )(idx_flat, ye4)


def _final_kernel(y_ref, xn_ref, mod_ref, pn_ref, o_ref):
    tm = xn_ref.shape[0]
    g2 = mod_ref[0][5:6]
    ss = jnp.zeros((tm, LANES), F32)
    for j in range(ROW_TILES):
        v = y_ref[pl.ds(j, tm, stride=ROW_TILES), :]
        ss = ss + v * v
    rs = lax.rsqrt(jnp.sum(ss, axis=-1, keepdims=True) * (1.0 / D) + EPS)
    for j in range(ROW_TILES):
        cs = slice(j * LANES, (j + 1) * LANES)
        v = y_ref[pl.ds(j, tm, stride=ROW_TILES), :]
        o_ref[:, cs] = xn_ref[:, cs] + g2[:, cs] * (v * rs * pn_ref[:, cs])


def _final(y2_rows, xn, mod3, pn, rows_per_batch, tm):
    n = xn.shape[0]
    tpb = rows_per_batch // tm
    return pl.pallas_call(
        _final_kernel,
        grid=(n // tm,),
        in_specs=[pl.BlockSpec((tm * ROW_TILES, LANES), lambda i: (i, 0)),
                  pl.BlockSpec((tm, D), lambda i: (i, 0)),
                  pl.BlockSpec((1, N_MOD, D), lambda i: (i // tpb, 0, 0)),
                  _const_spec((1, D))],
        out_specs=pl.BlockSpec((tm, D), lambda i: (i, 0)),
        out_shape=jax.ShapeDtypeStruct((n, D), F32),
        compiler_params=pltpu.CompilerParams(dimension_semantics=("arbitrary",)),
        name="final",
    )(y2_rows, xn, mod3, pn)


def _rope_partner():
    q = QK_ROPE // 4
    return np.concatenate([np.arange(q, 2 * q), np.arange(0, q), np.arange(3 * q, 4 * q), np.arange(2 * q, 3 * q)])


def _rope_table(length):
    pos = np.arange(length)
    half = QK_ROPE // 2
    inv = (1.0 / (ROPE_THETA ** (np.arange(0, half, 2, dtype=np.float32) / half))).astype(np.float32)
    ar = (pos // GRID_W).astype(np.float32)[:, None] * inv
    ac = (pos % GRID_W).astype(np.float32)[:, None] * inv
    cos = np.concatenate([np.cos(ar), np.cos(ar), np.cos(ac), np.cos(ac)], axis=1)
    sin = np.concatenate([-np.sin(ar), np.sin(ar), -np.sin(ac), np.sin(ac)], axis=1)
    z = np.zeros((length, LANES - QK_ROPE), np.float32)
    return jnp.asarray(np.concatenate([cos, z, sin, z], axis=1).astype(np.float32))


def _identity_rope_table(length):
    t = np.zeros((length, 2 * LANES), np.float32)
    t[:, :QK_ROPE] = 1.0
    return jnp.asarray(t)


def _prep_w_in(w_in):
    perm = _rope_partner()
    kpe = w_in[:, 2 * Q_LORA:2 * Q_LORA + QK_ROPE]
    rest = w_in[:, 2 * Q_LORA + QK_ROPE:]
    return jnp.concatenate([w_in[:, :2 * Q_LORA], kpe, kpe[:, perm], rest], axis=1).astype(BF16)


def _prep_w_uq(w_uq):
    perm = _rope_partner()
    w = w_uq.reshape(Q_LORA, HEADS, QK_NOPE + QK_ROPE)
    return jnp.concatenate([w, w[:, :, QK_NOPE:][:, :, perm]], axis=2).reshape(Q_LORA, HEADS * QK_PAD).astype(BF16)


def _prep_w_ukv(w_ukv):
    w = w_ukv.reshape(KV_LORA, HEADS, 2, QK_NOPE)
    return w.transpose(0, 2, 1, 3).reshape(KV_LORA, 2 * HEADS * QK_NOPE).astype(BF16)


def kernel(x, c, ctx, c_ctx, w_ada, b_ada, pre_norm1, w_in, q_norm_w, w_uq, kv_norm_w, w_ukv, sgu_norm_w, sgu_w,
           sgu_b, w_out, post_norm1, pre_norm2, w_router, b_router, w_e_gate, w_e_up, w_e_down, post_norm2):
    nb, seq, _ = x.shape
    lc = ctx.shape[1]
    depth = w_ada.shape[0]
    assert depth == 1 and seq % 512 == 0 and lc % 128 == 0 and nb < SUBLANES
    cap = CAP_FACTOR * seq // N_EXPERTS
    n = nb * seq
    tm = 256

    cc = jnp.zeros((SUBLANES, D), F32).at[:nb].set(c).at[nb].set(c_ctx)
    mod3 = _ada(cc, w_ada[0], b_ada[0]).reshape(SUBLANES, N_MOD, D)

    row = lambda w: w.reshape(1, -1)
    win = _prep_w_in(w_in[0])
    wuq = _prep_w_uq(w_uq[0])
    wukv = _prep_w_ukv(w_ukv[0])
    x2 = x.reshape(n, D)
    q, k, v, sgu = _inproj(x2, mod3, row(pre_norm1[0]), win, row(q_norm_w[0]), wuq, row(kv_norm_w[0]), wukv,
                           row(sgu_norm_w[0]), sgu_w[0].astype(BF16), sgu_b[0].T, _rope_table(seq), seq, 2 * tm)
    kc, vc = _ctx_kv(ctx.reshape(nb * lc, D), mod3, nb, row(pre_norm1[0]), win, row(kv_norm_w[0]), wukv,
                     _identity_rope_table(lc), lc)

    attn = _attention(q.reshape(nb, seq, -1), k.reshape(nb, seq, -1), v.reshape(nb, seq, -1),
                      kc.reshape(nb, lc, -1), vc.reshape(nb, lc, -1), tq=512, tk=4096)

    wr = jnp.zeros((D, LANES), BF16).at[:, :N_EXPERTS].set(w_router[0].astype(BF16))
    br = jnp.zeros((1, LANES), F32).at[0, :N_EXPERTS].set(b_router[0])
    xn, h2_rows, aff = _outproj(attn.reshape(n, MLA_WIDTH), sgu, x2, mod3, w_out[0].astype(BF16),
                                row(post_norm1[0]), row(pre_norm2[0]), wr, br, seq, 2 * tm)

    idx_col, gate_col = _route(aff, seq, cap)
    idx_flat = idx_col.reshape(-1)

    ye = _ffn(idx_flat, h2_rows, gate_col, w_e_gate[0], w_e_up[0], w_e_down[0], nb, seq, cap, tf=256)
    y2 = _combine(idx_flat, ye, nb, seq, cap)
    out = _final(y2.reshape(n * ROW_TILES, LANES), xn, mod3, row(post_norm2[0]), seq, tm)
    return out.reshape(nb, seq, D)
```

```python
import functools

import numpy as np
import jax
import jax.numpy as jnp
from jax import lax
from jax.experimental import pallas as pl
from jax.experimental.pallas import tpu as pltpu

F32 = jnp.float32
BF16 = jnp.bfloat16
I32 = jnp.int32

D = 2048
GRID_W = 64
EPS = 1e-6
N_MOD = 6
HEADS = 8
Q_LORA = 512
KV_LORA = 512
QK_NOPE = 128
QK_ROPE = 64
V_HEAD = 128
ROPE_THETA = 10000.0
ATTN_SCALE = (QK_NOPE + QK_ROPE) ** -0.5
Q_SCALE = ATTN_SCALE * float(np.log2(np.e))
SGU_HEADS = 8
SGU_DIM = 128
CHUNK = 128
MLA_WIDTH = HEADS * V_HEAD
SGU_WIDTH = SGU_HEADS * SGU_DIM
N_EXPERTS = 16
CAP_FACTOR = 2
EXPERT_FF = D

LANES = 128
SUBLANES = 8
ROW_TILES = D // LANES
QK_PAD = 256
VMEM_LIMIT = 56 * 1024 * 1024

C_Q = 0
C_KV = C_Q + Q_LORA
C_KPE = C_KV + KV_LORA
C_U = C_KPE + LANES
C_V = C_U + SGU_WIDTH
IN_EXT = C_V + SGU_WIDTH


def _rms(x, w):
    return x * lax.rsqrt(jnp.mean(x * x, axis=-1, keepdims=True) + EPS) * w


def _gelu_tanh(x):
    return 0.5 * x * (1.0 + jnp.tanh(np.sqrt(2.0 / np.pi).astype(np.float32) * (x + 0.044715 * (x * x * x))))


def _silu(x):
    return x * (1.0 / (1.0 + jnp.exp(-x)))


def _dot(a, b):
    return jnp.dot(a, b, preferred_element_type=F32)


def _ada_kernel(c_ref, w_ref, b_ref, o_ref):
    s = _silu(c_ref[...]).astype(BF16)
    o_ref[...] = _dot(s, w_ref[...].astype(BF16)) + b_ref[...]


def _ada(cc, w_ada, b_ada):
    n = w_ada.shape[1]
    tn = 1024
    return pl.pallas_call(
        _ada_kernel,
        grid=(n // tn,),
        in_specs=[pl.BlockSpec((SUBLANES, D), lambda j: (0, 0)),
                  pl.BlockSpec((D, tn), lambda j: (0, j)),
                  pl.BlockSpec((1, tn), lambda j: (0, j))],
        out_specs=pl.BlockSpec((SUBLANES, tn), lambda j: (0, j)),
        out_shape=jax.ShapeDtypeStruct((SUBLANES, n), F32),
        compiler_params=pltpu.CompilerParams(dimension_semantics=("arbitrary",), vmem_limit_bytes=VMEM_LIMIT),
        name="ada",
    )(cc, w_ada, b_ada.reshape(1, n))


def _rope(blk, cos, sin):
    return blk * cos + pltpu.roll(blk, QK_ROPE, axis=1) * sin


def _kv_path(hb, win_ref, kvn_ref, wukv_ref, cos, sin, k_ref, v_ref):
    ckv_kp = _dot(hb, win_ref[:, C_KV:C_KV + KV_LORA + LANES])
    ckvn = _rms(ckv_kp[:, :KV_LORA], kvn_ref[...]).astype(BF16)
    kn = _dot(ckvn, wukv_ref[:, :HEADS * QK_NOPE])
    v_ref[...] = _dot(ckvn, wukv_ref[:, HEADS * QK_NOPE:]).astype(BF16)
    kpe = _rope(ckv_kp[:, KV_LORA:], cos, sin).astype(BF16)
    for h in range(HEADS):
        k_ref[:, h * QK_PAD:h * QK_PAD + QK_NOPE] = kn[:, h * QK_NOPE:(h + 1) * QK_NOPE].astype(BF16)
        k_ref[:, h * QK_PAD + QK_NOPE:(h + 1) * QK_PAD] = kpe


def _inproj_kernel(x_ref, mod_ref, pn_ref, win_ref, qn_ref, wuq_ref, kvn_ref, wukv_ref,
                   sgn_ref, sgw_ref, sgb_ref, cs_ref, q_ref, k_ref, v_ref, sgu_ref):
    tm = x_ref.shape[0]
    mod = mod_ref[0]
    h = _rms(x_ref[...], pn_ref[...]) * (1.0 + mod[1:2]) + mod[0:1]
    hb = h.astype(BF16)
    cos = cs_ref[:, :LANES]
    sin = cs_ref[:, LANES:]

    _kv_path(hb, win_ref, kvn_ref, wukv_ref, cos, sin, k_ref, v_ref)

    cq = _dot(hb, win_ref[:, C_Q:C_Q + Q_LORA])
    cqn = _rms(cq, qn_ref[...]).astype(BF16)
    for hp in range(HEADS // 2):
        qm2 = _dot(cqn, wuq_ref[:, hp * 2 * QK_PAD:(hp + 1) * 2 * QK_PAD])
        for i in range(2):
            h_ = 2 * hp + i
            qm = qm2[:, i * QK_PAD:(i + 1) * QK_PAD]
            q_ref[:, h_ * QK_PAD:h_ * QK_PAD + QK_NOPE] = (qm[:, :QK_NOPE] * Q_SCALE).astype(BF16)
            q_ref[:, h_ * QK_PAD + QK_NOPE:(h_ + 1) * QK_PAD] = (
                _rope(qm[:, QK_NOPE:], cos, sin) * Q_SCALE).astype(BF16)

    for gp in range(SGU_HEADS // 2):
        u2 = _gelu_tanh(_dot(hb, win_ref[:, C_U + gp * 2 * SGU_DIM:C_U + (gp + 1) * 2 * SGU_DIM]))
        vs2 = _gelu_tanh(_dot(hb, win_ref[:, C_V + gp * 2 * SGU_DIM:C_V + (gp + 1) * 2 * SGU_DIM]))
        for i in range(2):
            g = 2 * gp + i
            gs = slice(g * SGU_DIM, (g + 1) * SGU_DIM)
            u = u2[:, i * SGU_DIM:(i + 1) * SGU_DIM]
            vs = vs2[:, i * SGU_DIM:(i + 1) * SGU_DIM]
            mu = jnp.mean(vs, axis=-1, keepdims=True)
            vc = vs - mu
            var = jnp.mean(vc * vc, axis=-1, keepdims=True)
            vn = (vc * lax.rsqrt(var + EPS) * sgn_ref[:, gs]).astype(BF16)
            wg = sgw_ref[g]
            bias = sgb_ref[:, g:g + 1]
            for n in range(tm // CHUNK):
                rs = slice(n * CHUNK, (n + 1) * CHUNK)
                mixed = _dot(wg, vn[rs, :]) + bias
                sgu_ref[rs, gs] = (u[rs, :] * mixed).astype(BF16)


def _ctx_kv_kernel(x_ref, mod_ref, pn_ref, win_ref, kvn_ref, wukv_ref, cs_ref, k_ref, v_ref):
    mod = mod_ref[0]
    h = _rms(x_ref[...], pn_ref[...]) * (1.0 + mod[1:2]) + mod[0:1]
    _kv_path(h.astype(BF16), win_ref, kvn_ref, wukv_ref, cs_ref[:, :LANES], cs_ref[:, LANES:], k_ref, v_ref)


def _const_spec(shape):
    nd = len(shape)
    return pl.BlockSpec(shape, lambda i: (0,) * nd, pipeline_mode=pl.Buffered(1))


def _inproj(x2, mod3, pn, win, qn, wuq, kvn, wukv, sgn, sgw, sgb, cs, rows_per_batch, tm):
    n = x2.shape[0]
    tpb = rows_per_batch // tm
    row = lambda w: pl.BlockSpec((tm, w), lambda i: (i, 0))
    return pl.pallas_call(
        _inproj_kernel,
        grid=(n // tm,),
        in_specs=[row(D),
                  pl.BlockSpec((1, N_MOD, D), lambda i: (i // tpb, 0, 0)),
                  _const_spec((1, D)), _const_spec(win.shape), _const_spec((1, Q_LORA)),
                  _const_spec(wuq.shape), _const_spec((1, KV_LORA)), _const_spec(wukv.shape),
                  _const_spec((1, SGU_WIDTH)), _const_spec(sgw.shape), _const_spec(sgb.shape),
                  pl.BlockSpec((tm, 2 * LANES), lambda i: (i % tpb, 0))],
        out_specs=[row(HEADS * QK_PAD), row(HEADS * QK_PAD), row(MLA_WIDTH), row(SGU_WIDTH)],
        out_shape=[jax.ShapeDtypeStruct((n, HEADS * QK_PAD), BF16),
                   jax.ShapeDtypeStruct((n, HEADS * QK_PAD), BF16),
                   jax.ShapeDtypeStruct((n, MLA_WIDTH), BF16),
                   jax.ShapeDtypeStruct((n, SGU_WIDTH), BF16)],
        compiler_params=pltpu.CompilerParams(dimension_semantics=("arbitrary",), vmem_limit_bytes=VMEM_LIMIT),
        name="inproj",
    )(x2, mod3, pn, win, qn, wuq, kvn, wukv, sgn, sgw, sgb, cs)


def _ctx_kv(c2, mod3, ctx_row, pn, win, kvn, wukv, cs, tm):
    n = c2.shape[0]
    row = lambda w: pl.BlockSpec((tm, w), lambda i: (i, 0))
    return pl.pallas_call(
        _ctx_kv_kernel,
        grid=(n // tm,),
        in_specs=[row(D),
                  pl.BlockSpec((1, N_MOD, D), lambda i: (ctx_row, 0, 0)),
                  _const_spec((1, D)), _const_spec(win.shape), _const_spec((1, KV_LORA)),
                  _const_spec(wukv.shape),
                  pl.BlockSpec((tm, 2 * LANES), lambda i: (0, 0))],
        out_specs=[row(HEADS * QK_PAD), row(MLA_WIDTH)],
        out_shape=[jax.ShapeDtypeStruct((n, HEADS * QK_PAD), BF16),
                   jax.ShapeDtypeStruct((n, MLA_WIDTH), BF16)],
        compiler_params=pltpu.CompilerParams(dimension_semantics=("arbitrary",), vmem_limit_bytes=VMEM_LIMIT),
        name="ctx_kv",
    )(c2, mod3, pn, win, kvn, wukv, cs)


def _attn_kernel(q_ref, k_ref, v_ref, kc_ref, vc_ref, o_ref, *, tk, streams):
    tq = q_ref.shape[1]
    ts = tq // streams
    nk = k_ref.shape[1] // tk
    qs = [q_ref[0, s * ts:(s + 1) * ts, :] for s in range(streams)]

    def step(q, kb, vb, carry):
        m, l, acc = carry
        s = lax.dot_general(q, kb, (((1,), (1,)), ((), ())), preferred_element_type=F32)
        m_new = jnp.maximum(m, jnp.max(s, axis=-1, keepdims=True))
        alpha = jnp.exp2(m - m_new)
        p = jnp.exp2(s - m_new)
        l = alpha * l + jnp.sum(p, axis=-1, keepdims=True)
        acc = alpha * acc + _dot(p.astype(BF16), vb)
        return m_new, l, acc

    def body(j, carries):
        off = pl.multiple_of(j * tk, tk)
        kb = k_ref[0, pl.ds(off, tk), :]
        vb = v_ref[0, pl.ds(off, tk), :]
        return tuple(step(qs[s], kb, vb, carries[s]) for s in range(streams))

    init = (jnp.full((ts, 1), -jnp.inf, F32), jnp.zeros((ts, 1), F32), jnp.zeros((ts, V_HEAD), F32))
    carries = (init,) * streams
    for j in range(nk):
        carries = body(j, carries)
    for s in range(streams):
        m, l, acc = step(qs[s], kc_ref[0], vc_ref[0], carries[s])
        o_ref[0, s * ts:(s + 1) * ts, :] = (acc / l).astype(BF16)


def _attention(q, k, v, kc, vc, tq, tk):
    b, l, _ = q.shape
    lc = kc.shape[1]
    tk = min(tk, l)
    assert l % tk == 0 and l % tq == 0
    return pl.pallas_call(
        functools.partial(_attn_kernel, tk=tk, streams=1),
        grid=(b, HEADS, l // tq),
        in_specs=[pl.BlockSpec((1, tq, QK_PAD), lambda b_, h, i: (b_, i, h)),
                  pl.BlockSpec((1, l, QK_PAD), lambda b_, h, i: (b_, 0, h)),
                  pl.BlockSpec((1, l, V_HEAD), lambda b_, h, i: (b_, 0, h)),
                  pl.BlockSpec((1, lc, QK_PAD), lambda b_, h, i: (b_, 0, h)),
                  pl.BlockSpec((1, lc, V_HEAD), lambda b_, h, i: (b_, 0, h))],
        out_specs=pl.BlockSpec((1, tq, V_HEAD), lambda b_, h, i: (b_, i, h)),
        out_shape=jax.ShapeDtypeStruct((b, l, MLA_WIDTH), BF16),
        compiler_params=pltpu.CompilerParams(dimension_semantics=("arbitrary",) * 3, vmem_limit_bytes=VMEM_LIMIT),
        name="attn",
    )(q, k, v, kc, vc)


OUTPROJ_SUB = 256


def _outproj_kernel(at_ref, sg_ref, x_ref, mod_ref, wo_ref, pn1_ref, pn2_ref, wr_ref, br_ref,
                    xn_ref, h2_ref, aff_ref):
    tm = x_ref.shape[0]
    mod = mod_ref[0]
    for r0 in range(0, tm, OUTPROJ_SUB):
        rs = slice(r0, r0 + OUTPROJ_SUB)
        y = _dot(at_ref[rs, :], wo_ref[:MLA_WIDTH, :]) + _dot(sg_ref[rs, :], wo_ref[MLA_WIDTH:, :])
        xn = x_ref[rs, :] + mod[2:3] * _rms(y, pn1_ref[...])
        xn_ref[rs, :] = xn
        h2 = _rms(xn, pn2_ref[...]) * (1.0 + mod[4:5]) + mod[3:4]
        h2_ref[rs, :] = h2
        logits = _dot(h2.astype(BF16), wr_ref[...]) + br_ref[...]
        lane = lax.broadcasted_iota(I32, logits.shape, 1)
        logits = jnp.where(lane < N_EXPERTS, logits, -jnp.inf)
        e = jnp.exp(logits - jnp.max(logits, axis=-1, keepdims=True))
        aff = e / jnp.sum(e, axis=-1, keepdims=True)
        aff_ref[:, rs] = jnp.transpose(aff)[:N_EXPERTS, :]


def _outproj(attn2, sgu2, x2, mod3, wo, pn1, pn2, wr, br, rows_per_batch, tm):
    n = x2.shape[0]
    tpb = rows_per_batch // tm
    row = lambda w: pl.BlockSpec((tm, w), lambda i: (i, 0))
    return pl.pallas_call(
        _outproj_kernel,
        grid=(n // tm,),
        in_specs=[row(MLA_WIDTH), row(SGU_WIDTH), row(D),
                  pl.BlockSpec((1, N_MOD, D), lambda i: (i // tpb, 0, 0)),
                  _const_spec(wo.shape), _const_spec((1, D)), _const_spec((1, D)),
                  _const_spec(wr.shape), _const_spec((1, LANES))],
        out_specs=[row(D), row(D), pl.BlockSpec((N_EXPERTS, tm), lambda i: (0, i))],
        out_shape=[jax.ShapeDtypeStruct((n, D), F32),
                   jax.ShapeDtypeStruct((n, D), F32),
                   jax.ShapeDtypeStruct((N_EXPERTS, n), F32)],
        compiler_params=pltpu.CompilerParams(dimension_semantics=("arbitrary",), vmem_limit_bytes=VMEM_LIMIT),
        name="outproj",
    )(attn2, sgu2, x2, mod3, wo, pn1, pn2, wr, br)


def _prefix_count(x01):
    r, t = x01.shape
    nb = t // LANES
    stacked = jnp.concatenate([x01[:, k * LANES:(k + 1) * LANES] for k in range(nb)], axis=0).astype(BF16)
    ii = lax.broadcasted_iota(I32, (LANES, LANES), 0)
    jj = lax.broadcasted_iota(I32, (LANES, LANES), 1)
    tri = jnp.where(ii <= jj, 1.0, 0.0).astype(BF16)
    within = _dot(stacked, tri)
    off = jnp.zeros((r, 1), F32)
    blocks = []
    for k in range(nb):
        w = within[k * r:(k + 1) * r, :]
        blocks.append(w + off)
        off = off + w[:, LANES - 1:LANES]
    return jnp.concatenate(blocks, axis=1)


ROUTE_SLOTS = 128
ROUTE_SEARCH_STEPS = 192


def _route_kernel(aff_ref, idx_ref, gate_ref, incl_s, isel_s, aff_s, *, cap, lt):
    aff = jnp.concatenate([aff_ref[:, b * lt:(b + 1) * lt] for b in range(aff_ref.shape[1] // lt)], axis=0)
    nr = aff.shape[0]
    capf = float(cap)

    def search(i, c):
        lo, hi = c
        mid = 0.5 * (lo + hi)
        ge = jnp.sum(jnp.where(aff >= mid, 1.0, 0.0), axis=1, keepdims=True) >= capf
        return jnp.where(ge, mid, lo), jnp.where(ge, hi, mid)

    lo, _ = lax.fori_loop(0, ROUTE_SEARCH_STEPS, search,
                          (jnp.zeros((nr, 1), F32), jnp.full((nr, 1), 2.0, F32)))
    thr = jnp.min(jnp.where(aff >= lo, aff, jnp.inf), axis=1, keepdims=True)
    gt = aff > thr
    eq = aff == thr
    n_gt = jnp.sum(jnp.where(gt, 1.0, 0.0), axis=1, keepdims=True)
    eq_rank = _prefix_count(jnp.where(eq, 1.0, 0.0))
    sel = jnp.where(gt, 1.0, jnp.where(eq, jnp.where(eq_rank <= capf - n_gt, 1.0, 0.0), 0.0))
    incl = _prefix_count(sel)
    isel = incl * sel
    for r in range(nr):
        incl_s[r] = incl[r:r + 1, :]
        isel_s[r] = isel[r:r + 1, :]
        aff_s[r] = aff[r:r + 1, :]

    ns = min(cap, ROUTE_SLOTS)

    def compact(r, _):
        for c0 in range(0, cap, ns):
            slot = (lax.broadcasted_iota(I32, (ns, LANES), 0) + c0).astype(F32)
            cnt = jnp.zeros((ns, LANES), F32)
            gat = jnp.zeros((ns, LANES), F32)
            for k in range(lt // LANES):
                ks = slice(k * LANES, (k + 1) * LANES)
                cnt = cnt + jnp.where(incl_s[r, :, ks] <= slot, 1.0, 0.0)
                gat = gat + jnp.where(isel_s[r, :, ks] == slot + 1.0, aff_s[r, :, ks], 0.0)
            tot = lax.dot_general(jnp.ones((SUBLANES, LANES), BF16), cnt.astype(BF16),
                                  (((1,), (1,)), ((), ())), preferred_element_type=F32)
            idx_ref[r, :, c0:c0 + ns] = tot[0:1, :].astype(I32)
            gate_ref[r, c0:c0 + ns, :] = jnp.sum(gat, axis=1, keepdims=True)
        return 0

    lax.fori_loop(0, nr, compact, 0)


def _route(aff_em, lt, cap):
    nr = aff_em.shape[0] * (aff_em.shape[1] // lt)
    return pl.pallas_call(
        functools.partial(_route_kernel, cap=cap, lt=lt),
        out_shape=[jax.ShapeDtypeStruct((nr, 1, cap), I32),
                   jax.ShapeDtypeStruct((nr, cap, 1), F32)],
        scratch_shapes=[pltpu.VMEM((nr, 1, lt), F32)] * 3,
        compiler_params=pltpu.CompilerParams(vmem_limit_bytes=VMEM_LIMIT),
        name="route",
    )(aff_em)


BATCH_GROUP = 4
WEIGHT_DMA_PRIORITY = 1


def _ffn_kernel(idx_ref, h2_ref, gate_ref, wg_ref, wu_ref, wd_ref, o_ref, ring, lhs, acc, wgs, wus, wds,
                wgb, wub, wdb, sems, wsem, *, cap, seq, chunk):
    p = pl.program_id(0)
    e = pl.program_id(1)
    f = pl.program_id(2)
    bb = pl.program_id(3)
    ne = pl.num_programs(1)
    nf = pl.num_programs(2)
    steps = nf * BATCH_GROUP
    cpb = cap // chunk
    n_items = pl.num_programs(0) * ne
    w = p * ne + e
    s = f * BATCH_GROUP + bb
    g = w * steps + s

    def issue_chunk(item, k, slot):
        bi = lax.div(item, ne) * BATCH_GROUP + lax.div(k, cpb)
        base = (bi * ne + lax.rem(item, ne)) * cap + lax.rem(k, cpb) * chunk
        for i in range(chunk):
            tok = bi * seq + idx_ref[base + i]
            pltpu.make_async_copy(h2_ref.at[pl.ds(tok, 1)], ring.at[slot, pl.ds(i, 1)],
                                  sems.at[slot]).start(priority=i % 2)

    def wait_chunk(slot):
        pltpu.make_async_copy(h2_ref.at[pl.ds(0, chunk)], ring.at[slot], sems.at[slot]).wait()

    def drain_chunk(slot, par, k):
        wait_chunk(slot)
        r0 = pl.multiple_of(lax.rem(k, cpb) * chunk, chunk)
        lhs[par, lax.div(k, cpb), pl.ds(r0, chunk), :] = ring[slot].astype(BF16)

    tf = wgb.shape[1]

    def weight_copies(tile):
        ei = lax.rem(lax.div(tile, nf), ne)
        c0 = pl.multiple_of(lax.rem(tile, nf) * tf, tf)
        return (pltpu.make_async_copy(wg_ref.at[ei, :, pl.ds(c0, tf)], wgs, wsem.at[0]),
                pltpu.make_async_copy(wu_ref.at[ei, :, pl.ds(c0, tf)], wus, wsem.at[1]),
                pltpu.make_async_copy(wd_ref.at[ei, pl.ds(c0, tf), :], wds, wsem.at[2]))

    @pl.when(g == 0)
    def _prologue():
        for cp in weight_copies(0):
            cp.start(priority=WEIGHT_DMA_PRIORITY)

        def fetch(k, _):
            issue_chunk(0, k, 0)
            drain_chunk(0, 0, k)
            return 0
        lax.fori_loop(0, steps - 1, fetch, 0)
        issue_chunk(0, steps - 1, 1)

    gq = g - 1 + steps
    drain_chunk(lax.rem(g + 1, 2), lax.rem(lax.div(gq, steps), 2), lax.rem(gq, steps))

    @pl.when(bb == 0)
    def _next_weights():
        tile = w * nf + f
        for cp in weight_copies(tile):
            cp.wait()
        wgb[...] = wgs[...].astype(BF16)
        wub[...] = wus[...].astype(BF16)
        wdb[...] = wds[...].astype(BF16)

        @pl.when(tile + 1 < n_items * nf)
        def _prefetch():
            for cp in weight_copies(tile + 1):
                cp.start(priority=WEIGHT_DMA_PRIORITY)

    @pl.when(f == 0)
    def _zero():
        acc[bb] = jnp.zeros((cap, D), F32)

    issue_chunk(jnp.minimum(w + 1, n_items - 1), s, lax.rem(g, 2))
    x = lhs[lax.rem(w, 2), bb]
    a = _dot(x, wgb[...])
    gg = _dot(x, wub[...])
    hm = (_silu(a) * gg).astype(BF16)
    acc[bb] += _dot(hm, wdb[...])

    @pl.when(f == nf - 1)
    def _emit():
        gate = gate_ref[0]
        for j in range(ROW_TILES):
            o_ref[pl.ds(j, cap, stride=ROW_TILES), :] = acc[bb, :, j * LANES:(j + 1) * LANES] * gate

    @pl.when(g == n_items * steps - 1)
    def _tail():
        wait_chunk(lax.rem(g, 2))


def _ffn(idx_flat, h2, gate_col, w_gate, w_up, w_down, nb, seq, cap, tf):
    ne, _, ff = w_gate.shape
    nf = ff // tf
    chunk = cap // nf
    assert nb % BATCH_GROUP == 0 and cap % nf == 0 and chunk % 16 == 0
    pair_block = lambda p, e, f, bb, idx: ((p * BATCH_GROUP + jnp.where(f == nf - 1, bb, 0)) * ne + e, 0)
    grid_spec = pltpu.PrefetchScalarGridSpec(
        num_scalar_prefetch=1,
        grid=(nb // BATCH_GROUP, ne, nf, BATCH_GROUP),
        in_specs=[pl.BlockSpec(memory_space=pl.ANY),
                  pl.BlockSpec((1, cap, 1), lambda p, e, f, bb, idx: pair_block(p, e, f, bb, idx) + (0,)),
                  pl.BlockSpec(memory_space=pl.ANY),
                  pl.BlockSpec(memory_space=pl.ANY),
                  pl.BlockSpec(memory_space=pl.ANY)],
        out_specs=pl.BlockSpec((cap * ROW_TILES, LANES), pair_block),
        scratch_shapes=[pltpu.VMEM((2, chunk, D), F32),
                        pltpu.VMEM((2, BATCH_GROUP, cap, D), BF16),
                        pltpu.VMEM((BATCH_GROUP, cap, D), F32),
                        pltpu.VMEM((D, tf), F32),
                        pltpu.VMEM((D, tf), F32),
                        pltpu.VMEM((tf, D), F32),
                        pltpu.VMEM((D, tf), BF16),
                        pltpu.VMEM((D, tf), BF16),
                        pltpu.VMEM((tf, D), BF16),
                        pltpu.SemaphoreType.DMA((2,)),
                        pltpu.SemaphoreType.DMA((3,))],
    )
    return pl.pallas_call(
        functools.partial(_ffn_kernel, cap=cap, seq=seq, chunk=chunk),
        grid_spec=grid_spec,
        out_shape=jax.ShapeDtypeStruct((nb * ne * cap * ROW_TILES, LANES), F32),
        compiler_params=pltpu.CompilerParams(dimension_semantics=("arbitrary",) * 4, vmem_limit_bytes=VMEM_LIMIT),
        name="ffn",
    )(idx_flat, h2, gate_col, w_gate, w_up, w_down)


COMBINE_UNROLL = 8
FINAL_TM = 256


def _combine_final_kernel(idx_ref, ye_ref, mod_ref, pn_ref, xn_ref, o_ref, y_acc, xbuf, obuf, in_sem, out_sem,
                          *, cap, seq):
    b = pl.program_id(0)
    e = pl.program_id(1)
    pair = b * pl.num_programs(1) + e
    nt = seq // FINAL_TM

    @pl.when(e == 0)
    def _zero():
        y_acc[...] = jnp.zeros(y_acc.shape, F32)

    def group(gi, _):
        base = gi * COMBINE_UNROLL
        toks = [idx_ref[pair * cap + base + u] for u in range(COMBINE_UNROLL)]
        refs = [y_acc.at[lax.shift_right_logical(t, FINAL_TM.bit_length() - 1),
                         pl.ds(pl.multiple_of((t & (FINAL_TM - 1)) * ROW_TILES, ROW_TILES), ROW_TILES)] for t in toks]
        rows = [refs[u][...] + ye_ref[base + u] for u in range(COMBINE_UNROLL)]
        for u in range(COMBINE_UNROLL):
            refs[u][...] = rows[u]
        return 0

    lax.fori_loop(0, cap // COMBINE_UNROLL, group, 0)

    @pl.when(e == pl.num_programs(1) - 1)
    def _epilogue():
        g2 = mod_ref[0][5:6]

        def rows_of(t):
            return pl.ds(pl.multiple_of(b * seq + t * FINAL_TM, FINAL_TM), FINAL_TM)

        def xn_copy(t, slot):
            return pltpu.make_async_copy(xn_ref.at[rows_of(t)], xbuf.at[slot], in_sem.at[slot])

        def out_copy(t, slot):
            return pltpu.make_async_copy(obuf.at[slot], o_ref.at[rows_of(t)], out_sem.at[slot])

        xn_copy(0, 0).start()

        def tile(t, _):
            slot = lax.rem(t, 2)
            xn_copy(t, slot).wait()

            @pl.when(t + 1 < nt)
            def _next_in():
                xn_copy(t + 1, 1 - slot).start()

            @pl.when(t >= 2)
            def _free_out():
                out_copy(t - 2, slot).wait()

            yt = y_acc.at[t]
            ss = jnp.zeros((FINAL_TM, LANES), F32)
            for j in range(ROW_TILES):
                v = yt[pl.ds(j, FINAL_TM, stride=ROW_TILES), :]
                obuf[slot, :, j * LANES:(j + 1) * LANES] = v
                ss = ss + v * v
            rs = lax.rsqrt(jnp.sum(ss, axis=-1, keepdims=True) * (1.0 / D) + EPS)
            obuf[slot] = xbuf[slot] + g2 * (obuf[slot] * rs * pn_ref[...])
            out_copy(t, slot).start()
            return 0

        lax.fori_loop(0, nt, tile, 0)
        for t in range(max(nt - 2, 0), nt):
            out_copy(t, t % 2).wait()


def _combine_final(idx_flat, ye, xn, mod3, pn, nb, seq, cap):
    ne = ye.shape[0] // (nb * cap * ROW_TILES)
    ye4 = ye.reshape(nb * ne, cap, ROW_TILES, LANES)
    assert seq % FINAL_TM == 0 and cap % COMBINE_UNROLL == 0
    grid_spec = pltpu.PrefetchScalarGridSpec(
        num_scalar_prefetch=1,
        grid=(nb, ne),
        in_specs=[pl.BlockSpec((None, cap, ROW_TILES, LANES), lambda b, e, idx: (b * ne + e, 0, 0, 0)),
                  pl.BlockSpec((1, N_MOD, D), lambda b, e, idx: (b, 0, 0)),
                  pl.BlockSpec((1, D), lambda b, e, idx: (0, 0)),
                  pl.BlockSpec(memory_space=pl.ANY)],
        out_specs=pl.BlockSpec(memory_space=pl.ANY),
        scratch_shapes=[pltpu.VMEM((seq // FINAL_TM, FINAL_TM * ROW_TILES, LANES), F32),
                        pltpu.VMEM((2, FINAL_TM, D), F32),
                        pltpu.VMEM((2, FINAL_TM, D), F32),
                        pltpu.SemaphoreType.DMA((2,)),
                        pltpu.SemaphoreType.DMA((2,))],
    )
    return pl.pallas_call(
        functools.partial(_combine_final_kernel, cap=cap, seq=seq),
        grid_spec=grid_spec,
        out_shape=jax.ShapeDtypeStruct(xn.shape, F32),
        compiler_params=pltpu.CompilerParams(dimension_semantics=("arbitrary",) * 2, vmem_limit_bytes=VMEM_LIMIT),
        name="combine_final",
    )(idx_flat, ye4, mod3, pn, xn)


def _rope_partner():
    q = QK_ROPE // 4
    return np.concatenate([np.arange(q, 2 * q), np.arange(0, q), np.arange(3 * q, 4 * q), np.arange(2 * q, 3 * q)])


def _rope_table(length):
    pos = np.arange(length)
    half = QK_ROPE // 2
    inv = (1.0 / (ROPE_THETA ** (np.arange(0, half, 2, dtype=np.float32) / half))).astype(np.float32)
    ar = (pos // GRID_W).astype(np.float32)[:, None] * inv
    ac = (pos % GRID_W).astype(np.float32)[:, None] * inv
    cos = np.concatenate([np.cos(ar), np.cos(ar), np.cos(ac), np.cos(ac)], axis=1)
    sin = np.concatenate([-np.sin(ar), np.sin(ar), -np.sin(ac), np.sin(ac)], axis=1)
    z = np.zeros((length, LANES - QK_ROPE), np.float32)
    return jnp.asarray(np.concatenate([cos, z, sin, z], axis=1).astype(np.float32))


def _identity_rope_table(length):
    t = np.zeros((length, 2 * LANES), np.float32)
    t[:, :QK_ROPE] = 1.0
    return jnp.asarray(t)


def _prep_w_in(w_in):
    perm = _rope_partner()
    kpe = w_in[:, 2 * Q_LORA:2 * Q_LORA + QK_ROPE]
    rest = w_in[:, 2 * Q_LORA + QK_ROPE:]
    return jnp.concatenate([w_in[:, :2 * Q_LORA], kpe, kpe[:, perm], rest], axis=1).astype(BF16)


def _prep_w_uq(w_uq):
    perm = _rope_partner()
    w = w_uq.reshape(Q_LORA, HEADS, QK_NOPE + QK_ROPE)
    return jnp.concatenate([w, w[:, :, QK_NOPE:][:, :, perm]], axis=2).reshape(Q_LORA, HEADS * QK_PAD).astype(BF16)


def _prep_w_ukv(w_ukv):
    w = w_ukv.reshape(KV_LORA, HEADS, 2, QK_NOPE)
    return w.transpose(0, 2, 1, 3).reshape(KV_LORA, 2 * HEADS * QK_NOPE).astype(BF16)


def kernel(x, c, ctx, c_ctx, w_ada, b_ada, pre_norm1, w_in, q_norm_w, w_uq, kv_norm_w, w_ukv, sgu_norm_w, sgu_w,
           sgu_b, w_out, post_norm1, pre_norm2, w_router, b_router, w_e_gate, w_e_up, w_e_down, post_norm2):
    nb, seq, _ = x.shape
    lc = ctx.shape[1]
    depth = w_ada.shape[0]
    assert depth == 1 and seq % 512 == 0 and lc % 128 == 0 and nb < SUBLANES
    cap = CAP_FACTOR * seq // N_EXPERTS
    n = nb * seq
    tm = 256

    cc = jnp.zeros((SUBLANES, D), F32).at[:nb].set(c).at[nb].set(c_ctx)
    mod3 = _ada(cc, w_ada[0], b_ada[0]).reshape(SUBLANES, N_MOD, D)

    row = lambda w: w.reshape(1, -1)
    win = _prep_w_in(w_in[0])
    wuq = _prep_w_uq(w_uq[0])
    wukv = _prep_w_ukv(w_ukv[0])
    x2 = x.reshape(n, D)
    q, k, v, sgu = _inproj(x2, mod3, row(pre_norm1[0]), win, row(q_norm_w[0]), wuq, row(kv_norm_w[0]), wukv,
                           row(sgu_norm_w[0]), sgu_w[0].astype(BF16), sgu_b[0].T, _rope_table(seq), seq, 2 * tm)
    kc, vc = _ctx_kv(ctx.reshape(nb * lc, D), mod3, nb, row(pre_norm1[0]), win, row(kv_norm_w[0]), wukv,
                     _identity_rope_table(lc), lc)

    attn = _attention(q.reshape(nb, seq, -1), k.reshape(nb, seq, -1), v.reshape(nb, seq, -1),
                      kc.reshape(nb, lc, -1), vc.reshape(nb, lc, -1), tq=512, tk=4096)

    wr = jnp.zeros((D, LANES), BF16).at[:, :N_EXPERTS].set(w_router[0].astype(BF16))
    br = jnp.zeros((1, LANES), F32).at[0, :N_EXPERTS].set(b_router[0])
    xn, h2_rows, aff = _outproj(attn.reshape(n, MLA_WIDTH), sgu, x2, mod3, w_out[0].astype(BF16),
                                row(post_norm1[0]), row(pre_norm2[0]), wr, br, seq, 2 * tm)

    idx_col, gate_col = _route(aff, seq, cap)
    idx_flat = idx_col.reshape(-1)

    ye = _ffn(idx_flat, h2_rows, gate_col, w_e_gate[0], w_e_up[0], w_e_down[0], nb, seq, cap, tf=256)
    out = _combine_final(idx_flat, ye, xn, mod3, row(post_norm2[0]), nb, seq, cap)
    return out.reshape(nb, seq, D)
```

```python
import functools

import numpy as np
import jax
import jax.numpy as jnp
from jax import lax
from jax.experimental import pallas as pl
from jax.experimental.pallas import tpu as pltpu

F32 = jnp.float32
BF16 = jnp.bfloat16
I32 = jnp.int32

D = 2048
GRID_W = 64
EPS = 1e-6
N_MOD = 6
HEADS = 8
Q_LORA = 512
KV_LORA = 512
QK_NOPE = 128
QK_ROPE = 64
V_HEAD = 128
ROPE_THETA = 10000.0
ATTN_SCALE = (QK_NOPE + QK_ROPE) ** -0.5
Q_SCALE = ATTN_SCALE * float(np.log2(np.e))
SGU_HEADS = 8
SGU_DIM = 128
CHUNK = 128
MLA_WIDTH = HEADS * V_HEAD
SGU_WIDTH = SGU_HEADS * SGU_DIM
N_EXPERTS = 16
CAP_FACTOR = 2
EXPERT_FF = D

LANES = 128
SUBLANES = 8
ROW_TILES = D // LANES
QK_PAD = 256
VMEM_LIMIT = 56 * 1024 * 1024

C_Q = 0
C_KV = C_Q + Q_LORA
C_KPE = C_KV + KV_LORA
C_U = 0
C_V = C_U + SGU_WIDTH


def _rms(x, w):
    return x * lax.rsqrt(jnp.mean(x * x, axis=-1, keepdims=True) + EPS) * w


def _gelu_tanh(x):
    return 0.5 * x * (1.0 + jnp.tanh(np.sqrt(2.0 / np.pi).astype(np.float32) * (x + 0.044715 * (x * x * x))))


def _silu(x):
    return x * (1.0 / (1.0 + jnp.exp(-x)))


def _dot(a, b):
    return jnp.dot(a, b, preferred_element_type=F32)


def _ada_kernel(c_ref, w_ref, b_ref, o_ref):
    s = _silu(c_ref[...]).astype(BF16)
    o_ref[...] = _dot(s, w_ref[...].astype(BF16)) + b_ref[...]


def _ada(cc, w_ada, b_ada):
    n = w_ada.shape[1]
    tn = 1024
    return pl.pallas_call(
        _ada_kernel,
        grid=(n // tn,),
        in_specs=[pl.BlockSpec((SUBLANES, D), lambda j: (0, 0)),
                  pl.BlockSpec((D, tn), lambda j: (0, j)),
                  pl.BlockSpec((1, tn), lambda j: (0, j))],
        out_specs=pl.BlockSpec((SUBLANES, tn), lambda j: (0, j)),
        out_shape=jax.ShapeDtypeStruct((SUBLANES, n), F32),
        compiler_params=pltpu.CompilerParams(dimension_semantics=("arbitrary",), vmem_limit_bytes=VMEM_LIMIT),
        name="ada",
    )(cc, w_ada, b_ada.reshape(1, n))


def _rope(blk, cos, sin):
    return blk * cos + pltpu.roll(blk, QK_ROPE, axis=1) * sin


def _kv_path(hb, win_ref, kvn_ref, wukv_ref, cos, sin, k_ref, v_ref):
    ckv_kp = _dot(hb, win_ref[:, C_KV:C_KV + KV_LORA + LANES])
    ckvn = _rms(ckv_kp[:, :KV_LORA], kvn_ref[...]).astype(BF16)
    kn = _dot(ckvn, wukv_ref[:, :HEADS * QK_NOPE])
    v_ref[...] = _dot(ckvn, wukv_ref[:, HEADS * QK_NOPE:]).astype(BF16)
    kpe = _rope(ckv_kp[:, KV_LORA:], cos, sin).astype(BF16)
    for h in range(HEADS):
        k_ref[:, h * QK_PAD:h * QK_PAD + QK_NOPE] = kn[:, h * QK_NOPE:(h + 1) * QK_NOPE].astype(BF16)
        k_ref[:, h * QK_PAD + QK_NOPE:(h + 1) * QK_PAD] = kpe


def _inproj_kernel(x_ref, mod_ref, pn_ref, win_ref, wsg_ref, qn_ref, wuq_ref, kvn_ref, wukv_ref,
                   sgn_ref, sgw_ref, sgb_ref, cs_ref, q_ref, k_ref, v_ref, sgu_ref):
    tm = x_ref.shape[0]
    mod = mod_ref[0]
    h = _rms(x_ref[...], pn_ref[...]) * (1.0 + mod[1:2]) + mod[0:1]
    hb = h.astype(BF16)
    cos = cs_ref[:, :LANES]
    sin = cs_ref[:, LANES:]

    _kv_path(hb, win_ref, kvn_ref, wukv_ref, cos, sin, k_ref, v_ref)

    cq = _dot(hb, win_ref[:, C_Q:C_Q + Q_LORA])
    cqn = _rms(cq, qn_ref[...]).astype(BF16)
    for hp in range(HEADS // 2):
        qm2 = _dot(cqn, wuq_ref[:, hp * 2 * QK_PAD:(hp + 1) * 2 * QK_PAD])
        for i in range(2):
            h_ = 2 * hp + i
            qm = qm2[:, i * QK_PAD:(i + 1) * QK_PAD]
            q_ref[:, h_ * QK_PAD:h_ * QK_PAD + QK_NOPE] = (qm[:, :QK_NOPE] * Q_SCALE).astype(BF16)
            q_ref[:, h_ * QK_PAD + QK_NOPE:(h_ + 1) * QK_PAD] = (
                _rope(qm[:, QK_NOPE:], cos, sin) * Q_SCALE).astype(BF16)

    for gp in range(SGU_HEADS // 2):
        u2 = _gelu_tanh(_dot(hb, wsg_ref[:, C_U + gp * 2 * SGU_DIM:C_U + (gp + 1) * 2 * SGU_DIM]))
        vs2 = _gelu_tanh(_dot(hb, wsg_ref[:, C_V + gp * 2 * SGU_DIM:C_V + (gp + 1) * 2 * SGU_DIM]))
        for i in range(2):
            g = 2 * gp + i
            gs = slice(g * SGU_DIM, (g + 1) * SGU_DIM)
            u = u2[:, i * SGU_DIM:(i + 1) * SGU_DIM]
            vs = vs2[:, i * SGU_DIM:(i + 1) * SGU_DIM]
            mu = jnp.mean(vs, axis=-1, keepdims=True)
            vc = vs - mu
            var = jnp.mean(vc * vc, axis=-1, keepdims=True)
            vn = (vc * lax.rsqrt(var + EPS) * sgn_ref[:, gs]).astype(BF16)
            wg = sgw_ref[g]
            bias = sgb_ref[:, g:g + 1]
            for n in range(tm // CHUNK):
                rs = slice(n * CHUNK, (n + 1) * CHUNK)
                mixed = _dot(wg, vn[rs, :]) + bias
                sgu_ref[rs, gs] = (u[rs, :] * mixed).astype(BF16)


def _ctx_kv_kernel(x_ref, mod_ref, pn_ref, win_ref, kvn_ref, wukv_ref, cs_ref, k_ref, v_ref):
    mod = mod_ref[0]
    h = _rms(x_ref[...], pn_ref[...]) * (1.0 + mod[1:2]) + mod[0:1]
    _kv_path(h.astype(BF16), win_ref, kvn_ref, wukv_ref, cs_ref[:, :LANES], cs_ref[:, LANES:], k_ref, v_ref)


def _const_spec(shape):
    nd = len(shape)
    return pl.BlockSpec(shape, lambda i: (0,) * nd, pipeline_mode=pl.Buffered(1))


def _inproj(x2, mod3, pn, win, wsg, qn, wuq, kvn, wukv, sgn, sgw, sgb, cs, rows_per_batch, tm):
    n = x2.shape[0]
    tpb = rows_per_batch // tm
    row = lambda w: pl.BlockSpec((tm, w), lambda i: (i, 0))
    return pl.pallas_call(
        _inproj_kernel,
        grid=(n // tm,),
        in_specs=[row(D),
                  pl.BlockSpec((1, N_MOD, D), lambda i: (i // tpb, 0, 0)),
                  _const_spec((1, D)), _const_spec(win.shape), _const_spec(wsg.shape), _const_spec((1, Q_LORA)),
                  _const_spec(wuq.shape), _const_spec((1, KV_LORA)), _const_spec(wukv.shape),
                  _const_spec((1, SGU_WIDTH)), _const_spec(sgw.shape), _const_spec(sgb.shape),
                  pl.BlockSpec((tm, 2 * LANES), lambda i: (i % tpb, 0))],
        out_specs=[row(HEADS * QK_PAD), row(HEADS * QK_PAD), row(MLA_WIDTH), row(SGU_WIDTH)],
        out_shape=[jax.ShapeDtypeStruct((n, HEADS * QK_PAD), BF16),
                   jax.ShapeDtypeStruct((n, HEADS * QK_PAD), BF16),
                   jax.ShapeDtypeStruct((n, MLA_WIDTH), BF16),
                   jax.ShapeDtypeStruct((n, SGU_WIDTH), BF16)],
        compiler_params=pltpu.CompilerParams(dimension_semantics=("arbitrary",), vmem_limit_bytes=VMEM_LIMIT),
        name="inproj",
    )(x2, mod3, pn, win, wsg, qn, wuq, kvn, wukv, sgn, sgw, sgb, cs)


def _ctx_kv(c2, mod3, ctx_row, pn, win, kvn, wukv, cs, tm):
    n = c2.shape[0]
    row = lambda w: pl.BlockSpec((tm, w), lambda i: (i, 0))
    return pl.pallas_call(
        _ctx_kv_kernel,
        grid=(n // tm,),
        in_specs=[row(D),
                  pl.BlockSpec((1, N_MOD, D), lambda i: (ctx_row, 0, 0)),
                  _const_spec((1, D)), _const_spec(win.shape), _const_spec((1, KV_LORA)),
                  _const_spec(wukv.shape),
                  pl.BlockSpec((tm, 2 * LANES), lambda i: (0, 0))],
        out_specs=[row(HEADS * QK_PAD), row(MLA_WIDTH)],
        out_shape=[jax.ShapeDtypeStruct((n, HEADS * QK_PAD), BF16),
                   jax.ShapeDtypeStruct((n, MLA_WIDTH), BF16)],
        compiler_params=pltpu.CompilerParams(dimension_semantics=("arbitrary",), vmem_limit_bytes=VMEM_LIMIT),
        name="ctx_kv",
    )(c2, mod3, pn, win, kvn, wukv, cs)


def _attn_kernel(q_ref, k_ref, v_ref, kc_ref, vc_ref, o_ref, *, tk, streams):
    tq = q_ref.shape[1]
    ts = tq // streams
    nk = k_ref.shape[1] // tk
    qs = [q_ref[0, s * ts:(s + 1) * ts, :] for s in range(streams)]

    def step(q, kb, vb, carry):
        m, l, acc = carry
        s = lax.dot_general(q, kb, (((1,), (1,)), ((), ())), preferred_element_type=F32)
        m_new = jnp.maximum(m, jnp.max(s, axis=-1, keepdims=True))
        alpha = jnp.exp2(m - m_new)
        p = jnp.exp2(s - m_new)
        l = alpha * l + jnp.sum(p, axis=-1, keepdims=True)
        acc = alpha * acc + _dot(p.astype(BF16), vb)
        return m_new, l, acc

    def body(j, carries):
        off = pl.multiple_of(j * tk, tk)
        kb = k_ref[0, pl.ds(off, tk), :]
        vb = v_ref[0, pl.ds(off, tk), :]
        return tuple(step(qs[s], kb, vb, carries[s]) for s in range(streams))

    init = (jnp.full((ts, 1), -jnp.inf, F32), jnp.zeros((ts, 1), F32), jnp.zeros((ts, V_HEAD), F32))
    carries = (init,) * streams
    for j in range(nk):
        carries = body(j, carries)
    for s in range(streams):
        m, l, acc = step(qs[s], kc_ref[0], vc_ref[0], carries[s])
        o_ref[0, s * ts:(s + 1) * ts, :] = (acc / l).astype(BF16)


def _attention(q, k, v, kc, vc, tq, tk):
    b, l, _ = q.shape
    lc = kc.shape[1]
    tk = min(tk, l)
    assert l % tk == 0 and l % tq == 0
    return pl.pallas_call(
        functools.partial(_attn_kernel, tk=tk, streams=1),
        grid=(b, HEADS, l // tq),
        in_specs=[pl.BlockSpec((1, tq, QK_PAD), lambda b_, h, i: (b_, i, h)),
                  pl.BlockSpec((1, l, QK_PAD), lambda b_, h, i: (b_, 0, h)),
                  pl.BlockSpec((1, l, V_HEAD), lambda b_, h, i: (b_, 0, h)),
                  pl.BlockSpec((1, lc, QK_PAD), lambda b_, h, i: (b_, 0, h)),
                  pl.BlockSpec((1, lc, V_HEAD), lambda b_, h, i: (b_, 0, h))],
        out_specs=pl.BlockSpec((1, tq, V_HEAD), lambda b_, h, i: (b_, i, h)),
        out_shape=jax.ShapeDtypeStruct((b, l, MLA_WIDTH), BF16),
        compiler_params=pltpu.CompilerParams(dimension_semantics=("arbitrary",) * 3, vmem_limit_bytes=VMEM_LIMIT),
        name="attn",
    )(q, k, v, kc, vc)


OUTPROJ_SUB = 256


def _outproj_kernel(at_ref, sg_ref, x_ref, mod_ref, wo_ref, pn1_ref, pn2_ref, wr_ref, br_ref,
                    xn_ref, h2_ref, aff_ref):
    tm = x_ref.shape[0]
    mod = mod_ref[0]
    for r0 in range(0, tm, OUTPROJ_SUB):
        rs = slice(r0, r0 + OUTPROJ_SUB)
        y = _dot(at_ref[rs, :], wo_ref[:MLA_WIDTH, :]) + _dot(sg_ref[rs, :], wo_ref[MLA_WIDTH:, :])
        xn = x_ref[rs, :] + mod[2:3] * _rms(y, pn1_ref[...])
        xn_ref[rs, :] = xn
        h2 = _rms(xn, pn2_ref[...]) * (1.0 + mod[4:5]) + mod[3:4]
        h2_ref[rs, :] = h2
        logits = _dot(h2.astype(BF16), wr_ref[...]) + br_ref[...]
        lane = lax.broadcasted_iota(I32, logits.shape, 1)
        logits = jnp.where(lane < N_EXPERTS, logits, -jnp.inf)
        e = jnp.exp(logits - jnp.max(logits, axis=-1, keepdims=True))
        aff = e / jnp.sum(e, axis=-1, keepdims=True)
        aff_ref[:, rs] = jnp.transpose(aff)[:N_EXPERTS, :]


def _outproj(attn2, sgu2, x2, mod3, wo, pn1, pn2, wr, br, rows_per_batch, tm):
    n = x2.shape[0]
    tpb = rows_per_batch // tm
    row = lambda w: pl.BlockSpec((tm, w), lambda i: (i, 0))
    return pl.pallas_call(
        _outproj_kernel,
        grid=(n // tm,),
        in_specs=[row(MLA_WIDTH), row(SGU_WIDTH), row(D),
                  pl.BlockSpec((1, N_MOD, D), lambda i: (i // tpb, 0, 0)),
                  _const_spec(wo.shape), _const_spec((1, D)), _const_spec((1, D)),
                  _const_spec(wr.shape), _const_spec((1, LANES))],
        out_specs=[row(D), row(D), pl.BlockSpec((N_EXPERTS, tm), lambda i: (0, i))],
        out_shape=[jax.ShapeDtypeStruct((n, D), F32),
                   jax.ShapeDtypeStruct((n, D), F32),
                   jax.ShapeDtypeStruct((N_EXPERTS, n), F32)],
        compiler_params=pltpu.CompilerParams(dimension_semantics=("arbitrary",), vmem_limit_bytes=VMEM_LIMIT),
        name="outproj",
    )(attn2, sgu2, x2, mod3, wo, pn1, pn2, wr, br)


def _prefix_count(x01):
    r, t = x01.shape
    nb = t // LANES
    stacked = jnp.concatenate([x01[:, k * LANES:(k + 1) * LANES] for k in range(nb)], axis=0).astype(BF16)
    ii = lax.broadcasted_iota(I32, (LANES, LANES), 0)
    jj = lax.broadcasted_iota(I32, (LANES, LANES), 1)
    tri = jnp.where(ii <= jj, 1.0, 0.0).astype(BF16)
    within = _dot(stacked, tri)
    off = jnp.zeros((r, 1), F32)
    blocks = []
    for k in range(nb):
        w = within[k * r:(k + 1) * r, :]
        blocks.append(w + off)
        off = off + w[:, LANES - 1:LANES]
    return jnp.concatenate(blocks, axis=1)


ROUTE_SLOTS = 128
ROUTE_SEARCH_STEPS = 192


def _route_kernel(aff_ref, idx_ref, gate_ref, incl_s, isel_s, aff_s, *, cap, lt):
    aff = jnp.concatenate([aff_ref[:, b * lt:(b + 1) * lt] for b in range(aff_ref.shape[1] // lt)], axis=0)
    nr = aff.shape[0]
    capf = float(cap)

    def moving(c):
        i, lo, hi = c
        mid = 0.5 * (lo + hi)
        inside = jnp.where(mid != lo, jnp.where(mid != hi, 1.0, 0.0), 0.0)
        return jnp.logical_and(i < ROUTE_SEARCH_STEPS, jnp.max(inside) > 0.0)

    def search(c):
        i, lo, hi = c
        mid = 0.5 * (lo + hi)
        ge = jnp.sum(jnp.where(aff >= mid, 1.0, 0.0), axis=1, keepdims=True) >= capf
        return i + 1, jnp.where(ge, mid, lo), jnp.where(ge, hi, mid)

    _, lo, _ = lax.while_loop(moving, search,
                              (jnp.int32(0), jnp.zeros((nr, 1), F32), jnp.full((nr, 1), 2.0, F32)))
    thr = jnp.min(jnp.where(aff >= lo, aff, jnp.inf), axis=1, keepdims=True)
    gt = aff > thr
    eq = aff == thr
    n_gt = jnp.sum(jnp.where(gt, 1.0, 0.0), axis=1, keepdims=True)
    eq_rank = _prefix_count(jnp.where(eq, 1.0, 0.0))
    sel = jnp.where(gt, 1.0, jnp.where(eq, jnp.where(eq_rank <= capf - n_gt, 1.0, 0.0), 0.0))
    incl = _prefix_count(sel)
    isel = incl * sel
    for r in range(nr):
        incl_s[r] = incl[r:r + 1, :]
        isel_s[r] = isel[r:r + 1, :]
        aff_s[r] = aff[r:r + 1, :]

    ns = min(cap, ROUTE_SLOTS)

    def compact(r, _):
        for c0 in range(0, cap, ns):
            slot = (lax.broadcasted_iota(I32, (ns, LANES), 0) + c0).astype(F32)
            cnt = jnp.zeros((ns, LANES), F32)
            gat = jnp.zeros((ns, LANES), F32)
            for k in range(lt // LANES):
                ks = slice(k * LANES, (k + 1) * LANES)
                cnt = cnt + jnp.where(incl_s[r, :, ks] <= slot, 1.0, 0.0)
                gat = gat + jnp.where(isel_s[r, :, ks] == slot + 1.0, aff_s[r, :, ks], 0.0)
            tot = lax.dot_general(jnp.ones((SUBLANES, LANES), BF16), cnt.astype(BF16),
                                  (((1,), (1,)), ((), ())), preferred_element_type=F32)
            idx_ref[r, :, c0:c0 + ns] = tot[0:1, :].astype(I32)
            gate_ref[r, c0:c0 + ns, :] = jnp.sum(gat, axis=1, keepdims=True)
        return 0

    lax.fori_loop(0, nr, compact, 0)


def _route(aff_em, lt, cap):
    nr = aff_em.shape[0] * (aff_em.shape[1] // lt)
    return pl.pallas_call(
        functools.partial(_route_kernel, cap=cap, lt=lt),
        out_shape=[jax.ShapeDtypeStruct((nr, 1, cap), I32),
                   jax.ShapeDtypeStruct((nr, cap, 1), F32)],
        scratch_shapes=[pltpu.VMEM((nr, 1, lt), F32)] * 3,
        compiler_params=pltpu.CompilerParams(vmem_limit_bytes=VMEM_LIMIT),
        name="route",
    )(aff_em)


BATCH_GROUP = 4
WEIGHT_DMA_PRIORITY = 1


def _ffn_kernel(idx_ref, h2_ref, gate_ref, wg_ref, wu_ref, wd_ref, o_ref, ring, lhs, acc, wgs, wus, wds,
                wgb, wub, wdb, sems, wsem, *, cap, seq, chunk):
    p = pl.program_id(0)
    e = pl.program_id(1)
    f = pl.program_id(2)
    bb = pl.program_id(3)
    ne = pl.num_programs(1)
    nf = pl.num_programs(2)
    steps = nf * BATCH_GROUP
    cpb = cap // chunk
    n_items = pl.num_programs(0) * ne
    w = p * ne + e
    s = f * BATCH_GROUP + bb
    g = w * steps + s

    def issue_chunk(item, k, slot):
        bi = lax.div(item, ne) * BATCH_GROUP + lax.div(k, cpb)
        base = (bi * ne + lax.rem(item, ne)) * cap + lax.rem(k, cpb) * chunk
        for i in range(chunk):
            tok = bi * seq + idx_ref[base + i]
            pltpu.make_async_copy(h2_ref.at[pl.ds(tok, 1)], ring.at[slot, pl.ds(i, 1)],
                                  sems.at[slot]).start(priority=i % 2)

    def wait_chunk(slot):
        pltpu.make_async_copy(h2_ref.at[pl.ds(0, chunk)], ring.at[slot], sems.at[slot]).wait()

    def drain_chunk(slot, par, k):
        wait_chunk(slot)
        r0 = pl.multiple_of(lax.rem(k, cpb) * chunk, chunk)
        lhs[par, lax.div(k, cpb), pl.ds(r0, chunk), :] = ring[slot].astype(BF16)

    tf = wgb.shape[1]

    def weight_copies(tile):
        ei = lax.rem(lax.div(tile, nf), ne)
        c0 = pl.multiple_of(lax.rem(tile, nf) * tf, tf)
        return (pltpu.make_async_copy(wg_ref.at[ei, :, pl.ds(c0, tf)], wgs, wsem.at[0]),
                pltpu.make_async_copy(wu_ref.at[ei, :, pl.ds(c0, tf)], wus, wsem.at[1]),
                pltpu.make_async_copy(wd_ref.at[ei, pl.ds(c0, tf), :], wds, wsem.at[2]))

    @pl.when(g == 0)
    def _prologue():
        for cp in weight_copies(0):
            cp.start(priority=WEIGHT_DMA_PRIORITY)

        def fetch(k, _):
            issue_chunk(0, k, 0)
            drain_chunk(0, 0, k)
            return 0
        lax.fori_loop(0, steps - 1, fetch, 0)
        issue_chunk(0, steps - 1, 1)

    gq = g - 1 + steps
    drain_chunk(lax.rem(g + 1, 2), lax.rem(lax.div(gq, steps), 2), lax.rem(gq, steps))

    @pl.when(bb == 0)
    def _next_weights():
        tile = w * nf + f
        for cp in weight_copies(tile):
            cp.wait()
        wgb[...] = wgs[...].astype(BF16)
        wub[...] = wus[...].astype(BF16)
        wdb[...] = wds[...].astype(BF16)

        @pl.when(tile + 1 < n_items * nf)
        def _prefetch():
            for cp in weight_copies(tile + 1):
                cp.start(priority=WEIGHT_DMA_PRIORITY)

    @pl.when(f == 0)
    def _zero():
        acc[bb] = jnp.zeros((cap, D), F32)

    issue_chunk(jnp.minimum(w + 1, n_items - 1), s, lax.rem(g, 2))
    x = lhs[lax.rem(w, 2), bb]
    a = _dot(x, wgb[...])
    gg = _dot(x, wub[...])
    hm = (_silu(a) * gg).astype(BF16)
    acc[bb] += _dot(hm, wdb[...])

    @pl.when(f == nf - 1)
    def _emit():
        gate = gate_ref[0]
        for j in range(ROW_TILES):
            o_ref[pl.ds(j, cap, stride=ROW_TILES), :] = acc[bb, :, j * LANES:(j + 1) * LANES] * gate

    @pl.when(g == n_items * steps - 1)
    def _tail():
        wait_chunk(lax.rem(g, 2))


def _ffn(idx_flat, h2, gate_col, w_gate, w_up, w_down, nb, seq, cap, tf):
    ne, _, ff = w_gate.shape
    nf = ff // tf
    chunk = cap // nf
    assert nb % BATCH_GROUP == 0 and cap % nf == 0 and chunk % 16 == 0
    pair_block = lambda p, e, f, bb, idx: ((p * BATCH_GROUP + jnp.where(f == nf - 1, bb, 0)) * ne + e, 0)
    grid_spec = pltpu.PrefetchScalarGridSpec(
        num_scalar_prefetch=1,
        grid=(nb // BATCH_GROUP, ne, nf, BATCH_GROUP),
        in_specs=[pl.BlockSpec(memory_space=pl.ANY),
                  pl.BlockSpec((1, cap, 1), lambda p, e, f, bb, idx: pair_block(p, e, f, bb, idx) + (0,)),
                  pl.BlockSpec(memory_space=pl.ANY),
                  pl.BlockSpec(memory_space=pl.ANY),
                  pl.BlockSpec(memory_space=pl.ANY)],
        out_specs=pl.BlockSpec((cap * ROW_TILES, LANES), pair_block),
        scratch_shapes=[pltpu.VMEM((2, chunk, D), F32),
                        pltpu.VMEM((2, BATCH_GROUP, cap, D), BF16),
                        pltpu.VMEM((BATCH_GROUP, cap, D), F32),
                        pltpu.VMEM((D, tf), F32),
                        pltpu.VMEM((D, tf), F32),
                        pltpu.VMEM((tf, D), F32),
                        pltpu.VMEM((D, tf), BF16),
                        pltpu.VMEM((D, tf), BF16),
                        pltpu.VMEM((tf, D), BF16),
                        pltpu.SemaphoreType.DMA((2,)),
                        pltpu.SemaphoreType.DMA((3,))],
    )
    return pl.pallas_call(
        functools.partial(_ffn_kernel, cap=cap, seq=seq, chunk=chunk),
        grid_spec=grid_spec,
        out_shape=jax.ShapeDtypeStruct((nb * ne * cap * ROW_TILES, LANES), F32),
        compiler_params=pltpu.CompilerParams(dimension_semantics=("arbitrary",) * 4, vmem_limit_bytes=VMEM_LIMIT),
        name="ffn",
    )(idx_flat, h2, gate_col, w_gate, w_up, w_down)


COMBINE_UNROLL = 8
FINAL_TM = 256


def _combine_final_kernel(idx_ref, ye_ref, mod_ref, pn_ref, xn_ref, o_ref, y_acc, xbuf, obuf, in_sem, out_sem,
                          *, cap, seq):
    b = pl.program_id(0)
    e = pl.program_id(1)
    pair = b * pl.num_programs(1) + e
    nt = seq // FINAL_TM

    @pl.when(e == 0)
    def _zero():
        y_acc[...] = jnp.zeros(y_acc.shape, F32)

    def group(gi, _):
        base = gi * COMBINE_UNROLL
        toks = [idx_ref[pair * cap + base + u] for u in range(COMBINE_UNROLL)]
        refs = [y_acc.at[lax.shift_right_logical(t, FINAL_TM.bit_length() - 1),
                         pl.ds(pl.multiple_of((t & (FINAL_TM - 1)) * ROW_TILES, ROW_TILES), ROW_TILES)] for t in toks]
        rows = [refs[u][...] + ye_ref[base + u] for u in range(COMBINE_UNROLL)]
        for u in range(COMBINE_UNROLL):
            refs[u][...] = rows[u]
        return 0

    lax.fori_loop(0, cap // COMBINE_UNROLL, group, 0)

    @pl.when(e == pl.num_programs(1) - 1)
    def _epilogue():
        g2 = mod_ref[0][5:6]

        def rows_of(t):
            return pl.ds(pl.multiple_of(b * seq + t * FINAL_TM, FINAL_TM), FINAL_TM)

        def xn_copy(t, slot):
            return pltpu.make_async_copy(xn_ref.at[rows_of(t)], xbuf.at[slot], in_sem.at[slot])

        def out_copy(t, slot):
            return pltpu.make_async_copy(obuf.at[slot], o_ref.at[rows_of(t)], out_sem.at[slot])

        xn_copy(0, 0).start()

        def tile(t, _):
            slot = lax.rem(t, 2)
            xn_copy(t, slot).wait()

            @pl.when(t + 1 < nt)
            def _next_in():
                xn_copy(t + 1, 1 - slot).start()

            @pl.when(t >= 2)
            def _free_out():
                out_copy(t - 2, slot).wait()

            yt = y_acc.at[t]
            ss = jnp.zeros((FINAL_TM, LANES), F32)
            for j in range(ROW_TILES):
                v = yt[pl.ds(j, FINAL_TM, stride=ROW_TILES), :]
                obuf[slot, :, j * LANES:(j + 1) * LANES] = v
                ss = ss + v * v
            rs = lax.rsqrt(jnp.sum(ss, axis=-1, keepdims=True) * (1.0 / D) + EPS)
            obuf[slot] = xbuf[slot] + g2 * (obuf[slot] * rs * pn_ref[...])
            out_copy(t, slot).start()
            return 0

        lax.fori_loop(0, nt, tile, 0)
        for t in range(max(nt - 2, 0), nt):
            out_copy(t, t % 2).wait()


def _combine_final(idx_flat, ye, xn, mod3, pn, nb, seq, cap):
    ne = ye.shape[0] // (nb * cap * ROW_TILES)
    ye4 = ye.reshape(nb * ne, cap, ROW_TILES, LANES)
    assert seq % FINAL_TM == 0 and cap % COMBINE_UNROLL == 0
    grid_spec = pltpu.PrefetchScalarGridSpec(
        num_scalar_prefetch=1,
        grid=(nb, ne),
        in_specs=[pl.BlockSpec((None, cap, ROW_TILES, LANES), lambda b, e, idx: (b * ne + e, 0, 0, 0)),
                  pl.BlockSpec((1, N_MOD, D), lambda b, e, idx: (b, 0, 0)),
                  pl.BlockSpec((1, D), lambda b, e, idx: (0, 0)),
                  pl.BlockSpec(memory_space=pl.ANY)],
        out_specs=pl.BlockSpec(memory_space=pl.ANY),
        scratch_shapes=[pltpu.VMEM((seq // FINAL_TM, FINAL_TM * ROW_TILES, LANES), F32),
                        pltpu.VMEM((2, FINAL_TM, D), F32),
                        pltpu.VMEM((2, FINAL_TM, D), F32),
                        pltpu.SemaphoreType.DMA((2,)),
                        pltpu.SemaphoreType.DMA((2,))],
    )
    return pl.pallas_call(
        functools.partial(_combine_final_kernel, cap=cap, seq=seq),
        grid_spec=grid_spec,
        out_shape=jax.ShapeDtypeStruct(xn.shape, F32),
        compiler_params=pltpu.CompilerParams(dimension_semantics=("arbitrary",) * 2, vmem_limit_bytes=VMEM_LIMIT),
        name="combine_final",
    )(idx_flat, ye4, mod3, pn, xn)


def _rope_partner():
    q = QK_ROPE // 4
    return np.concatenate([np.arange(q, 2 * q), np.arange(0, q), np.arange(3 * q, 4 * q), np.arange(2 * q, 3 * q)])


def _rope_table(length):
    pos = np.arange(length)
    half = QK_ROPE // 2
    inv = (1.0 / (ROPE_THETA ** (np.arange(0, half, 2, dtype=np.float32) / half))).astype(np.float32)
    ar = (pos // GRID_W).astype(np.float32)[:, None] * inv
    ac = (pos % GRID_W).astype(np.float32)[:, None] * inv
    cos = np.concatenate([np.cos(ar), np.cos(ar), np.cos(ac), np.cos(ac)], axis=1)
    sin = np.concatenate([-np.sin(ar), np.sin(ar), -np.sin(ac), np.sin(ac)], axis=1)
    z = np.zeros((length, LANES - QK_ROPE), np.float32)
    return jnp.asarray(np.concatenate([cos, z, sin, z], axis=1).astype(np.float32))


def _identity_rope_table(length):
    t = np.zeros((length, 2 * LANES), np.float32)
    t[:, :QK_ROPE] = 1.0
    return jnp.asarray(t)


def _prep_w_in(w_in):
    perm = _rope_partner()
    kpe = w_in[:, 2 * Q_LORA:2 * Q_LORA + QK_ROPE]
    head = jnp.concatenate([w_in[:, :2 * Q_LORA], kpe, kpe[:, perm]], axis=1).astype(BF16)
    return head, w_in[:, 2 * Q_LORA + QK_ROPE:].astype(BF16)


def _prep_w_uq(w_uq):
    perm = _rope_partner()
    w = w_uq.reshape(Q_LORA, HEADS, QK_NOPE + QK_ROPE)
    return jnp.concatenate([w, w[:, :, QK_NOPE:][:, :, perm]], axis=2).reshape(Q_LORA, HEADS * QK_PAD).astype(BF16)


def _prep_w_ukv(w_ukv):
    w = w_ukv.reshape(KV_LORA, HEADS, 2, QK_NOPE)
    return w.transpose(0, 2, 1, 3).reshape(KV_LORA, 2 * HEADS * QK_NOPE).astype(BF16)


def kernel(x, c, ctx, c_ctx, w_ada, b_ada, pre_norm1, w_in, q_norm_w, w_uq, kv_norm_w, w_ukv, sgu_norm_w, sgu_w,
           sgu_b, w_out, post_norm1, pre_norm2, w_router, b_router, w_e_gate, w_e_up, w_e_down, post_norm2):
    nb, seq, _ = x.shape
    lc = ctx.shape[1]
    depth = w_ada.shape[0]
    assert depth == 1 and seq % 512 == 0 and lc % 128 == 0 and nb < SUBLANES
    cap = CAP_FACTOR * seq // N_EXPERTS
    n = nb * seq
    tm = 256

    cc = jnp.zeros((SUBLANES, D), F32).at[:nb].set(c).at[nb].set(c_ctx)
    mod3 = _ada(cc, w_ada[0], b_ada[0]).reshape(SUBLANES, N_MOD, D)

    row = lambda w: w.reshape(1, -1)
    win, wsg = _prep_w_in(w_in[0])
    wuq = _prep_w_uq(w_uq[0])
    wukv = _prep_w_ukv(w_ukv[0])
    x2 = x.reshape(n, D)
    q, k, v, sgu = _inproj(x2, mod3, row(pre_norm1[0]), win, wsg, row(q_norm_w[0]), wuq, row(kv_norm_w[0]), wukv,
                           row(sgu_norm_w[0]), sgu_w[0].astype(BF16), sgu_b[0].T, _rope_table(seq), seq, 2 * tm)
    kc, vc = _ctx_kv(ctx.reshape(nb * lc, D), mod3, nb, row(pre_norm1[0]), win, row(kv_norm_w[0]), wukv,
                     _identity_rope_table(lc), lc)

    attn = _attention(q.reshape(nb, seq, -1), k.reshape(nb, seq, -1), v.reshape(nb, seq, -1),
                      kc.reshape(nb, lc, -1), vc.reshape(nb, lc, -1), tq=512, tk=4096)

    wr = jnp.zeros((D, LANES), BF16).at[:, :N_EXPERTS].set(w_router[0].astype(BF16))
    br = jnp.zeros((1, LANES), F32).at[0, :N_EXPERTS].set(b_router[0])
    xn, h2_rows, aff = _outproj(attn.reshape(n, MLA_WIDTH), sgu, x2, mod3, w_out[0].astype(BF16),
                                row(post_norm1[0]), row(pre_norm2[0]), wr, br, seq, 2 * tm)

    idx_col, gate_col = _route(aff, seq, cap)
    idx_flat = idx_col.reshape(-1)

    ye = _ffn(idx_flat, h2_rows, gate_col, w_e_gate[0], w_e_up[0], w_e_down[0], nb, seq, cap, tf=256)
    out = _combine_final(idx_flat, ye, xn, mod3, row(post_norm2[0]), nb, seq, cap)
    return out.reshape(nb, seq, D)
```

```python
import functools

import numpy as np
import jax
import jax.numpy as jnp
from jax import lax
from jax.experimental import pallas as pl
from jax.experimental.pallas import tpu as pltpu

F32 = jnp.float32
BF16 = jnp.bfloat16
I32 = jnp.int32

D = 2048
GRID_W = 64
EPS = 1e-6
N_MOD = 6
HEADS = 8
Q_LORA = 512
KV_LORA = 512
QK_NOPE = 128
QK_ROPE = 64
V_HEAD = 128
ROPE_THETA = 10000.0
ATTN_SCALE = (QK_NOPE + QK_ROPE) ** -0.5
Q_SCALE = ATTN_SCALE * float(np.log2(np.e))
SGU_HEADS = 8
SGU_DIM = 128
CHUNK = 128
MLA_WIDTH = HEADS * V_HEAD
SGU_WIDTH = SGU_HEADS * SGU_DIM
N_EXPERTS = 16
CAP_FACTOR = 2
EXPERT_FF = D

LANES = 128
SUBLANES = 8
ROW_TILES = D // LANES
QK_PAD = 256
VMEM_LIMIT = 56 * 1024 * 1024

C_Q = 0
C_KV = C_Q + Q_LORA
C_KPE = C_KV + KV_LORA
C_U = 0
C_V = C_U + SGU_WIDTH


def _rms(x, w):
    return x * lax.rsqrt(jnp.mean(x * x, axis=-1, keepdims=True) + EPS) * w


def _gelu_tanh(x):
    return 0.5 * x * (1.0 + jnp.tanh(np.sqrt(2.0 / np.pi).astype(np.float32) * (x + 0.044715 * (x * x * x))))


def _silu(x):
    return x * (1.0 / (1.0 + jnp.exp(-x)))


def _dot(a, b):
    return jnp.dot(a, b, preferred_element_type=F32)


def _ada_kernel(c_ref, w_ref, b_ref, o_ref):
    s = _silu(c_ref[...]).astype(BF16)
    o_ref[...] = _dot(s, w_ref[...].astype(BF16)) + b_ref[...]


def _ada(cc, w_ada, b_ada):
    n = w_ada.shape[1]
    tn = 1024
    return pl.pallas_call(
        _ada_kernel,
        grid=(n // tn,),
        in_specs=[pl.BlockSpec((SUBLANES, D), lambda j: (0, 0)),
                  pl.BlockSpec((D, tn), lambda j: (0, j)),
                  pl.BlockSpec((1, tn), lambda j: (0, j))],
        out_specs=pl.BlockSpec((SUBLANES, tn), lambda j: (0, j)),
        out_shape=jax.ShapeDtypeStruct((SUBLANES, n), F32),
        compiler_params=pltpu.CompilerParams(dimension_semantics=("arbitrary",), vmem_limit_bytes=VMEM_LIMIT),
        name="ada",
    )(cc, w_ada, b_ada.reshape(1, n))


def _rope(blk, cos, sin):
    return blk * cos + pltpu.roll(blk, QK_ROPE, axis=1) * sin


def _kv_path(hb, win_ref, kvn_ref, wukv_ref, cos, sin, k_ref, v_ref):
    ckv_kp = _dot(hb, win_ref[:, C_KV:C_KV + KV_LORA + LANES])
    ckvn = _rms(ckv_kp[:, :KV_LORA], kvn_ref[...]).astype(BF16)
    kn = _dot(ckvn, wukv_ref[:, :HEADS * QK_NOPE])
    v_ref[...] = _dot(ckvn, wukv_ref[:, HEADS * QK_NOPE:]).astype(BF16)
    kpe = _rope(ckv_kp[:, KV_LORA:], cos, sin).astype(BF16)
    for h in range(HEADS):
        k_ref[:, h * QK_PAD:h * QK_PAD + QK_NOPE] = kn[:, h * QK_NOPE:(h + 1) * QK_NOPE].astype(BF16)
        k_ref[:, h * QK_PAD + QK_NOPE:(h + 1) * QK_PAD] = kpe


def _inproj_kernel(x_ref, mod_ref, pn_ref, win_ref, wsg_ref, qn_ref, wuq_ref, kvn_ref, wukv_ref,
                   sgn_ref, sgw_ref, sgb_ref, cs_ref, q_ref, k_ref, v_ref, sgu_ref):
    tm = x_ref.shape[0]
    mod = mod_ref[0]
    h = _rms(x_ref[...], pn_ref[...]) * (1.0 + mod[1:2]) + mod[0:1]
    hb = h.astype(BF16)
    cos = cs_ref[:, :LANES]
    sin = cs_ref[:, LANES:]

    _kv_path(hb, win_ref, kvn_ref, wukv_ref, cos, sin, k_ref, v_ref)

    cq = _dot(hb, win_ref[:, C_Q:C_Q + Q_LORA])
    cqn = _rms(cq, qn_ref[...]).astype(BF16)
    for hp in range(HEADS // 2):
        qm2 = _dot(cqn, wuq_ref[:, hp * 2 * QK_PAD:(hp + 1) * 2 * QK_PAD])
        for i in range(2):
            h_ = 2 * hp + i
            qm = qm2[:, i * QK_PAD:(i + 1) * QK_PAD]
            q_ref[:, h_ * QK_PAD:h_ * QK_PAD + QK_NOPE] = (qm[:, :QK_NOPE] * Q_SCALE).astype(BF16)
            q_ref[:, h_ * QK_PAD + QK_NOPE:(h_ + 1) * QK_PAD] = (
                _rope(qm[:, QK_NOPE:], cos, sin) * Q_SCALE).astype(BF16)

    for gp in range(SGU_HEADS // 2):
        u2 = _gelu_tanh(_dot(hb, wsg_ref[:, C_U + gp * 2 * SGU_DIM:C_U + (gp + 1) * 2 * SGU_DIM]))
        vs2 = _gelu_tanh(_dot(hb, wsg_ref[:, C_V + gp * 2 * SGU_DIM:C_V + (gp + 1) * 2 * SGU_DIM]))
        for i in range(2):
            g = 2 * gp + i
            gs = slice(g * SGU_DIM, (g + 1) * SGU_DIM)
            u = u2[:, i * SGU_DIM:(i + 1) * SGU_DIM]
            vs = vs2[:, i * SGU_DIM:(i + 1) * SGU_DIM]
            mu = jnp.mean(vs, axis=-1, keepdims=True)
            vc = vs - mu
            var = jnp.mean(vc * vc, axis=-1, keepdims=True)
            vn = (vc * lax.rsqrt(var + EPS) * sgn_ref[:, gs]).astype(BF16)
            wg = sgw_ref[g]
            bias = sgb_ref[:, g:g + 1]
            for n in range(tm // CHUNK):
                rs = slice(n * CHUNK, (n + 1) * CHUNK)
                mixed = _dot(wg, vn[rs, :]) + bias
                sgu_ref[rs, gs] = (u[rs, :] * mixed).astype(BF16)


def _ctx_kv_kernel(x_ref, mod_ref, pn_ref, win_ref, kvn_ref, wukv_ref, cs_ref, k_ref, v_ref):
    mod = mod_ref[0]
    h = _rms(x_ref[...], pn_ref[...]) * (1.0 + mod[1:2]) + mod[0:1]
    _kv_path(h.astype(BF16), win_ref, kvn_ref, wukv_ref, cs_ref[:, :LANES], cs_ref[:, LANES:], k_ref, v_ref)


def _const_spec(shape):
    nd = len(shape)
    return pl.BlockSpec(shape, lambda i: (0,) * nd, pipeline_mode=pl.Buffered(1))


def _inproj(x2, mod3, pn, win, wsg, qn, wuq, kvn, wukv, sgn, sgw, sgb, cs, rows_per_batch, tm):
    n = x2.shape[0]
    tpb = rows_per_batch // tm
    row = lambda w: pl.BlockSpec((tm, w), lambda i: (i, 0))
    return pl.pallas_call(
        _inproj_kernel,
        grid=(n // tm,),
        in_specs=[row(D),
                  pl.BlockSpec((1, N_MOD, D), lambda i: (i // tpb, 0, 0)),
                  _const_spec((1, D)), _const_spec(win.shape), _const_spec(wsg.shape), _const_spec((1, Q_LORA)),
                  _const_spec(wuq.shape), _const_spec((1, KV_LORA)), _const_spec(wukv.shape),
                  _const_spec((1, SGU_WIDTH)), _const_spec(sgw.shape), _const_spec(sgb.shape),
                  pl.BlockSpec((tm, 2 * LANES), lambda i: (i % tpb, 0))],
        out_specs=[row(HEADS * QK_PAD), row(HEADS * QK_PAD), row(MLA_WIDTH), row(SGU_WIDTH)],
        out_shape=[jax.ShapeDtypeStruct((n, HEADS * QK_PAD), BF16),
                   jax.ShapeDtypeStruct((n, HEADS * QK_PAD), BF16),
                   jax.ShapeDtypeStruct((n, MLA_WIDTH), BF16),
                   jax.ShapeDtypeStruct((n, SGU_WIDTH), BF16)],
        compiler_params=pltpu.CompilerParams(dimension_semantics=("arbitrary",), vmem_limit_bytes=VMEM_LIMIT),
        name="inproj",
    )(x2, mod3, pn, win, wsg, qn, wuq, kvn, wukv, sgn, sgw, sgb, cs)


def _ctx_kv(c2, mod3, ctx_row, pn, win, kvn, wukv, cs, tm):
    n = c2.shape[0]
    row = lambda w: pl.BlockSpec((tm, w), lambda i: (i, 0))
    return pl.pallas_call(
        _ctx_kv_kernel,
        grid=(n // tm,),
        in_specs=[row(D),
                  pl.BlockSpec((1, N_MOD, D), lambda i: (ctx_row, 0, 0)),
                  _const_spec((1, D)), _const_spec(win.shape), _const_spec((1, KV_LORA)),
                  _const_spec(wukv.shape),
                  pl.BlockSpec((tm, 2 * LANES), lambda i: (0, 0))],
        out_specs=[row(HEADS * QK_PAD), row(MLA_WIDTH)],
        out_shape=[jax.ShapeDtypeStruct((n, HEADS * QK_PAD), BF16),
                   jax.ShapeDtypeStruct((n, MLA_WIDTH), BF16)],
        compiler_params=pltpu.CompilerParams(dimension_semantics=("arbitrary",), vmem_limit_bytes=VMEM_LIMIT),
        name="ctx_kv",
    )(c2, mod3, pn, win, kvn, wukv, cs)


def _attn_kernel(q_ref, k_ref, v_ref, kc_ref, vc_ref, o_ref, *, tk, streams):
    tq = q_ref.shape[1]
    ts = tq // streams
    nk = k_ref.shape[1] // tk
    qs = [q_ref[0, s * ts:(s + 1) * ts, :] for s in range(streams)]

    def step(q, kb, vb, carry):
        m, l, acc = carry
        s = lax.dot_general(q, kb, (((1,), (1,)), ((), ())), preferred_element_type=F32)
        m_new = jnp.maximum(m, jnp.max(s, axis=-1, keepdims=True))
        alpha = jnp.exp2(m - m_new)
        p = jnp.exp2(s - m_new)
        l = alpha * l + jnp.sum(p, axis=-1, keepdims=True)
        acc = alpha * acc + _dot(p.astype(BF16), vb)
        return m_new, l, acc

    def body(j, carries):
        off = pl.multiple_of(j * tk, tk)
        kb = k_ref[0, pl.ds(off, tk), :]
        vb = v_ref[0, pl.ds(off, tk), :]
        return tuple(step(qs[s], kb, vb, carries[s]) for s in range(streams))

    init = (jnp.full((ts, 1), -jnp.inf, F32), jnp.zeros((ts, 1), F32), jnp.zeros((ts, V_HEAD), F32))
    carries = (init,) * streams
    for j in range(nk):
        carries = body(j, carries)
    for s in range(streams):
        m, l, acc = step(qs[s], kc_ref[0], vc_ref[0], carries[s])
        o_ref[0, s * ts:(s + 1) * ts, :] = (acc / l).astype(BF16)


def _attention(q, k, v, kc, vc, tq, tk):
    b, l, _ = q.shape
    lc = kc.shape[1]
    tk = min(tk, l)
    assert l % tk == 0 and l % tq == 0
    return pl.pallas_call(
        functools.partial(_attn_kernel, tk=tk, streams=1),
        grid=(b, HEADS, l // tq),
        in_specs=[pl.BlockSpec((1, tq, QK_PAD), lambda b_, h, i: (b_, i, h)),
                  pl.BlockSpec((1, l, QK_PAD), lambda b_, h, i: (b_, 0, h)),
                  pl.BlockSpec((1, l, V_HEAD), lambda b_, h, i: (b_, 0, h)),
                  pl.BlockSpec((1, lc, QK_PAD), lambda b_, h, i: (b_, 0, h)),
                  pl.BlockSpec((1, lc, V_HEAD), lambda b_, h, i: (b_, 0, h))],
        out_specs=pl.BlockSpec((1, tq, V_HEAD), lambda b_, h, i: (b_, i, h)),
        out_shape=jax.ShapeDtypeStruct((b, l, MLA_WIDTH), BF16),
        compiler_params=pltpu.CompilerParams(dimension_semantics=("arbitrary",) * 3, vmem_limit_bytes=VMEM_LIMIT),
        name="attn",
    )(q, k, v, kc, vc)


OUTPROJ_SUB = 256


def _outproj_kernel(at_ref, sg_ref, x_ref, mod_ref, wo_ref, pn1_ref, pn2_ref, wr_ref, br_ref,
                    xn_ref, h2_ref, aff_ref):
    tm = x_ref.shape[0]
    mod = mod_ref[0]
    for r0 in range(0, tm, OUTPROJ_SUB):
        rs = slice(r0, r0 + OUTPROJ_SUB)
        y = _dot(at_ref[rs, :], wo_ref[:MLA_WIDTH, :]) + _dot(sg_ref[rs, :], wo_ref[MLA_WIDTH:, :])
        xn = x_ref[rs, :] + mod[2:3] * _rms(y, pn1_ref[...])
        xn_ref[rs, :] = xn
        h2 = _rms(xn, pn2_ref[...]) * (1.0 + mod[4:5]) + mod[3:4]
        h2_ref[rs, :] = h2
        logits = _dot(h2.astype(BF16), wr_ref[...]) + br_ref[...]
        lane = lax.broadcasted_iota(I32, logits.shape, 1)
        logits = jnp.where(lane < N_EXPERTS, logits, -jnp.inf)
        e = jnp.exp(logits - jnp.max(logits, axis=-1, keepdims=True))
        aff = e / jnp.sum(e, axis=-1, keepdims=True)
        aff_ref[:, rs] = jnp.transpose(aff)[:N_EXPERTS, :]


def _outproj(attn2, sgu2, x2, mod3, wo, pn1, pn2, wr, br, rows_per_batch, tm):
    n = x2.shape[0]
    tpb = rows_per_batch // tm
    row = lambda w: pl.BlockSpec((tm, w), lambda i: (i, 0))
    return pl.pallas_call(
        _outproj_kernel,
        grid=(n // tm,),
        in_specs=[row(MLA_WIDTH), row(SGU_WIDTH), row(D),
                  pl.BlockSpec((1, N_MOD, D), lambda i: (i // tpb, 0, 0)),
                  _const_spec(wo.shape), _const_spec((1, D)), _const_spec((1, D)),
                  _const_spec(wr.shape), _const_spec((1, LANES))],
        out_specs=[row(D), row(D), pl.BlockSpec((N_EXPERTS, tm), lambda i: (0, i))],
        out_shape=[jax.ShapeDtypeStruct((n, D), F32),
                   jax.ShapeDtypeStruct((n, D), F32),
                   jax.ShapeDtypeStruct((N_EXPERTS, n), F32)],
        compiler_params=pltpu.CompilerParams(dimension_semantics=("arbitrary",), vmem_limit_bytes=VMEM_LIMIT),
        name="outproj",
    )(attn2, sgu2, x2, mod3, wo, pn1, pn2, wr, br)


def _prefix_count(x01):
    r, t = x01.shape
    nb = t // LANES
    stacked = jnp.concatenate([x01[:, k * LANES:(k + 1) * LANES] for k in range(nb)], axis=0).astype(BF16)
    ii = lax.broadcasted_iota(I32, (LANES, LANES), 0)
    jj = lax.broadcasted_iota(I32, (LANES, LANES), 1)
    tri = jnp.where(ii <= jj, 1.0, 0.0).astype(BF16)
    within = _dot(stacked, tri)
    off = jnp.zeros((r, 1), F32)
    blocks = []
    for k in range(nb):
        w = within[k * r:(k + 1) * r, :]
        blocks.append(w + off)
        off = off + w[:, LANES - 1:LANES]
    return jnp.concatenate(blocks, axis=1)


ROUTE_SEARCH_STEPS = 192


def _route_kernel(aff_ref, idx_ref, gate_ref, ends_s, tab_hi, tab_lo, tab_a1, tab_a2, tab_a3, *, cap, lt):
    aff = jnp.concatenate([aff_ref[:, b * lt:(b + 1) * lt] for b in range(aff_ref.shape[1] // lt)], axis=0)
    nr = aff.shape[0]
    capf = float(cap)

    def moving(c):
        i, lo, hi = c
        mid = 0.5 * (lo + hi)
        inside = jnp.where(mid != lo, jnp.where(mid != hi, 1.0, 0.0), 0.0)
        return jnp.logical_and(i < ROUTE_SEARCH_STEPS, jnp.max(inside) > 0.0)

    def search(c):
        i, lo, hi = c
        mid = 0.5 * (lo + hi)
        ge = jnp.sum(jnp.where(aff >= mid, 1.0, 0.0), axis=1, keepdims=True) >= capf
        return i + 1, jnp.where(ge, mid, lo), jnp.where(ge, hi, mid)

    _, lo, _ = lax.while_loop(moving, search,
                              (jnp.int32(0), jnp.zeros((nr, 1), F32), jnp.full((nr, 1), 2.0, F32)))
    thr = jnp.min(jnp.where(aff >= lo, aff, jnp.inf), axis=1, keepdims=True)
    gt = aff > thr
    eq = aff == thr
    n_gt = jnp.sum(jnp.where(gt, 1.0, 0.0), axis=1, keepdims=True)
    eq_rank = _prefix_count(jnp.where(eq, 1.0, 0.0))
    sel = jnp.where(gt, 1.0, jnp.where(eq, jnp.where(eq_rank <= capf - n_gt, 1.0, 0.0), 0.0))
    incl = _prefix_count(sel)
    nblk = lt // LANES
    assert nblk <= LANES
    half = jnp.floor(incl * 0.5)
    a1 = aff.astype(BF16).astype(F32)
    r1 = aff - a1
    a2 = r1.astype(BF16).astype(F32)
    tables = ((tab_hi, half), (tab_lo, incl - 2.0 * half), (tab_a1, a1), (tab_a2, a2), (tab_a3, r1 - a2))
    lane_r = lax.broadcasted_iota(I32, (nr, LANES), 1)
    ends = jnp.full((nr, LANES), 2.0 * lt, F32)
    for tab, _ in tables:
        tab[...] = jnp.zeros(tab.shape, F32)
    for k in range(nblk):
        ks = slice(k * LANES, (k + 1) * LANES)
        for tab, val in tables:
            tab[pl.ds(k, nr, stride=LANES), :] = val[:, ks]
        ends = jnp.where(lane_r == k, incl[:, (k + 1) * LANES - 1:(k + 1) * LANES], ends)
    for r in range(nr):
        ends_s[r] = ends[r:r + 1, :]

    slot = lax.broadcasted_iota(I32, (cap, LANES), 0).astype(F32)
    lane_c = lax.broadcasted_iota(I32, (cap, LANES), 1).astype(F32)
    ones = jnp.ones((SUBLANES, LANES), BF16)

    def compact(r, _):
        base = pl.multiple_of(r * LANES, LANES)
        slab = lambda tab: tab[pl.ds(base, LANES), :].astype(BF16)
        ge = jnp.where(slot >= ends_s[r], 1.0, 0.0)
        kc = jnp.sum(ge, axis=1, keepdims=True)
        onehot = jnp.where(lane_c == kc, 1.0, 0.0).astype(BF16)
        incl_blk = 2.0 * _dot(onehot, slab(tab_hi)) + _dot(onehot, slab(tab_lo))
        before = jnp.where(incl_blk <= slot, 1.0, 0.0)
        pos = jnp.sum(before, axis=1, keepdims=True)
        tot = lax.dot_general(ones, (before + float(LANES) * ge).astype(BF16),
                              (((1,), (1,)), ((), ())), preferred_element_type=F32)
        idx_ref[r] = tot[0:1, :].astype(I32)
        aff_blk = (_dot(onehot, slab(tab_a1)) + _dot(onehot, slab(tab_a2))) + _dot(onehot, slab(tab_a3))
        gate_ref[r] = jnp.sum(jnp.where(lane_c == pos, aff_blk, 0.0), axis=1, keepdims=True)
        return 0

    lax.fori_loop(0, nr, compact, 0)


def _route(aff_em, lt, cap):
    nr = aff_em.shape[0] * (aff_em.shape[1] // lt)
    return pl.pallas_call(
        functools.partial(_route_kernel, cap=cap, lt=lt),
        out_shape=[jax.ShapeDtypeStruct((nr, 1, cap), I32),
                   jax.ShapeDtypeStruct((nr, cap, 1), F32)],
        scratch_shapes=[pltpu.VMEM((nr, 1, LANES), F32)] + [pltpu.VMEM((nr * LANES, LANES), F32)] * 5,
        compiler_params=pltpu.CompilerParams(vmem_limit_bytes=VMEM_LIMIT),
        name="route",
    )(aff_em)


BATCH_GROUP = 4
WEIGHT_DMA_PRIORITY = 1


def _ffn_kernel(idx_ref, h2_ref, gate_ref, wg_ref, wu_ref, wd_ref, o_ref, ring, lhs, acc, wgs, wus, wds,
                wgb, wub, wdb, sems, wsem, *, cap, seq, chunk):
    p = pl.program_id(0)
    e = pl.program_id(1)
    f = pl.program_id(2)
    bb = pl.program_id(3)
    ne = pl.num_programs(1)
    nf = pl.num_programs(2)
    steps = nf * BATCH_GROUP
    cpb = cap // chunk
    n_items = pl.num_programs(0) * ne
    w = p * ne + e
    s = f * BATCH_GROUP + bb
    g = w * steps + s

    def issue_chunk(item, k, slot):
        bi = lax.div(item, ne) * BATCH_GROUP + lax.div(k, cpb)
        base = (bi * ne + lax.rem(item, ne)) * cap + lax.rem(k, cpb) * chunk
        for i in range(chunk):
            tok = bi * seq + idx_ref[base + i]
            pltpu.make_async_copy(h2_ref.at[pl.ds(tok, 1)], ring.at[slot, pl.ds(i, 1)],
                                  sems.at[slot]).start(priority=i % 2)

    def wait_chunk(slot):
        pltpu.make_async_copy(h2_ref.at[pl.ds(0, chunk)], ring.at[slot], sems.at[slot]).wait()

    def drain_chunk(slot, par, k):
        wait_chunk(slot)
        r0 = pl.multiple_of(lax.rem(k, cpb) * chunk, chunk)
        lhs[par, lax.div(k, cpb), pl.ds(r0, chunk), :] = ring[slot].astype(BF16)

    tf = wgb.shape[1]

    def weight_copies(tile):
        ei = lax.rem(lax.div(tile, nf), ne)
        c0 = pl.multiple_of(lax.rem(tile, nf) * tf, tf)
        return (pltpu.make_async_copy(wg_ref.at[ei, :, pl.ds(c0, tf)], wgs, wsem.at[0]),
                pltpu.make_async_copy(wu_ref.at[ei, :, pl.ds(c0, tf)], wus, wsem.at[1]),
                pltpu.make_async_copy(wd_ref.at[ei, pl.ds(c0, tf), :], wds, wsem.at[2]))

    @pl.when(g == 0)
    def _prologue():
        for cp in weight_copies(0):
            cp.start(priority=WEIGHT_DMA_PRIORITY)

        def fetch(k, _):
            issue_chunk(0, k, 0)
            drain_chunk(0, 0, k)
            return 0
        lax.fori_loop(0, steps - 1, fetch, 0)
        issue_chunk(0, steps - 1, 1)

    gq = g - 1 + steps
    drain_chunk(lax.rem(g + 1, 2), lax.rem(lax.div(gq, steps), 2), lax.rem(gq, steps))

    @pl.when(bb == 0)
    def _next_weights():
        tile = w * nf + f
        for cp in weight_copies(tile):
            cp.wait()
        wgb[...] = wgs[...].astype(BF16)
        wub[...] = wus[...].astype(BF16)
        wdb[...] = wds[...].astype(BF16)

        @pl.when(tile + 1 < n_items * nf)
        def _prefetch():
            for cp in weight_copies(tile + 1):
                cp.start(priority=WEIGHT_DMA_PRIORITY)

    @pl.when(f == 0)
    def _zero():
        acc[bb] = jnp.zeros((cap, D), F32)

    issue_chunk(jnp.minimum(w + 1, n_items - 1), s, lax.rem(g, 2))
    x = lhs[lax.rem(w, 2), bb]
    a = _dot(x, wgb[...])
    gg = _dot(x, wub[...])
    hm = (_silu(a) * gg).astype(BF16)
    acc[bb] += _dot(hm, wdb[...])

    @pl.when(f == nf - 1)
    def _emit():
        gate = gate_ref[0]
        for j in range(ROW_TILES):
            o_ref[pl.ds(j, cap, stride=ROW_TILES), :] = acc[bb, :, j * LANES:(j + 1) * LANES] * gate

    @pl.when(g == n_items * steps - 1)
    def _tail():
        wait_chunk(lax.rem(g, 2))


def _ffn(idx_flat, h2, gate_col, w_gate, w_up, w_down, nb, seq, cap, tf):
    ne, _, ff = w_gate.shape
    nf = ff // tf
    chunk = cap // nf
    assert nb % BATCH_GROUP == 0 and cap % nf == 0 and chunk % 16 == 0
    pair_block = lambda p, e, f, bb, idx: ((p * BATCH_GROUP + jnp.where(f == nf - 1, bb, 0)) * ne + e, 0)
    grid_spec = pltpu.PrefetchScalarGridSpec(
        num_scalar_prefetch=1,
        grid=(nb // BATCH_GROUP, ne, nf, BATCH_GROUP),
        in_specs=[pl.BlockSpec(memory_space=pl.ANY),
                  pl.BlockSpec((1, cap, 1), lambda p, e, f, bb, idx: pair_block(p, e, f, bb, idx) + (0,)),
                  pl.BlockSpec(memory_space=pl.ANY),
                  pl.BlockSpec(memory_space=pl.ANY),
                  pl.BlockSpec(memory_space=pl.ANY)],
        out_specs=pl.BlockSpec((cap * ROW_TILES, LANES), pair_block),
        scratch_shapes=[pltpu.VMEM((2, chunk, D), F32),
                        pltpu.VMEM((2, BATCH_GROUP, cap, D), BF16),
                        pltpu.VMEM((BATCH_GROUP, cap, D), F32),
                        pltpu.VMEM((D, tf), F32),
                        pltpu.VMEM((D, tf), F32),
                        pltpu.VMEM((tf, D), F32),
                        pltpu.VMEM((D, tf), BF16),
                        pltpu.VMEM((D, tf), BF16),
                        pltpu.VMEM((tf, D), BF16),
                        pltpu.SemaphoreType.DMA((2,)),
                        pltpu.SemaphoreType.DMA((3,))],
    )
    return pl.pallas_call(
        functools.partial(_ffn_kernel, cap=cap, seq=seq, chunk=chunk),
        grid_spec=grid_spec,
        out_shape=jax.ShapeDtypeStruct((nb * ne * cap * ROW_TILES, LANES), F32),
        compiler_params=pltpu.CompilerParams(dimension_semantics=("arbitrary",) * 4, vmem_limit_bytes=VMEM_LIMIT),
        name="ffn",
    )(idx_flat, h2, gate_col, w_gate, w_up, w_down)


COMBINE_UNROLL = 8
FINAL_TM = 256


def _combine_final_kernel(idx_ref, ye_ref, mod_ref, pn_ref, xn_ref, o_ref, y_acc, xbuf, obuf, in_sem, out_sem,
                          *, cap, seq):
    b = pl.program_id(0)
    e = pl.program_id(1)
    pair = b * pl.num_programs(1) + e
    nt = seq // FINAL_TM

    @pl.when(e == 0)
    def _zero():
        y_acc[...] = jnp.zeros(y_acc.shape, F32)

    def group(gi, _):
        base = gi * COMBINE_UNROLL
        toks = [idx_ref[pair * cap + base + u] for u in range(COMBINE_UNROLL)]
        refs = [y_acc.at[lax.shift_right_logical(t, FINAL_TM.bit_length() - 1),
                         pl.ds(pl.multiple_of((t & (FINAL_TM - 1)) * ROW_TILES, ROW_TILES), ROW_TILES)] for t in toks]
        rows = [refs[u][...] + ye_ref[base + u] for u in range(COMBINE_UNROLL)]
        for u in range(COMBINE_UNROLL):
            refs[u][...] = rows[u]
        return 0

    lax.fori_loop(0, cap // COMBINE_UNROLL, group, 0)

    @pl.when(e == pl.num_programs(1) - 1)
    def _epilogue():
        g2 = mod_ref[0][5:6]

        def rows_of(t):
            return pl.ds(pl.multiple_of(b * seq + t * FINAL_TM, FINAL_TM), FINAL_TM)

        def xn_copy(t, slot):
            return pltpu.make_async_copy(xn_ref.at[rows_of(t)], xbuf.at[slot], in_sem.at[slot])

        def out_copy(t, slot):
            return pltpu.make_async_copy(obuf.at[slot], o_ref.at[rows_of(t)], out_sem.at[slot])

        xn_copy(0, 0).start()

        def tile(t, _):
            slot = lax.rem(t, 2)
            xn_copy(t, slot).wait()

            @pl.when(t + 1 < nt)
            def _next_in():
                xn_copy(t + 1, 1 - slot).start()

            @pl.when(t >= 2)
            def _free_out():
                out_copy(t - 2, slot).wait()

            yt = y_acc.at[t]
            ss = jnp.zeros((FINAL_TM, LANES), F32)
            for j in range(ROW_TILES):
                v = yt[pl.ds(j, FINAL_TM, stride=ROW_TILES), :]
                obuf[slot, :, j * LANES:(j + 1) * LANES] = v
                ss = ss + v * v
            rs = lax.rsqrt(jnp.sum(ss, axis=-1, keepdims=True) * (1.0 / D) + EPS)
            obuf[slot] = xbuf[slot] + g2 * (obuf[slot] * rs * pn_ref[...])
            out_copy(t, slot).start()
            return 0

        lax.fori_loop(0, nt, tile, 0)
        for t in range(max(nt - 2, 0), nt):
            out_copy(t, t % 2).wait()


def _combine_final(idx_flat, ye, xn, mod3, pn, nb, seq, cap):
    ne = ye.shape[0] // (nb * cap * ROW_TILES)
    ye4 = ye.reshape(nb * ne, cap, ROW_TILES, LANES)
    assert seq % FINAL_TM == 0 and cap % COMBINE_UNROLL == 0
    grid_spec = pltpu.PrefetchScalarGridSpec(
        num_scalar_prefetch=1,
        grid=(nb, ne),
        in_specs=[pl.BlockSpec((None, cap, ROW_TILES, LANES), lambda b, e, idx: (b * ne + e, 0, 0, 0)),
                  pl.BlockSpec((1, N_MOD, D), lambda b, e, idx: (b, 0, 0)),
                  pl.BlockSpec((1, D), lambda b, e, idx: (0, 0)),
                  pl.BlockSpec(memory_space=pl.ANY)],
        out_specs=pl.BlockSpec(memory_space=pl.ANY),
        scratch_shapes=[pltpu.VMEM((seq // FINAL_TM, FINAL_TM * ROW_TILES, LANES), F32),
                        pltpu.VMEM((2, FINAL_TM, D), F32),
                        pltpu.VMEM((2, FINAL_TM, D), F32),
                        pltpu.SemaphoreType.DMA((2,)),
                        pltpu.SemaphoreType.DMA((2,))],
    )
    return pl.pallas_call(
        functools.partial(_combine_final_kernel, cap=cap, seq=seq),
        grid_spec=grid_spec,
        out_shape=jax.ShapeDtypeStruct(xn.shape, F32),
        compiler_params=pltpu.CompilerParams(dimension_semantics=("arbitrary",) * 2, vmem_limit_bytes=VMEM_LIMIT),
        name="combine_final",
    )(idx_flat, ye4, mod3, pn, xn)


def _rope_partner():
    q = QK_ROPE // 4
    return np.concatenate([np.arange(q, 2 * q), np.arange(0, q), np.arange(3 * q, 4 * q), np.arange(2 * q, 3 * q)])


def _rope_table(length):
    pos = np.arange(length)
    half = QK_ROPE // 2
    inv = (1.0 / (ROPE_THETA ** (np.arange(0, half, 2, dtype=np.float32) / half))).astype(np.float32)
    ar = (pos // GRID_W).astype(np.float32)[:, None] * inv
    ac = (pos % GRID_W).astype(np.float32)[:, None] * inv
    cos = np.concatenate([np.cos(ar), np.cos(ar), np.cos(ac), np.cos(ac)], axis=1)
    sin = np.concatenate([-np.sin(ar), np.sin(ar), -np.sin(ac), np.sin(ac)], axis=1)
    z = np.zeros((length, LANES - QK_ROPE), np.float32)
    return jnp.asarray(np.concatenate([cos, z, sin, z], axis=1).astype(np.float32))


def _identity_rope_table(length):
    t = np.zeros((length, 2 * LANES), np.float32)
    t[:, :QK_ROPE] = 1.0
    return jnp.asarray(t)


def _prep_w_in(w_in):
    perm = _rope_partner()
    kpe = w_in[:, 2 * Q_LORA:2 * Q_LORA + QK_ROPE]
    head = jnp.concatenate([w_in[:, :2 * Q_LORA], kpe, kpe[:, perm]], axis=1).astype(BF16)
    return head, w_in[:, 2 * Q_LORA + QK_ROPE:].astype(BF16)


def _prep_w_uq(w_uq):
    perm = _rope_partner()
    w = w_uq.reshape(Q_LORA, HEADS, QK_NOPE + QK_ROPE)
    return jnp.concatenate([w, w[:, :, QK_NOPE:][:, :, perm]], axis=2).reshape(Q_LORA, HEADS * QK_PAD).astype(BF16)


def _prep_w_ukv(w_ukv):
    w = w_ukv.reshape(KV_LORA, HEADS, 2, QK_NOPE)
    return w.transpose(0, 2, 1, 3).reshape(KV_LORA, 2 * HEADS * QK_NOPE).astype(BF16)


def kernel(x, c, ctx, c_ctx, w_ada, b_ada, pre_norm1, w_in, q_norm_w, w_uq, kv_norm_w, w_ukv, sgu_norm_w, sgu_w,
           sgu_b, w_out, post_norm1, pre_norm2, w_router, b_router, w_e_gate, w_e_up, w_e_down, post_norm2):
    nb, seq, _ = x.shape
    lc = ctx.shape[1]
    depth = w_ada.shape[0]
    assert depth == 1 and seq % 512 == 0 and lc % 128 == 0 and nb < SUBLANES
    cap = CAP_FACTOR * seq // N_EXPERTS
    n = nb * seq
    tm = 256

    cc = jnp.zeros((SUBLANES, D), F32).at[:nb].set(c).at[nb].set(c_ctx)
    mod3 = _ada(cc, w_ada[0], b_ada[0]).reshape(SUBLANES, N_MOD, D)

    row = lambda w: w.reshape(1, -1)
    win, wsg = _prep_w_in(w_in[0])
    wuq = _prep_w_uq(w_uq[0])
    wukv = _prep_w_ukv(w_ukv[0])
    x2 = x.reshape(n, D)
    q, k, v, sgu = _inproj(x2, mod3, row(pre_norm1[0]), win, wsg, row(q_norm_w[0]), wuq, row(kv_norm_w[0]), wukv,
                           row(sgu_norm_w[0]), sgu_w[0].astype(BF16), sgu_b[0].T, _rope_table(seq), seq, 2 * tm)
    kc, vc = _ctx_kv(ctx.reshape(nb * lc, D), mod3, nb, row(pre_norm1[0]), win, row(kv_norm_w[0]), wukv,
                     _identity_rope_table(lc), lc)

    attn = _attention(q.reshape(nb, seq, -1), k.reshape(nb, seq, -1), v.reshape(nb, seq, -1),
                      kc.reshape(nb, lc, -1), vc.reshape(nb, lc, -1), tq=512, tk=4096)

    wr = jnp.zeros((D, LANES), BF16).at[:, :N_EXPERTS].set(w_router[0].astype(BF16))
    br = jnp.zeros((1, LANES), F32).at[0, :N_EXPERTS].set(b_router[0])
    xn, h2_rows, aff = _outproj(attn.reshape(n, MLA_WIDTH), sgu, x2, mod3, w_out[0].astype(BF16),
                                row(post_norm1[0]), row(pre_norm2[0]), wr, br, seq, 2 * tm)

    idx_col, gate_col = _route(aff, seq, cap)
    idx_flat = idx_col.reshape(-1)

    ye = _ffn(idx_flat, h2_rows, gate_col, w_e_gate[0], w_e_up[0], w_e_down[0], nb, seq, cap, tf=256)
    out = _combine_final(idx_flat, ye, xn, mod3, row(post_norm2[0]), nb, seq, cap)
    return out.reshape(nb, seq, D)
```
